```python
import math
import jax, jax.numpy as jnp
from jax import lax
import numpy as np

D_MODEL = 1024
BATCH = 32
SEQ = 2048
DEPTH = 2

CTX_LEN = 256
GRID_W = 64
F32 = jnp.float32
EPS = 1e-6

N_MOD = 9
D_FF = 256 * ((8 * D_MODEL // 3 + 255) // 256)
A_INNER = D_MODEL // 2
A_HEADS = 8
A_HEAD_DIM = A_INNER // A_HEADS
A_GROUPS = 2
A_STATE = 64
A_CONV = 5
A_CONV_DIM = A_INNER + 2 * A_GROUPS * A_STATE
A_COLS = A_INNER + A_CONV_DIM + 2 * A_HEADS
SSD_CHUNK = 64
B_WIDTH = D_MODEL // 4
B_GROUP = 16
B_NGROUPS = B_WIDTH // B_GROUP
B_STATE = 64
B_COLS = B_WIDTH
C_WIDTH = D_MODEL // 4
C_HEADS = 4
C_KEY = C_WIDTH // C_HEADS
C_VAL = C_WIDTH // C_HEADS
C_COLS = 5 * C_WIDTH
HG_CHUNK = 64

MIX_WIDTH = A_INNER + B_WIDTH + C_WIDTH
IN_COLS = A_COLS + B_COLS + C_COLS

kernel_name = "hybrid_ssd_s5_hgrn2_dit_block"


def rms_norm(x):
    xf = x.astype(F32)
    return (xf * lax.rsqrt(jnp.mean(xf * xf, axis=-1, keepdims=True) + EPS)).astype(x.dtype)


def modulate(h, shift, scale):
    return rms_norm(h) * (1 + scale) + shift


def swiglu(u, w_in, w_out):
    gate, up = jnp.split(u @ w_in, 2, axis=-1)
    return (jax.nn.silu(gate) * up) @ w_out


def masked_exp(diff, mask):
    return jnp.where(mask, jnp.exp(jnp.where(mask, diff, 0.0)), 0.0)


def _flip(t):
    return jnp.flip(t, axis=1)


def raster_to_column(t, rows):
    b, s, d = t.shape
    return t.reshape(b, rows, GRID_W, d).transpose(0, 2, 1, 3).reshape(b, s, d)


def column_to_raster(t, rows):
    b, s, d = t.shape
    return t.reshape(b, GRID_W, rows, d).transpose(0, 2, 1, 3).reshape(b, s, d)


def depthwise_conv(x, w, b):
    pad = A_CONV // 2
    y = lax.conv_general_dilated(x, w[:, None, :].astype(x.dtype), window_strides=(1,),
                                 padding=[(pad, pad)], dimension_numbers=('NWC', 'WIO', 'NWC'),
                                 feature_group_count=x.shape[-1])
    return y + b.astype(x.dtype)


def ssd_chunked(x, dt, a, bm, cm, s0):
    bsz, L, H, P = x.shape
    N = bm.shape[-1]
    Q = SSD_CHUNK
    nc = L // Q
    x = x.reshape(bsz, nc, Q, H, P)
    dt = dt.reshape(bsz, nc, Q, H)
    bm = bm.reshape(bsz, nc, Q, H, N)
    cm = cm.reshape(bsz, nc, Q, H, N)
    cum = jnp.cumsum(dt * a, axis=2)
    tri = jnp.tril(jnp.ones((Q, Q), dtype=bool))[None, None, :, :, None]
    decay = masked_exp(cum[:, :, :, None, :] - cum[:, :, None, :, :], tri)
    scores = jnp.einsum('bcihn,bcjhn->bcijh', cm, bm) * decay * dt[:, :, None, :, :]
    y_intra = jnp.einsum('bcijh,bcjhp->bcihp', scores, x)
    w_end = jnp.exp(cum[:, :, -1:, :] - cum) * dt
    chunk_states = jnp.einsum('bcjh,bcjhn,bcjhp->bchpn', w_end, bm, x)
    chunk_decay = jnp.exp(cum[:, :, -1, :])

    def step(s, inp):
        dec, cs = inp
        return s * dec[:, :, None, None] + cs, s

    s_fin, s_in = lax.scan(step, s0, (jnp.moveaxis(chunk_decay, 1, 0), jnp.moveaxis(chunk_states, 1, 0)))
    s_in = jnp.moveaxis(s_in, 0, 1)
    y_inter = jnp.einsum('bcihn,bchpn->bcihp', cm, s_in) * jnp.exp(cum)[..., None]
    return (y_intra + y_inter).reshape(bsz, L, H, P), s_fin


def mamba2_mixer(p, conv_w, conv_b, dt_bias, a_log, d_skip, norm_w, init):
    bsz, L, _ = p.shape
    z, xbc, dt_raw = jnp.split(p, [A_INNER, A_INNER + A_CONV_DIM], axis=-1)
    xbc = jax.nn.silu(depthwise_conv(xbc, conv_w, conv_b))
    xs, bm, cm = jnp.split(xbc, [A_INNER, A_INNER + A_GROUPS * A_STATE], axis=-1)
    rep = A_HEADS // A_GROUPS
    xs = xs.astype(F32).reshape(bsz, L, A_HEADS, A_HEAD_DIM)
    bm = jnp.repeat(bm.astype(F32).reshape(bsz, L, A_GROUPS, A_STATE), rep, axis=2)
    cm = jnp.repeat(cm.astype(F32).reshape(bsz, L, A_GROUPS, A_STATE), rep, axis=2)
    dt = jax.nn.softplus(dt_raw.astype(F32).reshape(bsz, L, 2, A_HEADS) + dt_bias.astype(F32))
    a = -jnp.exp(a_log.astype(F32))
    y_f, s_f = ssd_chunked(xs, dt[:, :, 0], a[0], bm, cm, init[0])
    y_b, s_b = ssd_chunked(_flip(xs), _flip(dt[:, :, 1]), a[1], _flip(bm), _flip(cm), init[1])
    y = y_f + _flip(y_b) + d_skip.astype(F32)[:, None] * xs
    y = y.reshape(bsz, L, A_INNER) * jax.nn.silu(z.astype(F32))
    y = rms_norm(y) * norm_w.astype(F32)
    return y.astype(p.dtype), jnp.stack([s_f, s_b])


def _complex_affine_combine(e1, e2):
    a1r, a1i, b1r, b1i = e1
    a2r, a2i, b2r, b2i = e2
    return (a1r * a2r - a1i * a2i, a1r * a2i + a1i * a2r,
            a2r * b1r - a2i * b1i + b2r, a2r * b1i + a2i * b1r + b2i)


def s5_scan(u, lam_re, lam_im, log_step, b_re, b_im, s0_re, s0_im):
    L = u.shape[1]
    lam_re = lam_re.astype(F32)
    lam_im = lam_im.astype(F32)
    step = jnp.exp(log_step.astype(F32))[:, None]
    mag = jnp.exp(lam_re * step)
    ar = mag * jnp.cos(lam_im * step)
    ai = mag * jnp.sin(lam_im * step)
    den = lam_re * lam_re + lam_im * lam_im
    nr = ar - 1.0
    kr = (nr * lam_re + ai * lam_im) / den
    ki = (ai * lam_re - nr * lam_im) / den
    b_re = b_re.astype(F32)
    b_im = b_im.astype(F32)
    br = kr[..., None] * b_re - ki[..., None] * b_im
    bi = kr[..., None] * b_im + ki[..., None] * b_re
    vr = jnp.einsum('gnc,blgc->blgn', br, u)
    vi = jnp.einsum('gnc,blgc->blgn', bi, u)
    vr = vr.at[:, 0].add(ar * s0_re - ai * s0_im)
    vi = vi.at[:, 0].add(ar * s0_im + ai * s0_re)
    shape_a = (1, L) + ar.shape
    elems = (jnp.broadcast_to(ar, shape_a), jnp.broadcast_to(ai, shape_a), vr, vi)
    _, _, xr, xi = lax.associative_scan(_complex_affine_combine, elems, axis=1)
    return xr, xi


def s5_mixer(p, lam_re, lam_im, log_step, b_re, b_im, c_re, c_im, d_skip, glu_w, glu_b, init_re, init_im):
    bsz, L, _ = p.shape
    pf = p.astype(F32)
    u = pf.reshape(bsz, L, B_NGROUPS, B_GROUP)
    c_re = c_re.astype(F32)
    c_im = c_im.astype(F32)
    xr_f, xi_f = s5_scan(u, lam_re[0], lam_im[0], log_step[0], b_re[0], b_im[0], init_re[0], init_im[0])
    xr_b, xi_b = s5_scan(_flip(u), lam_re[1], lam_im[1], log_step[1], b_re[1], b_im[1], init_re[1], init_im[1])
    y_f = jnp.einsum('gcn,blgn->blgc', c_re[0], xr_f) - jnp.einsum('gcn,blgn->blgc', c_im[0], xi_f)
    y_b = jnp.einsum('gcn,blgn->blgc', c_re[1], xr_b) - jnp.einsum('gcn,blgn->blgc', c_im[1], xi_b)
    y = (y_f + _flip(y_b)).reshape(bsz, L, B_WIDTH) + d_skip.astype(F32) * pf
    y = jax.nn.gelu(y)
    y = y * jax.nn.sigmoid(y @ glu_w.astype(F32) + glu_b.astype(F32))
    fin_re = jnp.stack([xr_f[:, -1], xr_b[:, -1]])
    fin_im = jnp.stack([xi_f[:, -1], xi_b[:, -1]])
    return y.astype(p.dtype), fin_re, fin_im


def hgrn2_chunked(q, log_f, k, v, s0):
    bsz, L, H, K = q.shape
    V = v.shape[-1]
    nc = L // HG_CHUNK

    def chunks(t):
        return jnp.moveaxis(t.reshape((bsz, nc, HG_CHUNK) + t.shape[2:]), 1, 0)

    tri = jnp.tril(jnp.ones((HG_CHUNK, HG_CHUNK), dtype=bool))[None, :, :, None, None]

    def step(s, inp):
        qc, lfc, kc, vc = inp
        cum = jnp.cumsum(lfc, axis=1)
        decay = masked_exp(cum[:, :, None] - cum[:, None, :], tri)
        scores = jnp.einsum('bihk,bjhk,bijhk->bijh', qc, kc, decay)
        o = jnp.einsum('bijh,bjhv->bihv', scores, vc) + jnp.einsum('bihk,bhkv->bihv', qc * jnp.exp(cum), s)
        w = kc * jnp.exp(cum[:, -1:] - cum)
        s_new = s * jnp.exp(cum[:, -1])[..., None] + jnp.einsum('bjhk,bjhv->bhkv', w, vc)
        return s_new, o

    s_fin, o = lax.scan(step, s0, (chunks(q), chunks(log_f), chunks(k), chunks(v)))
    return jnp.moveaxis(o, 0, 1).reshape(bsz, L, H, V), s_fin


def hgrn2_mixer(p, lower, norm_w, init):
    bsz, L, _ = p.shape
    q, f_raw, i, g = jnp.split(p, [C_WIDTH, 3 * C_WIDTH, 4 * C_WIDTH], axis=-1)
    q = jax.nn.silu(q.astype(F32)).reshape(bsz, L, C_HEADS, C_KEY)
    v = i.astype(F32).reshape(bsz, L, C_HEADS, C_VAL)
    f_raw = f_raw.astype(F32).reshape(bsz, L, 2, C_HEADS, C_KEY)
    lower = lower.astype(F32).reshape(2, C_HEADS, C_KEY)
    outs, finals = [], []
    for d in range(2):
        zf = f_raw[:, :, d]
        lb = lower[d]
        f = lb + (1.0 - lb) * jax.nn.sigmoid(zf)
        log_f = jnp.log(f)
        k = 1.0 - f
        if d == 0:
            o, s = hgrn2_chunked(q, log_f, k, v, init[0])
        else:
            o, s = hgrn2_chunked(_flip(q), _flip(log_f), _flip(k), _flip(v), init[1])
            o = _flip(o)
        outs.append(o)
        finals.append(s)
    o = rms_norm(outs[0] + outs[1]) * norm_w.astype(F32).reshape(C_HEADS, C_VAL)
    o = o.reshape(bsz, L, C_WIDTH) * jax.nn.silu(g.astype(F32))
    return o.astype(p.dtype), jnp.stack(finals)


def token_mixers(p_ctx, p_lat, conv_w, conv_b, dt_bias, a_log, a_d, a_norm_w,
                 lam_re, lam_im, log_step, b_re, b_im, c_re, c_im, s5_d, glu_w, glu_b,
                 lower, hg_norm_w):
    bsz = p_ctx.shape[0]
    cuts = [A_COLS, A_COLS + B_COLS]
    pa_c, pb_c, pc_c = jnp.split(p_ctx, cuts, axis=-1)
    pa_l, pb_l, pc_l = jnp.split(p_lat, cuts, axis=-1)
    za = jnp.zeros((2, bsz, A_HEADS, A_HEAD_DIM, A_STATE), F32)
    ya_c, sa = mamba2_mixer(pa_c, conv_w, conv_b, dt_bias, a_log, a_d, a_norm_w, za)
    ya_l, _ = mamba2_mixer(pa_l, conv_w, conv_b, dt_bias, a_log, a_d, a_norm_w, sa)
    zb = jnp.zeros((2, bsz, B_NGROUPS, B_STATE), F32)
    yb_c, sb_re, sb_im = s5_mixer(pb_c, lam_re, lam_im, log_step, b_re, b_im, c_re, c_im, s5_d, glu_w, glu_b, zb, zb)
    yb_l, _, _ = s5_mixer(pb_l, lam_re, lam_im, log_step, b_re, b_im, c_re, c_im, s5_d, glu_w, glu_b, sb_re, sb_im)
    zc = jnp.zeros((2, bsz, C_HEADS, C_KEY, C_VAL), F32)
    yc_c, sc = hgrn2_mixer(pc_c, lower, hg_norm_w, zc)
    yc_l, _ = hgrn2_mixer(pc_l, lower, hg_norm_w, sc)
    return (jnp.concatenate([ya_c, yb_c, yc_c], axis=-1), jnp.concatenate([ya_l, yb_l, yc_l], axis=-1))


def setup_inputs(seed: int = 0) -> dict:
    key = jax.random.key(seed)
    ks = jax.random.split(key, 32)
    dm = D_MODEL

    def nrm(k, shape, scale):
        return scale * jax.random.normal(k, shape, F32)

    x = nrm(ks[0], (BATCH, SEQ, dm), 1.0)
    c = nrm(ks[1], (BATCH, dm), 1.0)
    ctx = nrm(ks[2], (BATCH, CTX_LEN, dm), 1.0)
    c_ctx = nrm(ks[3], (dm,), 1.0)
    mod_w = nrm(ks[4], (DEPTH, dm, N_MOD * dm), dm ** -0.5)
    mod_b = nrm(ks[5], (DEPTH, N_MOD * dm), 0.01)
    ffn_w_in = nrm(ks[6], (DEPTH, 2, dm, 2 * D_FF), dm ** -0.5)
    ffn_w_out = nrm(ks[7], (DEPTH, 2, D_FF, dm), D_FF ** -0.5)
    w_in = nrm(ks[8], (DEPTH, dm, IN_COLS), dm ** -0.5)
    w_out = nrm(ks[9], (DEPTH, MIX_WIDTH, dm), MIX_WIDTH ** -0.5)
    a_conv_w = nrm(ks[10], (DEPTH, A_CONV, A_CONV_DIM), A_CONV ** -0.5)
    a_conv_b = nrm(ks[11], (DEPTH, A_CONV_DIM), 0.01)
    dt0 = jnp.exp(jax.random.uniform(ks[12], (DEPTH, 2, A_HEADS), F32, math.log(1e-3), math.log(1e-1)))
    a_dt_bias = dt0 + jnp.log(-jnp.expm1(-dt0))
    a_log = jnp.log(jax.random.uniform(ks[13], (DEPTH, 2, A_HEADS), F32, 1.0, 16.0))
    a_d = 1.0 + nrm(ks[14], (DEPTH, A_HEADS), 0.01)
    a_norm_w = 1.0 + nrm(ks[15], (DEPTH, A_INNER), 0.01)
    s5_lam_re = -0.5 + nrm(ks[16], (DEPTH, 2, B_NGROUPS, B_STATE), 0.01)
    s5_lam_im = math.pi * jnp.arange(B_STATE, dtype=F32) + nrm(ks[17], (DEPTH, 2, B_NGROUPS, B_STATE), 0.01)
    s5_log_step = jax.random.uniform(ks[18], (DEPTH, 2, B_NGROUPS), F32, math.log(1e-3), math.log(1e-1))
    s5_b_re = nrm(ks[19], (DEPTH, 2, B_NGROUPS, B_STATE, B_GROUP), (2 * B_GROUP) ** -0.5)
    s5_b_im = nrm(ks[20], (DEPTH, 2, B_NGROUPS, B_STATE, B_GROUP), (2 * B_GROUP) ** -0.5)
    s5_c_re = nrm(ks[21], (DEPTH, 2, B_NGROUPS, B_GROUP, B_STATE), (2 * B_STATE) ** -0.5)
    s5_c_im = nrm(ks[22], (DEPTH, 2, B_NGROUPS, B_GROUP, B_STATE), (2 * B_STATE) ** -0.5)
    s5_d = nrm(ks[23], (DEPTH, B_WIDTH), 1.0)
    s5_glu_w = nrm(ks[24], (DEPTH, B_WIDTH, B_WIDTH), B_WIDTH ** -0.5)
    s5_glu_b = nrm(ks[25], (DEPTH, B_WIDTH), 0.01)
    hg_lb_logits = nrm(ks[26], (DEPTH, 2, C_WIDTH), 0.5)
    hg_norm_w = 1.0 + nrm(ks[27], (DEPTH, C_WIDTH), 0.01)
    final_norm_w = 1.0 + nrm(ks[28], (dm,), 0.01)
    return {"x": x, "c": c, "ctx": ctx, "c_ctx": c_ctx, "mod_w": mod_w, "mod_b": mod_b,
            "ffn_w_in": ffn_w_in, "ffn_w_out": ffn_w_out, "w_in": w_in, "w_out": w_out,
            "a_conv_w": a_conv_w, "a_conv_b": a_conv_b, "a_dt_bias": a_dt_bias, "a_log": a_log,
            "a_d": a_d, "a_norm_w": a_norm_w, "s5_lam_re": s5_lam_re, "s5_lam_im": s5_lam_im,
            "s5_log_step": s5_log_step, "s5_b_re": s5_b_re, "s5_b_im": s5_b_im, "s5_c_re": s5_c_re,
            "s5_c_im": s5_c_im, "s5_d": s5_d, "s5_glu_w": s5_glu_w, "s5_glu_b": s5_glu_b,
            "hg_lb_logits": hg_lb_logits, "hg_norm_w": hg_norm_w, "final_norm_w": final_norm_w}


def reference(x, c, ctx, c_ctx, mod_w, mod_b, ffn_w_in, ffn_w_out, w_in, w_out,
              a_conv_w, a_conv_b, a_dt_bias, a_log, a_d, a_norm_w,
              s5_lam_re, s5_lam_im, s5_log_step, s5_b_re, s5_b_im, s5_c_re, s5_c_im,
              s5_d, s5_glu_w, s5_glu_b, hg_lb_logits, hg_norm_w, final_norm_w):
    bsz, seq, dm = x.shape
    rows = seq // GRID_W
    p_lb = jax.nn.softmax(hg_lb_logits.astype(F32), axis=0)
    lower_bounds = jnp.cumsum(p_lb, axis=0) - p_lb[:1]
    h_lat, h_ctx = x, ctx
    for l in range(DEPTH):
        last = l == DEPTH - 1
        col_major = l % 2 == 1
        m_lat = (jax.nn.silu(c) @ mod_w[l] + mod_b[l]).reshape(bsz, N_MOD, 1, dm)
        m_ctx = (jax.nn.silu(c_ctx) @ mod_w[l] + mod_b[l]).reshape(N_MOD, dm)
        h_lat = h_lat + 0.5 * m_lat[:, 2] * swiglu(modulate(h_lat, m_lat[:, 0], m_lat[:, 1]), ffn_w_in[l, 0], ffn_w_out[l, 0])
        h_ctx = h_ctx + 0.5 * m_ctx[2] * swiglu(modulate(h_ctx, m_ctx[0], m_ctx[1]), ffn_w_in[l, 0], ffn_w_out[l, 0])
        u_lat = modulate(h_lat, m_lat[:, 3], m_lat[:, 4])
        u_ctx = modulate(h_ctx, m_ctx[3], m_ctx[4])
        if col_major:
            u_lat = raster_to_column(u_lat, rows)
        mix_ctx, mix_lat = token_mixers(u_ctx @ w_in[l], u_lat @ w_in[l],
                                        a_conv_w[l], a_conv_b[l], a_dt_bias[l], a_log[l], a_d[l], a_norm_w[l],
                                        s5_lam_re[l], s5_lam_im[l], s5_log_step[l], s5_b_re[l], s5_b_im[l],
                                        s5_c_re[l], s5_c_im[l], s5_d[l], s5_glu_w[l], s5_glu_b[l],
                                        lower_bounds[l], hg_norm_w[l])
        y_lat = mix_lat @ w_out[l]
        if col_major:
            y_lat = column_to_raster(y_lat, rows)
        h_lat = h_lat + m_lat[:, 5] * y_lat
        h_lat = h_lat + 0.5 * m_lat[:, 8] * swiglu(modulate(h_lat, m_lat[:, 6], m_lat[:, 7]), ffn_w_in[l, 1], ffn_w_out[l, 1])
        if not last:
            h_ctx = h_ctx + m_ctx[5] * (mix_ctx @ w_out[l])
            h_ctx = h_ctx + 0.5 * m_ctx[8] * swiglu(modulate(h_ctx, m_ctx[6], m_ctx[7]), ffn_w_in[l, 1], ffn_w_out[l, 1])
    return rms_norm(h_lat) * final_norm_w
```

```python
import functools
import math

import jax
import jax.numpy as jnp
from jax import lax
from jax.experimental import pallas as pl
from jax.experimental.pallas import tpu as pltpu

F32 = jnp.float32
BF16 = jnp.bfloat16
EPS = 1e-6

V7X_VMEM_BYTES = 64 * 1024 * 1024
V7X_LANES = 128
V7X_SUBLANES = 8

GRID_W = 64
N_MOD = 9
A_HEADS = 8
A_HEAD_DIM = 64
A_GROUPS = 2
A_STATE = 64
A_CONV = 5
B_GROUP = 16
B_STATE = 64
C_HEADS = 4
C_KEY = 64

TOKEN_TILE = 256
MIX_CHUNK = 256
S5_CHUNK = 16
HG_DIAG = 16


def _vmem_limit(estimate_bytes):
    return int(min(V7X_VMEM_BYTES - 6 * 1024 * 1024, max(estimate_bytes, 16 * 1024 * 1024)))


def _silu(x):
    return x * jax.nn.sigmoid(x)


def _rms(x):
    return x * lax.rsqrt(jnp.mean(x * x, axis=-1, keepdims=True) + EPS)


def _split3(x):
    hi = x.astype(BF16)
    r = x - hi.astype(F32)
    mid = r.astype(BF16)
    lo = (r - mid.astype(F32)).astype(BF16)
    return hi, mid, lo


def _dot01_left(m01, x):
    hi, mid, lo = _split3(x)
    d = lambda a: jnp.dot(m01, a, preferred_element_type=F32)
    return (d(lo) + d(mid)) + d(hi)


def _dot01_right(x, m01):
    hi, mid, lo = _split3(x)
    d = lambda a: jnp.dot(a, m01, preferred_element_type=F32)
    return (d(lo) + d(mid)) + d(hi)


def _dot_nt(a, b):
    return lax.dot_general(a, b, (((1,), (1,)), ((), ())), preferred_element_type=F32)


def _mod_kernel(c_ref, w_ref, b_ref, o_ref):
    s = _silu(c_ref[...]).astype(BF16)
    o_ref[0] = jnp.dot(s, w_ref[0].astype(BF16), preferred_element_type=F32) + b_ref[0]


def _mod_vectors(cvec, mod_w, mod_b):
    depth, dm, nm = mod_w.shape
    rows = cvec.shape[0]
    tn = 1024
    return pl.pallas_call(
        _mod_kernel,
        grid=(depth, nm // tn),
        in_specs=[pl.BlockSpec((rows, dm), lambda l, j: (0, 0)),
                  pl.BlockSpec((1, dm, tn), lambda l, j: (l, 0, j)),
                  pl.BlockSpec((1, 1, tn), lambda l, j: (l, 0, j))],
        out_specs=pl.BlockSpec((1, rows, tn), lambda l, j: (l, 0, j)),
        out_shape=jax.ShapeDtypeStruct((depth, rows, nm), F32),
        compiler_params=pltpu.CompilerParams(dimension_semantics=("arbitrary", "arbitrary")),
        name="mod_vectors",
    )(cvec, mod_w, mod_b.reshape(depth, 1, nm))


def _ffn_kernel(h_ref, mod_ref, win_ref, wout_ref, fw_ref, o_ref, *, base, fc, d_ff, final):
    x = h_ref[0]
    m = mod_ref[0, 0]
    shift, scale, gate = m[base:base + 1], m[base + 1:base + 2], m[base + 2:base + 3]
    u = (_rms(x) * (1.0 + scale) + shift).astype(BF16)
    acc = jnp.zeros(x.shape, F32)
    for j in range(d_ff // fc):
        g = jnp.dot(u, win_ref[:, j * fc:(j + 1) * fc], preferred_element_type=F32)
        up = jnp.dot(u, win_ref[:, d_ff + j * fc:d_ff + (j + 1) * fc], preferred_element_type=F32)
        a = (_silu(g) * up).astype(BF16)
        acc = acc + jnp.dot(a, wout_ref[j * fc:(j + 1) * fc, :], preferred_element_type=F32)
    y = x + 0.5 * gate * acc
    if final:
        y = _rms(y) * fw_ref[...]
    o_ref[0] = y


def _ffn(h, mods, w_in, w_out, final_w, *, base, n_ctx, tile_start, final):
    bsz, lt, dm = h.shape
    d_ff = w_out.shape[0]
    tm = TOKEN_TILE
    nct = n_ctx // tm
    n_tiles = lt // tm - tile_start
    fc = d_ff // 2 if (d_ff // 2) % V7X_LANES == 0 else d_ff
    kern = functools.partial(_ffn_kernel, base=base, fc=fc, d_ff=d_ff, final=final)
    w_bytes = (w_in.size + w_out.size) * 2
    est = w_bytes + 4 * tm * dm * 4 + 8 * tm * fc * 4 + 8 * 1024 * 1024
    return pl.pallas_call(
        kern,
        grid=(bsz, n_tiles),
        in_specs=[pl.BlockSpec((1, tm, dm), lambda b, t: (b, t + tile_start, 0)),
                  pl.BlockSpec((1, 1, N_MOD, dm), lambda b, t: (b, jnp.where(t + tile_start >= nct, 1, 0), 0, 0)),
                  pl.BlockSpec(w_in.shape, lambda b, t: (0, 0), pipeline_mode=pl.Buffered(1)),
                  pl.BlockSpec(w_out.shape, lambda b, t: (0, 0), pipeline_mode=pl.Buffered(1)),
                  pl.BlockSpec((1, dm), lambda b, t: (0, 0))],
        out_specs=pl.BlockSpec((1, tm, dm), lambda b, t: (b, t, 0)),
        out_shape=jax.ShapeDtypeStruct((bsz, n_tiles * tm, dm), F32),
        compiler_params=pltpu.CompilerParams(dimension_semantics=("arbitrary", "arbitrary"),
                                             vmem_limit_bytes=_vmem_limit(est)),
        name="ffn",
    )(h, mods, w_in, w_out, final_w)


IN_SPLIT = (512, 768, 128, 256, 1280)


def _inproj_kernel(h_ref, mod_ref, w_ref, z_ref, xbc_ref, dt_ref, pb_ref, pc_ref):
    x = h_ref[0]
    m = mod_ref[0, 0]
    u = (_rms(x) * (1.0 + m[4:5]) + m[3:4]).astype(BF16)
    p = jnp.dot(u, w_ref[...], preferred_element_type=F32)
    off = 0
    for ref, wd in zip((z_ref, xbc_ref, dt_ref, pb_ref, pc_ref), IN_SPLIT):
        ref[0] = p[:, off:off + wd]
        off += wd


def _inproj(h, mods, w, *, n_ctx):
    bsz, lt, dm = h.shape
    tm = TOKEN_TILE
    nct = n_ctx // tm
    tok = lambda wd: pl.BlockSpec((1, tm, wd), lambda b, t: (b, t, 0))
    return pl.pallas_call(
        _inproj_kernel,
        grid=(bsz, lt // tm),
        in_specs=[tok(dm),
                  pl.BlockSpec((1, 1, N_MOD, dm), lambda b, t: (b, jnp.where(t >= nct, 1, 0), 0, 0)),
                  pl.BlockSpec(w.shape, lambda b, t: (0, 0), pipeline_mode=pl.Buffered(1))],
        out_specs=[tok(wd) for wd in IN_SPLIT],
        out_shape=[jax.ShapeDtypeStruct((bsz, lt, wd), F32) for wd in IN_SPLIT],
        compiler_params=pltpu.CompilerParams(dimension_semantics=("arbitrary", "arbitrary"),
                                             vmem_limit_bytes=_vmem_limit(40 * 1024 * 1024)),
        name="inproj",
    )(h, mods, w)


def _bwd_chunk(i, n_ctx_chunks, n_chunks):
    return jnp.where(i < n_ctx_chunks, n_ctx_chunks - 1 - i, n_chunks - 1 - (i - n_ctx_chunks))


def _ssd_kernel(z_ref, xbc_ref, dt_ref, cw_ref, cb_ref, dtb_ref, aneg_ref, dsk_ref, nw_ref,
                o_ref, xc_ref, st_ref, *, n_ctx, q):
    lt = o_ref.shape[1]
    n_chunks = lt // q
    ncc = n_ctx // q
    inner = A_HEADS * A_HEAD_DIM
    gw = A_STATE
    hpg = A_HEADS // A_GROUPS
    gl = hpg * A_HEAD_DIM

    cw = cw_ref[...]
    cb = cb_ref[...]

    def conv_body(i, carry):
        t0 = pl.multiple_of(i * q, q)
        first = jnp.logical_or(i == 0, i == ncc)
        last = jnp.logical_or(i == ncc - 1, i == n_chunks - 1)
        cur = xbc_ref[0, pl.ds(t0, q), :]
        p0 = pl.multiple_of(jnp.maximum(t0 - V7X_SUBLANES, 0), V7X_SUBLANES)
        n0 = pl.multiple_of(jnp.minimum(t0 + q, lt - V7X_SUBLANES), V7X_SUBLANES)
        prev = jnp.where(first, 0.0, xbc_ref[0, pl.ds(p0, V7X_SUBLANES), :])
        nxt = jnp.where(last, 0.0, xbc_ref[0, pl.ds(n0, V7X_SUBLANES), :])
        ext = jnp.concatenate([prev, cur, nxt], axis=0)
        acc = cb
        for k in range(A_CONV):
            s0 = V7X_SUBLANES - A_CONV // 2 + k
            acc = acc + cw[k:k + 1] * ext[s0:s0 + q]
        xc_ref[pl.ds(t0, q), :] = _silu(acc)
        return carry

    lax.fori_loop(0, n_chunks, conv_body, 0)

    o_ref[...] = jnp.zeros(o_ref.shape, F32)
    st_ref[...] = jnp.zeros(st_ref.shape, F32)

    rowi = lax.broadcasted_iota(jnp.int32, (q, q), 0)
    coli = lax.broadcasted_iota(jnp.int32, (q, q), 1)
    dtb = dtb_ref[...]
    aneg = aneg_ref[...]

    def chunk(d, c_idx):
        t0 = pl.multiple_of(c_idx * q, q)
        tri = (rowi >= coli) if d == 0 else (rowi <= coli)
        tri_bf = jnp.where(tri, 1.0, 0.0).astype(BF16)
        last = q - 1 if d == 0 else 0
        xc = xc_ref[pl.ds(t0, q), :]
        xs = xc[:, :inner]
        bm = xc[:, inner:inner + A_GROUPS * gw]
        cm = xc[:, inner + A_GROUPS * gw:]
        dt = jax.nn.softplus(dt_ref[0, pl.ds(t0, q), :] + dtb)
        d_a = dt * aneg
        cum = _dot01_left(tri_bf, d_a)
        er = lax.broadcasted_iota(jnp.int32, (V7X_LANES, inner), 0)
        ec = lax.broadcasted_iota(jnp.int32, (V7X_LANES, inner), 1)
        expand = jnp.where(er == d * A_HEADS + (ec >> int(math.log2(A_HEAD_DIM))), 1.0, 0.0).astype(BF16)
        cumx = _dot01_right(cum, expand)
        dtx = _dot01_right(dt, expand)
        cum_t = cum.T
        cum_last = cumx[last:last + 1, :]
        xdt = (xs * dtx).astype(BF16)
        xw = (xs * (jnp.exp(cum_last - cumx) * dtx)).astype(BF16)
        ecum = jnp.exp(cumx)
        e_last = jnp.exp(cum_last)
        bm_t = bm.T
        ys = []
        for g in range(A_GROUPS):
            cg = cm[:, g * gw:(g + 1) * gw].astype(BF16)
            bg = bm[:, g * gw:(g + 1) * gw].astype(BF16)
            gmat = _dot_nt(cg, bg)
            st = st_ref[d, g]
            y_inter = jnp.dot(cg, st.astype(BF16), preferred_element_type=F32) * ecum[:, g * gl:(g + 1) * gl]
            for hh in range(hpg):
                h = g * hpg + hh
                lane = d * A_HEADS + h
                diff = cum[:, lane:lane + 1] - cum_t[lane:lane + 1, :]
                decay = jnp.where(tri, jnp.exp(jnp.where(tri, diff, 0.0)), 0.0)
                sc = (gmat * decay).astype(BF16)
                ys.append(jnp.dot(sc, xdt[:, h * A_HEAD_DIM:(h + 1) * A_HEAD_DIM], preferred_element_type=F32)
                          + y_inter[:, hh * A_HEAD_DIM:(hh + 1) * A_HEAD_DIM])
            upd = jnp.dot(bm_t[g * gw:(g + 1) * gw].astype(BF16), xw[:, g * gl:(g + 1) * gl],
                          preferred_element_type=F32)
            st_ref[d, g] = st * e_last[:, g * gl:(g + 1) * gl] + upd
        y = jnp.concatenate(ys, axis=1)
        o_ref[0, pl.ds(t0, q), :] = o_ref[0, pl.ds(t0, q), :] + y

    def scan_body(i, carry):
        chunk(0, i)
        chunk(1, _bwd_chunk(i, ncc, n_chunks))
        return carry

    lax.fori_loop(0, n_chunks, scan_body, 0)

    dsk = dsk_ref[...]
    nw = nw_ref[...]

    def fin_body(i, carry):
        t0 = pl.multiple_of(i * q, q)
        xs = xc_ref[pl.ds(t0, q), :inner]
        y = o_ref[0, pl.ds(t0, q), :] + dsk * xs
        y = y * _silu(z_ref[0, pl.ds(t0, q), :])
        o_ref[0, pl.ds(t0, q), :] = _rms(y) * nw
        return carry

    lax.fori_loop(0, n_chunks, fin_body, 0)


def _ssd(z, xbc, dtp, cw, cb, dtb, aneg, dsk, nw, *, n_ctx):
    bsz, lt, inner = z.shape
    q = MIX_CHUNK
    cd = xbc.shape[-1]
    seq = lambda wd: pl.BlockSpec((1, lt, wd), lambda b: (b, 0, 0))
    full = lambda a: pl.BlockSpec(a.shape, lambda b: (0,) * a.ndim)
    est = 2 * 4 * lt * (inner + cd + V7X_LANES + inner) + 4 * lt * cd + 12 * 1024 * 1024
    return pl.pallas_call(
        functools.partial(_ssd_kernel, n_ctx=n_ctx, q=q),
        grid=(bsz,),
        in_specs=[seq(inner), seq(cd), seq(V7X_LANES), full(cw), full(cb), full(dtb), full(aneg), full(dsk), full(nw)],
        out_specs=seq(inner),
        out_shape=jax.ShapeDtypeStruct((bsz, lt, inner), F32),
        scratch_shapes=[pltpu.VMEM((lt, cd), F32),
                        pltpu.VMEM((2, A_GROUPS, A_STATE, inner // A_GROUPS), F32)],
        compiler_params=pltpu.CompilerParams(dimension_semantics=("arbitrary",),
                                             vmem_limit_bytes=_vmem_limit(est)),
        name="ssd_mixer",
    )(z, xbc, dtp, cw, cb, dtb, aneg, dsk, nw)


def _hgrn_kernel(q_ref, ff_ref, fb_ref, v_ref, g_ref, lb_ref, nw_ref, o_ref, st_ref, *, n_ctx, c):
    lt = o_ref.shape[1]
    n_chunks = lt // c
    ncc = n_ctx // c
    w = C_HEADS * C_KEY
    o_ref[...] = jnp.zeros(o_ref.shape, F32)
    st_ref[...] = jnp.zeros(st_ref.shape, F32)

    rowi = lax.broadcasted_iota(jnp.int32, (c, c), 0)
    coli = lax.broadcasted_iota(jnp.int32, (c, c), 1)
    hr = lax.broadcasted_iota(jnp.int32, (w, w), 0)
    hc = lax.broadcasted_iota(jnp.int32, (w, w), 1)
    same_head = (hr >> int(math.log2(C_KEY))) == (hc >> int(math.log2(C_KEY)))
    bones = jnp.where(same_head, 1.0, 0.0).astype(BF16)
    trow = lax.broadcasted_iota(jnp.int32, (c, w), 0)
    levels = []
    s = c // 2
    while s >= HG_DIAG:
        levels.append(s)
        s //= 2

    def bcast_rows(x, first, step, reps):
        n = x.shape[0] // step
        return jnp.concatenate([jnp.broadcast_to(x[first + b * step:first + b * step + 1, :], (reps, x.shape[1]))
                                for b in range(n)], axis=0)

    def chunk(d, c_idx):
        t0 = pl.multiple_of(c_idx * c, c)
        tri = (rowi >= coli) if d == 0 else (rowi <= coli)
        tri_bf = jnp.where(tri, 1.0, 0.0).astype(BF16)
        last = c - 1 if d == 0 else 0
        qs = _silu(q_ref[0, pl.ds(t0, c), :])
        zf = (ff_ref if d == 0 else fb_ref)[0, pl.ds(t0, c), :]
        lb = lb_ref[d:d + 1, :]
        f = lb + (1.0 - lb) * jax.nn.sigmoid(zf)
        logf = jnp.log(f)
        kk = 1.0 - f
        v = v_ref[0, pl.ds(t0, c), :]
        cum = _dot01_left(tri_bf, logf)
        cum_last = cum[last:last + 1, :]

        st = st_ref[d]
        qe = (qs * jnp.exp(cum)).astype(BF16)
        out = _dot_nt(qe, st.astype(BF16))
        kw = (kk * jnp.exp(cum_last - cum)).astype(BF16)
        upd = jnp.dot(v.T.astype(BF16), kw, preferred_element_type=F32)
        st_ref[d] = jnp.where(same_head, st * jnp.exp(cum_last) + upd, 0.0)

        scores = [jnp.zeros((c, c), F32) for _ in range(C_HEADS)]
        for s in levels:
            grp = 2 * s
            pos = trow & (grp - 1)
            is_query = (pos >= s) if d == 0 else (pos < s)
            refb = bcast_rows(cum, (s - 1) if d == 0 else s, grp, grp)
            e = jnp.where(is_query, cum - refb, refb - cum)
            fac = jnp.exp(e)
            qt = jnp.where(is_query, qs * fac, 0.0).astype(BF16)
            kt = jnp.where(is_query, 0.0, kk * fac).astype(BF16)
            sh = int(math.log2(grp))
            same_grp = (rowi >> sh) == (coli >> sh)
            for h in range(C_HEADS):
                sc = _dot_nt(qt[:, h * C_KEY:(h + 1) * C_KEY], kt[:, h * C_KEY:(h + 1) * C_KEY])
                scores[h] = scores[h] + jnp.where(same_grp, sc, 0.0)
        vb16 = v.astype(BF16)
        out = out + jnp.concatenate(
            [jnp.dot(scores[h].astype(BF16), vb16[:, h * C_KEY:(h + 1) * C_KEY], preferred_element_type=F32)
             for h in range(C_HEADS)], axis=1)

        pos = trow & (HG_DIAG - 1)
        for jo in range(HG_DIAG):
            kb = bcast_rows(kk, jo, HG_DIAG, HG_DIAG)
            cb = bcast_rows(cum, jo, HG_DIAG, HG_DIAG)
            vb = bcast_rows(v, jo, HG_DIAG, HG_DIAG)
            valid = (pos >= jo) if d == 0 else (pos <= jo)
            t = jnp.where(valid, qs * kb * jnp.exp(jnp.where(valid, cum - cb, 0.0)), 0.0)
            r = jnp.dot(t.astype(BF16), bones, preferred_element_type=F32)
            out = out + r * vb
        o_ref[0, pl.ds(t0, c), :] = o_ref[0, pl.ds(t0, c), :] + out

    def scan_body(i, carry):
        chunk(0, i)
        chunk(1, _bwd_chunk(i, ncc, n_chunks))
        return carry

    lax.fori_loop(0, n_chunks, scan_body, 0)

    nw = nw_ref[...]

    def fin_body(i, carry):
        t0 = pl.multiple_of(i * c, c)
        o = o_ref[0, pl.ds(t0, c), :]
        ms = _dot01_right(o * o, bones) * (1.0 / C_KEY)
        y = o * lax.rsqrt(ms + EPS) * nw
        o_ref[0, pl.ds(t0, c), :] = y * _silu(g_ref[0, pl.ds(t0, c), :])
        return carry

    lax.fori_loop(0, n_chunks, fin_body, 0)


def _hgrn(pc, lower, nw, *, n_ctx):
    bsz, lt, _ = pc.shape
    w = C_HEADS * C_KEY
    c = MIX_CHUNK
    full = lambda a: pl.BlockSpec(a.shape, lambda b: (0,) * a.ndim)
    est = 2 * 4 * lt * (5 * w + w) + 16 * 1024 * 1024
    return pl.pallas_call(
        functools.partial(_hgrn_kernel, n_ctx=n_ctx, c=c),
        grid=(bsz,),
        in_specs=[pl.BlockSpec((1, lt, w), lambda b: (b, 0, 0)),
                  pl.BlockSpec((1, lt, w), lambda b: (b, 0, 1)),
                  pl.BlockSpec((1, lt, w), lambda b: (b, 0, 2)),
                  pl.BlockSpec((1, lt, w), lambda b: (b, 0, 3)),
                  pl.BlockSpec((1, lt, w), lambda b: (b, 0, 4)),
                  full(lower), full(nw)],
        out_specs=pl.BlockSpec((1, lt, w), lambda b: (b, 0, 0)),
        out_shape=jax.ShapeDtypeStruct((bsz, lt, w), F32),
        scratch_shapes=[pltpu.VMEM((2, w, w), F32)],
        compiler_params=pltpu.CompilerParams(dimension_semantics=("arbitrary",),
                                             vmem_limit_bytes=_vmem_limit(est)),
        name="hgrn_mixer",
    )(pc, pc, pc, pc, pc, lower, nw)


def _s5_tables(lam_re, lam_im, log_step, b_re, b_im, c_re, c_im):
    q = S5_CHUNK
    hp = lax.Precision.HIGHEST
    lam_re = lam_re.astype(F32)
    lam_im = lam_im.astype(F32)
    step = jnp.exp(log_step.astype(F32))[..., None]
    tau = jnp.arange(q + 1, dtype=F32)[:, None, None, None]
    mag = jnp.exp(lam_re * step * tau)
    pr = mag * jnp.cos(lam_im * step * tau)
    pi = mag * jnp.sin(lam_im * step * tau)
    ar, ai = pr[1], pi[1]
    den = lam_re * lam_re + lam_im * lam_im
    nr = ar - 1.0
    kr = (nr * lam_re + ai * lam_im) / den
    ki = (ai * lam_re - nr * lam_im) / den
    b_re = b_re.astype(F32)
    b_im = b_im.astype(F32)
    br = kr[..., None] * b_re - ki[..., None] * b_im
    bi = kr[..., None] * b_im + ki[..., None] * b_re
    c_re = c_re.astype(F32)
    c_im = c_im.astype(F32)
    qr = pr[..., None] * br - pi[..., None] * bi
    qi = pr[..., None] * bi + pi[..., None] * br
    hk = jnp.einsum('dgon,tdgni->tdgoi', c_re, qr, precision=hp) - jnp.einsum('dgon,tdgni->tdgoi', c_im, qi, precision=hp)
    s_idx = jnp.arange(q)[:, None]
    t_idx = jnp.arange(q)[None, :]
    lag = jnp.stack([t_idx - s_idx, s_idx - t_idx])
    ok = lag >= 0
    lagc = jnp.clip(lag, 0, q)
    toe = jnp.stack([hk[lagc[d], d] for d in range(2)])
    toe = jnp.where(ok[:, :, :, None, None, None], toe, 0.0)
    toe = toe.transpose(0, 3, 1, 5, 2, 4)
    g = toe.shape[1]
    cg = toe.shape[3]
    toe = toe.reshape(2, g, q * cg, q * cg)
    tau_st = jnp.stack([q - 1 - jnp.arange(q), jnp.arange(q)])
    wst_r = jnp.stack([qr[tau_st[d], d] for d in range(2)])
    wst_i = jnp.stack([qi[tau_st[d], d] for d in range(2)])
    wst = jnp.concatenate([wst_r, wst_i], axis=3)
    wst = wst.transpose(0, 2, 1, 4, 3).reshape(2, g, q * cg, -1)
    tau_o = jnp.stack([jnp.arange(q) + 1, q - jnp.arange(q)])
    po_r = jnp.stack([pr[tau_o[d], d] for d in range(2)])
    po_i = jnp.stack([pi[tau_o[d], d] for d in range(2)])
    w_xr = c_re[:, None] * po_r[:, :, :, None, :] - c_im[:, None] * po_i[:, :, :, None, :]
    w_xi = -c_re[:, None] * po_i[:, :, :, None, :] - c_im[:, None] * po_r[:, :, :, None, :]
    wout = jnp.concatenate([w_xr, w_xi], axis=-1)
    wout = wout.transpose(0, 2, 4, 1, 3).reshape(2, g, -1, q * cg)
    dec = jnp.stack([jnp.concatenate([pr[q], pr[q]], axis=-1),
                     jnp.concatenate([-pi[q], pi[q]], axis=-1)], axis=2)
    return toe.astype(BF16), wst.astype(BF16), wout.astype(BF16), dec


def _s5_kernel(u_ref, toe_ref, wst_ref, wout_ref, dec_ref, o_ref, z_ref, xin_ref, *, bsz, n_ctx_chunks):
    rows = u_ref.shape[1]
    n_chunks = rows // bsz
    ns = z_ref.shape[1]
    half = ns // 2
    u = u_ref[0].astype(BF16)
    y = None
    for d in range(2):
        yi = jnp.dot(u, toe_ref[d, 0], preferred_element_type=F32)
        z_ref[...] = jnp.dot(u, wst_ref[d, 0], preferred_element_type=F32)
        dr = dec_ref[d, 0, 0:1, :]
        di = dec_ref[d, 0, 1:2, :]

        def body(i, x, d=d, dr=dr, di=di):
            c_idx = i if d == 0 else _bwd_chunk(i, n_ctx_chunks, n_chunks)
            r0 = pl.multiple_of(c_idx * bsz, bsz)
            xin_ref[pl.ds(r0, bsz), :] = x
            swapped = jnp.concatenate([x[:, half:], x[:, :half]], axis=1)
            return dr * x + di * swapped + z_ref[pl.ds(r0, bsz), :]

        lax.fori_loop(0, n_chunks, body, jnp.zeros((bsz, ns), F32))
        yo = jnp.dot(xin_ref[...].astype(BF16), wout_ref[d, 0], preferred_element_type=F32)
        y = yi + yo if y is None else y + (yi + yo)
    o_ref[0] = y


def _s5_core(u2, toe, wst, wout, dec, *, bsz, n_ctx_chunks):
    g, rows, wd = u2.shape
    ns = wst.shape[-1]
    est = 2 * 2 * 4 * rows * wd + 2 * 4 * rows * ns + 6 * 4 * rows * wd + 8 * 1024 * 1024
    return pl.pallas_call(
        functools.partial(_s5_kernel, bsz=bsz, n_ctx_chunks=n_ctx_chunks),
        grid=(g,),
        in_specs=[pl.BlockSpec((1, rows, wd), lambda i: (i, 0, 0)),
                  pl.BlockSpec((2, 1, wd, wd), lambda i: (0, i, 0, 0)),
                  pl.BlockSpec((2, 1, wd, ns), lambda i: (0, i, 0, 0)),
                  pl.BlockSpec((2, 1, ns, wd), lambda i: (0, i, 0, 0)),
                  pl.BlockSpec((2, 1, 2, ns), lambda i: (0, i, 0, 0))],
        out_specs=pl.BlockSpec((1, rows, wd), lambda i: (i, 0, 0)),
        out_shape=jax.ShapeDtypeStruct((g, rows, wd), F32),
        scratch_shapes=[pltpu.VMEM((rows, ns), F32), pltpu.VMEM((rows, ns), F32)],
        compiler_params=pltpu.CompilerParams(dimension_semantics=("arbitrary",),
                                             vmem_limit_bytes=_vmem_limit(est)),
        name="s5_core",
    )(u2, toe, wst, wout, dec)


def _gelu_tanh(x):
    return 0.5 * x * (1.0 + jnp.tanh(math.sqrt(2.0 / math.pi) * (x + 0.044715 * (x * x * x))))


def _outproj_kernel(h_ref, mod_ref, ya_ref, ybr_ref, pb_ref, yc_ref, sd_ref, gw_ref, gb_ref, w_ref, o_ref):
    m = mod_ref[0, 0]
    yb = _gelu_tanh(ybr_ref[0] + sd_ref[...] * pb_ref[0])
    gate = jnp.dot(yb.astype(BF16), gw_ref[...], preferred_element_type=F32) + gb_ref[...]
    yb = yb * jax.nn.sigmoid(gate)
    mix = jnp.concatenate([ya_ref[0], yb, yc_ref[0]], axis=1).astype(BF16)
    y = jnp.dot(mix, w_ref[...], preferred_element_type=F32)
    o_ref[0] = h_ref[0] + m[5:6] * y


def _outproj(h, mods, ya, yb_raw, pb, yc, s5_d, glu_w, glu_b, w_out, *, n_ctx, tile_start):
    bsz, lt, dm = h.shape
    tm = TOKEN_TILE
    nct = n_ctx // tm
    n_tiles = lt // tm - tile_start
    tok = lambda wd: pl.BlockSpec((1, tm, wd), lambda b, t: (b, t + tile_start, 0))
    full = lambda a: pl.BlockSpec(a.shape, lambda b, t: (0,) * a.ndim)
    return pl.pallas_call(
        _outproj_kernel,
        grid=(bsz, n_tiles),
        in_specs=[tok(dm),
                  pl.BlockSpec((1, 1, N_MOD, dm), lambda b, t: (b, jnp.where(t + tile_start >= nct, 1, 0), 0, 0)),
                  tok(ya.shape[-1]), tok(yb_raw.shape[-1]), tok(pb.shape[-1]), tok(yc.shape[-1]),
                  full(s5_d), full(glu_w), full(glu_b), full(w_out)],
        out_specs=pl.BlockSpec((1, tm, dm), lambda b, t: (b, t, 0)),
        out_shape=jax.ShapeDtypeStruct((bsz, n_tiles * tm, dm), F32),
        compiler_params=pltpu.CompilerParams(dimension_semantics=("arbitrary", "arbitrary"),
                                             vmem_limit_bytes=_vmem_limit(32 * 1024 * 1024)),
        name="outproj",
    )(h, mods, ya, yb_raw, pb, yc, s5_d, glu_w, glu_b, w_out)


def _raster_to_column(t, rows):
    b, s, d = t.shape
    return t.reshape(b, rows, GRID_W, d).transpose(0, 2, 1, 3).reshape(b, s, d)


def _column_to_raster(t, rows):
    b, s, d = t.shape
    return t.reshape(b, GRID_W, rows, d).transpose(0, 2, 1, 3).reshape(b, s, d)


def kernel(x, c, ctx, c_ctx, mod_w, mod_b, ffn_w_in, ffn_w_out, w_in, w_out, a_conv_w, a_conv_b, a_dt_bias, a_log,
           a_d, a_norm_w, s5_lam_re, s5_lam_im, s5_log_step, s5_b_re, s5_b_im, s5_c_re, s5_c_im, s5_d, s5_glu_w,
           s5_glu_b, hg_lb_logits, hg_norm_w, final_norm_w):
    bsz, seq, dm = x.shape
    n_ctx = ctx.shape[1]
    depth = mod_w.shape[0]
    rows = seq // GRID_W
    lt = n_ctx + seq
    assert n_ctx % MIX_CHUNK == 0 and seq % MIX_CHUNK == 0 and bsz % V7X_SUBLANES == 0
    a_inner = A_HEADS * A_HEAD_DIM
    a_conv_dim = a_conv_w.shape[-1]
    b_width = s5_d.shape[-1]
    c_width = hg_norm_w.shape[-1]
    n_groups = b_width // B_GROUP

    pad = (-(bsz + 1)) % V7X_SUBLANES
    cvec = jnp.concatenate([c, c_ctx[None, :], jnp.zeros((pad, dm), F32)], axis=0)
    mod_all = _mod_vectors(cvec, mod_w, mod_b)

    p_lb = jax.nn.softmax(hg_lb_logits.astype(F32), axis=0)
    lower_bounds = jnp.cumsum(p_lb, axis=0) - p_lb[:1]

    h = jnp.concatenate([ctx, x], axis=1)
    nsc = lt // S5_CHUNK
    col_order = False
    for l in range(depth):
        last = l == depth - 1
        if (l % 2 == 1) != col_order:
            reorder = _column_to_raster if col_order else _raster_to_column
            h = jnp.concatenate([h[:, :n_ctx], reorder(h[:, n_ctx:], rows)], axis=1)
            col_order = not col_order
        m_lat = mod_all[l, :bsz].reshape(bsz, 1, N_MOD, dm)
        m_ctx = jnp.broadcast_to(mod_all[l, bsz].reshape(1, 1, N_MOD, dm), (bsz, 1, N_MOD, dm))
        mods = jnp.concatenate([m_ctx, m_lat], axis=1)
        fw = final_norm_w.reshape(1, dm).astype(F32)

        h = _ffn(h, mods, ffn_w_in[l, 0].astype(BF16), ffn_w_out[l, 0].astype(BF16), fw,
                 base=0, n_ctx=n_ctx, tile_start=0, final=False)

        wl = w_in[l]
        o_z, o_xbc, o_dt = 0, a_inner, a_inner + a_conv_dim
        o_b = o_dt + 2 * A_HEADS
        o_c = o_b + b_width
        w_cat = jnp.concatenate([wl[:, o_z:o_xbc], wl[:, o_xbc:o_dt], wl[:, o_dt:o_b],
                                 jnp.zeros((dm, V7X_LANES - 2 * A_HEADS), F32), wl[:, o_b:o_c], wl[:, o_c:]], axis=1)
        pz, pxbc, pdt, pb, pc = _inproj(h, mods, w_cat.astype(BF16), n_ctx=n_ctx)

        lane_pad = jnp.zeros((V7X_LANES - 2 * A_HEADS,), F32)
        dtb = jnp.concatenate([a_dt_bias[l].reshape(-1), lane_pad]).reshape(1, V7X_LANES)
        aneg = jnp.concatenate([-jnp.exp(a_log[l].astype(F32)).reshape(-1), lane_pad]).reshape(1, V7X_LANES)
        dsk = jnp.repeat(a_d[l].astype(F32), A_HEAD_DIM).reshape(1, a_inner)
        ya = _ssd(pz, pxbc, pdt, a_conv_w[l], a_conv_b[l].reshape(1, -1), dtb, aneg, dsk,
                  a_norm_w[l].reshape(1, -1), n_ctx=n_ctx)

        toe, wst, wout, dec = _s5_tables(s5_lam_re[l], s5_lam_im[l], s5_log_step[l], s5_b_re[l], s5_b_im[l],
                                         s5_c_re[l], s5_c_im[l])
        u2 = pb.reshape(bsz, nsc, S5_CHUNK, n_groups, B_GROUP).transpose(3, 1, 0, 2, 4)
        u2 = u2.reshape(n_groups, nsc * bsz, S5_CHUNK * B_GROUP)
        y2 = _s5_core(u2, toe, wst, wout, dec, bsz=bsz, n_ctx_chunks=n_ctx // S5_CHUNK)
        yb_raw = y2.reshape(n_groups, nsc, bsz, S5_CHUNK, B_GROUP).transpose(2, 1, 3, 0, 4).reshape(bsz, lt, b_width)

        yc = _hgrn(pc, lower_bounds[l].astype(F32), hg_norm_w[l].reshape(1, c_width), n_ctx=n_ctx)

        ts = (n_ctx // TOKEN_TILE) if last else 0
        h = _outproj(h, mods, ya, yb_raw, pb, yc, s5_d[l].reshape(1, -1), s5_glu_w[l].astype(BF16),
                     s5_glu_b[l].reshape(1, -1), w_out[l].astype(BF16), n_ctx=n_ctx, tile_start=ts)
        h = _ffn(h, mods, ffn_w_in[l, 1].astype(BF16), ffn_w_out[l, 1].astype(BF16), fw,
                 base=6, n_ctx=0 if last else n_ctx, tile_start=0, final=last)
    return _column_to_raster(h, rows) if col_order else h
```

```python
import functools
import math

import jax
import jax.numpy as jnp
from jax import lax
from jax.experimental import pallas as pl
from jax.experimental.pallas import tpu as pltpu

F32 = jnp.float32
BF16 = jnp.bfloat16
EPS = 1e-6
NEG_LOG2E = -1.4426950408889634

V7X_VMEM_BYTES = 64 * 1024 * 1024
V7X_LANES = 128
V7X_SUBLANES = 8
V7X_MXU_DIM = 256

GRID_W = 64
N_MOD = 9
A_HEADS = 8
A_HEAD_DIM = 64
A_GROUPS = 2
A_STATE = 64
A_CONV = 5
B_GROUP = 16
B_STATE = 64
C_HEADS = 4
C_KEY = 64

TOKEN_TILES = (512, 384, 256)
MIX_CHUNK = 256
S5_CHUNK = 16
GRID_COLS_PER_TILE = V7X_SUBLANES


def _vmem_limit(estimate_bytes):
    return int(min(V7X_VMEM_BYTES - 6 * 1024 * 1024, max(estimate_bytes, 16 * 1024 * 1024)))


def _token_tile(n):
    for tm in TOKEN_TILES:
        if n % tm == 0:
            return tm
    raise ValueError(f"token count {n} has no supported tile")


def _silu(x):
    return x * jax.nn.sigmoid(x)


def _rms(x):
    return x * lax.rsqrt(jnp.mean(x * x, axis=-1, keepdims=True) + EPS)


def _split3(x):
    hi = x.astype(BF16)
    r = x - hi.astype(F32)
    mid = r.astype(BF16)
    lo = (r - mid.astype(F32)).astype(BF16)
    return hi, mid, lo


def _dot01_left(m01, x):
    hi, mid, lo = _split3(x)
    d = lambda a: jnp.dot(m01, a, preferred_element_type=F32)
    return (d(lo) + d(mid)) + d(hi)


def _dot01_right(x, m01):
    hi, mid, lo = _split3(x)
    d = lambda a: jnp.dot(a, m01, preferred_element_type=F32)
    return (d(lo) + d(mid)) + d(hi)


def _dot_nt(a, b):
    return lax.dot_general(a, b, (((1,), (1,)), ((), ())), preferred_element_type=F32)


def _mod_spec(mod):
    per_sample = mod.shape[0] > 1
    return pl.BlockSpec((1,) + mod.shape[1:], lambda b, t: (b if per_sample else 0, 0, 0))


def _mod_kernel(c_ref, w_ref, b_ref, o_ref):
    s = _silu(c_ref[...]).astype(BF16)
    o_ref[0] = jnp.dot(s, w_ref[0].astype(BF16), preferred_element_type=F32) + b_ref[0]


def _mod_vectors(cvec, mod_w, mod_b):
    depth, dm, nm = mod_w.shape
    rows = cvec.shape[0]
    tn = 1024
    return pl.pallas_call(
        _mod_kernel,
        grid=(depth, nm // tn),
        in_specs=[pl.BlockSpec((rows, dm), lambda l, j: (0, 0)),
                  pl.BlockSpec((1, dm, tn), lambda l, j: (l, 0, j)),
                  pl.BlockSpec((1, 1, tn), lambda l, j: (l, 0, j))],
        out_specs=pl.BlockSpec((1, rows, tn), lambda l, j: (l, 0, j)),
        out_shape=jax.ShapeDtypeStruct((depth, rows, nm), F32),
        compiler_params=pltpu.CompilerParams(dimension_semantics=("arbitrary", "arbitrary")),
        name="mod_vectors",
    )(cvec, mod_w, mod_b.reshape(depth, 1, nm))


def _ff_chunks(d_ff):
    if d_ff % V7X_MXU_DIM:
        return ((0, d_ff),)
    tiles = d_ff // V7X_MXU_DIM
    first = (tiles + 1) // 2 * V7X_MXU_DIM
    return ((0, first), (first, d_ff - first)) if d_ff > first else ((0, d_ff),)


def _ffn_kernel(h_ref, mod_ref, win_ref, wout_ref, fw_ref, o_ref, *, base, d_ff, final, cols_in, cols_out):
    if cols_in:
        x = jnp.concatenate([h_ref[0, :, wv, :] for wv in range(cols_in)], axis=0)
    else:
        x = h_ref[0]
    m = mod_ref[0]
    u = (_rms(x) * (1.0 + m[base + 1:base + 2]) + m[base:base + 1]).astype(BF16)
    acc = jnp.zeros(x.shape, F32)
    for c0, cw in _ff_chunks(d_ff):
        g = jnp.dot(u, win_ref[:, c0:c0 + cw], preferred_element_type=F32)
        up = jnp.dot(u, win_ref[:, d_ff + c0:d_ff + c0 + cw], preferred_element_type=F32)
        a = (_silu(g) * up).astype(BF16)
        acc = acc + jnp.dot(a, wout_ref[c0:c0 + cw, :], preferred_element_type=F32)
    y = x + 0.5 * m[base + 2:base + 3] * acc
    if final:
        y = _rms(y) * fw_ref[...]
    if cols_out:
        nr = y.shape[0] // cols_out
        for wv in range(cols_out):
            o_ref[0, :, wv, :] = y[wv * nr:(wv + 1) * nr]
    else:
        o_ref[0] = y


def _ffn(h, mod, w_in, w_out, final_w, *, base, final, to_column=False, to_raster=False):
    bsz, lt, dm = h.shape
    d_ff = w_out.shape[0]
    grid_rows = lt // GRID_W
    cols = GRID_COLS_PER_TILE
    tm = grid_rows * cols if (to_column or to_raster) else _token_tile(lt)
    tok = pl.BlockSpec((1, tm, dm), lambda b, t: (b, t, 0))
    grid_blk = pl.BlockSpec((1, grid_rows, cols, dm), lambda b, t: (b, 0, t, 0))
    kern = functools.partial(_ffn_kernel, base=base, d_ff=d_ff, final=final,
                             cols_in=cols if to_column else 0, cols_out=cols if to_raster else 0)
    w_bytes = (w_in.size + w_out.size) * 2
    est = w_bytes + 5 * tm * dm * 4 + 3 * tm * d_ff * 4 + 6 * 1024 * 1024
    out = pl.pallas_call(
        kern,
        grid=(bsz, lt // tm),
        in_specs=[grid_blk if to_column else tok,
                  _mod_spec(mod),
                  pl.BlockSpec(w_in.shape, lambda b, t: (0, 0), pipeline_mode=pl.Buffered(1)),
                  pl.BlockSpec(w_out.shape, lambda b, t: (0, 0), pipeline_mode=pl.Buffered(1)),
                  pl.BlockSpec((1, dm), lambda b, t: (0, 0))],
        out_specs=grid_blk if to_raster else tok,
        out_shape=jax.ShapeDtypeStruct((bsz, grid_rows, GRID_W, dm) if to_raster else (bsz, lt, dm), F32),
        compiler_params=pltpu.CompilerParams(dimension_semantics=("arbitrary", "arbitrary"),
                                             vmem_limit_bytes=_vmem_limit(est)),
        name="ffn",
    )(h.reshape(bsz, grid_rows, GRID_W, dm) if to_column else h, mod, w_in, w_out, final_w)
    return out.reshape(bsz, lt, dm)


IN_SPLIT = ((512, BF16), (768, F32), (128, F32), (256, F32), (256, BF16), (512, F32), (256, BF16), (256, BF16))


def _inproj_kernel(h_ref, mod_ref, w_ref, *out_refs):
    m = mod_ref[0]
    u = (_rms(h_ref[0]) * (1.0 + m[4:5]) + m[3:4]).astype(BF16)
    p = jnp.dot(u, w_ref[...], preferred_element_type=F32)
    off = 0
    for ref, (wd, dt) in zip(out_refs, IN_SPLIT):
        ref[0] = p[:, off:off + wd].astype(dt)
        off += wd


def _inproj(h, mod, w):
    bsz, lt, dm = h.shape
    tm = _token_tile(lt)
    tok = lambda wd: pl.BlockSpec((1, tm, wd), lambda b, t: (b, t, 0))
    return pl.pallas_call(
        _inproj_kernel,
        grid=(bsz, lt // tm),
        in_specs=[tok(dm), _mod_spec(mod),
                  pl.BlockSpec(w.shape, lambda b, t: (0, 0), pipeline_mode=pl.Buffered(1))],
        out_specs=[tok(wd) for wd, _ in IN_SPLIT],
        out_shape=[jax.ShapeDtypeStruct((bsz, lt, wd), dt) for wd, dt in IN_SPLIT],
        compiler_params=pltpu.CompilerParams(dimension_semantics=("arbitrary", "arbitrary"),
                                             vmem_limit_bytes=_vmem_limit(40 * 1024 * 1024)),
        name="inproj",
    )(h, mod, w)


def _seq_spec(a):
    return pl.BlockSpec((1,) + a.shape[1:], lambda b: (b,) + (0,) * (a.ndim - 1))


def _full_spec(a):
    return pl.BlockSpec(a.shape, lambda b: (0,) * a.ndim)


def _ssd_kernel(zc_ref, zl_ref, xc_in_ref, xl_in_ref, dtc_ref, dtl_ref, cw_ref, cb_ref, dtb_ref, aneg_ref,
                dsk_ref, nw_ref, oc_ref, ol_ref, xconv_ref, st_ref, *, q):
    inner = A_HEADS * A_HEAD_DIM
    gw = A_STATE
    hpg = A_HEADS // A_GROUPS
    gl = hpg * A_HEAD_DIM
    n_ctx = oc_ref.shape[1]
    segs = ((zc_ref, xc_in_ref, dtc_ref, oc_ref, 0), (zl_ref, xl_in_ref, dtl_ref, ol_ref, n_ctx))

    cw = cw_ref[...]
    cb = cb_ref[...]
    for _, xin_ref, _, o_ref, base in segs:
        ls = o_ref.shape[1]
        n_chunks = ls // q

        def conv_body(i, carry, xin_ref=xin_ref, base=base, ls=ls, n_chunks=n_chunks):
            t0 = pl.multiple_of(i * q, q)
            cur = xin_ref[0, pl.ds(t0, q), :]
            p0 = pl.multiple_of(jnp.maximum(t0 - V7X_SUBLANES, 0), V7X_SUBLANES)
            n0 = pl.multiple_of(jnp.minimum(t0 + q, ls - V7X_SUBLANES), V7X_SUBLANES)
            prev = jnp.where(i == 0, 0.0, xin_ref[0, pl.ds(p0, V7X_SUBLANES), :])
            nxt = jnp.where(i == n_chunks - 1, 0.0, xin_ref[0, pl.ds(n0, V7X_SUBLANES), :])
            ext = jnp.concatenate([prev, cur, nxt], axis=0)
            acc = cb
            for k in range(A_CONV):
                s0 = V7X_SUBLANES - A_CONV // 2 + k
                acc = acc + cw[k:k + 1] * ext[s0:s0 + q]
            xconv_ref[pl.ds(pl.multiple_of(base + t0, q), q), :] = _silu(acc)
            return carry

        lax.fori_loop(0, n_chunks, conv_body, 0)
        o_ref[...] = jnp.zeros(o_ref.shape, F32)

    st_ref[...] = jnp.zeros(st_ref.shape, F32)

    rowi = lax.broadcasted_iota(jnp.int32, (q, q), 0)
    coli = lax.broadcasted_iota(jnp.int32, (q, q), 1)
    er = lax.broadcasted_iota(jnp.int32, (V7X_LANES, inner), 0)
    ec = lax.broadcasted_iota(jnp.int32, (V7X_LANES, inner), 1)
    dtb = dtb_ref[...]
    aneg = aneg_ref[...]

    def chunk(d, seg, c_idx):
        _, _, dt_ref, o_ref, base = seg
        t0 = pl.multiple_of(c_idx * q, q)
        tri = (rowi >= coli) if d == 0 else (rowi <= coli)
        tri_bf = jnp.where(tri, 1.0, 0.0).astype(BF16)
        last = q - 1 if d == 0 else 0
        xc = xconv_ref[pl.ds(pl.multiple_of(base + t0, q), q), :]
        xs = xc[:, :inner]
        bm = xc[:, inner:inner + A_GROUPS * gw]
        cm = xc[:, inner + A_GROUPS * gw:]
        dt = jax.nn.softplus(dt_ref[0, pl.ds(t0, q), :] + dtb)
        d_a = dt * aneg
        cum = _dot01_left(tri_bf, d_a)
        expand = jnp.where(er == d * A_HEADS + (ec >> int(math.log2(A_HEAD_DIM))), 1.0, 0.0).astype(BF16)
        cumx = _dot01_right(cum, expand)
        dtx = _dot01_right(dt, expand)
        cum_t = cum.T
        cum_last = cumx[last:last + 1, :]
        xdt = (xs * dtx).astype(BF16)
        xw = (xs * (jnp.exp(cum_last - cumx) * dtx)).astype(BF16)
        ecum = jnp.exp(cumx)
        e_last = jnp.exp(cum_last)
        bm_t = bm.T
        ys = []
        for g in range(A_GROUPS):
            cg = cm[:, g * gw:(g + 1) * gw].astype(BF16)
            bg = bm[:, g * gw:(g + 1) * gw].astype(BF16)
            gmat = _dot_nt(cg, bg)
            st = st_ref[d, g]
            y_inter = jnp.dot(cg, st.astype(BF16), preferred_element_type=F32) * ecum[:, g * gl:(g + 1) * gl]
            for hh in range(hpg):
                h = g * hpg + hh
                lane = d * A_HEADS + h
                diff = cum[:, lane:lane + 1] - cum_t[lane:lane + 1, :]
                decay = jnp.where(tri, jnp.exp(jnp.where(tri, diff, 0.0)), 0.0)
                sc = (gmat * decay).astype(BF16)
                ys.append(jnp.dot(sc, xdt[:, h * A_HEAD_DIM:(h + 1) * A_HEAD_DIM], preferred_element_type=F32)
                          + y_inter[:, hh * A_HEAD_DIM:(hh + 1) * A_HEAD_DIM])
            upd = jnp.dot(bm_t[g * gw:(g + 1) * gw].astype(BF16), xw[:, g * gl:(g + 1) * gl],
                          preferred_element_type=F32)
            st_ref[d, g] = st * e_last[:, g * gl:(g + 1) * gl] + upd
        y = jnp.concatenate(ys, axis=1)
        o_ref[0, pl.ds(t0, q), :] = o_ref[0, pl.ds(t0, q), :] + y

    dsk = dsk_ref[...]
    nw = nw_ref[...]
    for seg in segs:
        n_chunks = seg[3].shape[1] // q

        def scan_body(i, carry, seg=seg, n_chunks=n_chunks):
            chunk(0, seg, i)
            chunk(1, seg, n_chunks - 1 - i)
            return carry

        lax.fori_loop(0, n_chunks, scan_body, 0)

    for z_ref, _, _, o_ref, base in segs:
        def fin_body(i, carry, z_ref=z_ref, o_ref=o_ref, base=base):
            t0 = pl.multiple_of(i * q, q)
            xs = xconv_ref[pl.ds(pl.multiple_of(base + t0, q), q), :inner]
            y = o_ref[0, pl.ds(t0, q), :] + dsk * xs
            y = y * _silu(z_ref[0, pl.ds(t0, q), :].astype(F32))
            o_ref[0, pl.ds(t0, q), :] = _rms(y) * nw
            return carry

        lax.fori_loop(0, o_ref.shape[1] // q, fin_body, 0)


def _ssd(zc, zl, xc, xl, dtc, dtl, cw, cb, dtb, aneg, dsk, nw):
    bsz, n_ctx, inner = zc.shape
    seq = zl.shape[1]
    lt = n_ctx + seq
    cd = xc.shape[-1]
    ins = (zc, zl, xc, xl, dtc, dtl)
    consts = (cw, cb, dtb, aneg, dsk, nw)
    est = 2 * lt * (2 * inner + 4 * cd + 4 * V7X_LANES + 4 * inner) + 4 * lt * cd + 12 * 1024 * 1024
    return pl.pallas_call(
        functools.partial(_ssd_kernel, q=MIX_CHUNK),
        grid=(bsz,),
        in_specs=[_seq_spec(a) for a in ins] + [_full_spec(a) for a in consts],
        out_specs=[pl.BlockSpec((1, n_ctx, inner), lambda b: (b, 0, 0)),
                   pl.BlockSpec((1, seq, inner), lambda b: (b, 0, 0))],
        out_shape=[jax.ShapeDtypeStruct((bsz, n_ctx, inner), F32), jax.ShapeDtypeStruct((bsz, seq, inner), F32)],
        scratch_shapes=[pltpu.VMEM((lt, cd), F32),
                        pltpu.VMEM((2, A_GROUPS, A_STATE, inner // A_GROUPS), F32)],
        compiler_params=pltpu.CompilerParams(dimension_semantics=("arbitrary",),
                                             vmem_limit_bytes=_vmem_limit(est)),
        name="ssd_mixer",
    )(*ins, *consts)


def _hgrn_kernel(qc_ref, ql_ref, ffc_ref, ffl_ref, fbc_ref, fbl_ref, vc_ref, vl_ref, gc_ref, gl_ref,
                 lb_ref, nw_ref, oc_ref, ol_ref, st_ref, *, c):
    w = C_HEADS * C_KEY
    nv = c // V7X_SUBLANES
    segs = ((qc_ref, (ffc_ref, fbc_ref), vc_ref, gc_ref, oc_ref), (ql_ref, (ffl_ref, fbl_ref), vl_ref, gl_ref, ol_ref))
    oc_ref[...] = jnp.zeros(oc_ref.shape, F32)
    ol_ref[...] = jnp.zeros(ol_ref.shape, F32)
    st_ref[...] = jnp.zeros(st_ref.shape, F32)

    rowi = lax.broadcasted_iota(jnp.int32, (c, c), 0)
    coli = lax.broadcasted_iota(jnp.int32, (c, c), 1)
    pair_bits = rowi ^ coli
    hr = lax.broadcasted_iota(jnp.int32, (w, w), 0)
    hc = lax.broadcasted_iota(jnp.int32, (w, w), 1)
    same_head = (hr >> int(math.log2(C_KEY))) == (hc >> int(math.log2(C_KEY)))
    bones = jnp.where(same_head, 1.0, 0.0).astype(BF16)
    trow = lax.broadcasted_iota(jnp.int32, (c, w), 0)
    sub3 = lax.broadcasted_iota(jnp.int32, (nv, V7X_SUBLANES, w), 1)

    def boundary_small(x3, p0, s):
        b = 1
        while b <= s:
            if p0 & b:
                x3 = jnp.where((sub3 & b) != 0, x3, pltpu.roll(x3, V7X_SUBLANES - b, axis=1))
            else:
                x3 = jnp.where((sub3 & b) != 0, pltpu.roll(x3, b, axis=1), x3)
            b *= 2
        return x3

    def chunk(d, seg, c_idx):
        q_ref, f_refs, v_ref, _, o_ref = seg
        t0 = pl.multiple_of(c_idx * c, c)
        causal = (rowi >= coli) if d == 0 else (rowi <= coli)
        tri_bf = jnp.where(causal, 1.0, 0.0).astype(BF16)
        last = c - 1 if d == 0 else 0
        qs = _silu(q_ref[0, pl.ds(t0, c), :].astype(F32))
        zf = f_refs[d][0, pl.ds(t0, c), :]
        lb = lb_ref[d:d + 1, :]
        f = lb + (1.0 - lb) * jax.nn.sigmoid(zf)
        logf = jnp.log(f)
        kk = 1.0 - f
        v = v_ref[0, pl.ds(t0, c), :]
        v32 = v.astype(F32)
        cum = _dot01_left(tri_bf, logf)
        cum_last = cum[last:last + 1, :]

        st = st_ref[d]
        qe = (qs * jnp.exp(cum)).astype(BF16)
        out = _dot_nt(qe, st.astype(BF16))
        kw = (kk * jnp.exp(cum_last - cum)).astype(BF16)
        upd = jnp.dot(v32.T.astype(BF16), kw, preferred_element_type=F32)
        st_ref[d] = jnp.where(same_head, st * jnp.exp(cum_last) + upd, 0.0)

        out = out + jnp.dot((qs * kk).astype(BF16), bones, preferred_element_type=F32) * v32

        cum3 = cum.reshape(nv, V7X_SUBLANES, w)
        edge = V7X_SUBLANES - 1 if d == 0 else 0
        edge3 = jnp.broadcast_to(cum3[:, edge:edge + 1, :], cum3.shape)
        scores = [jnp.zeros((c, c), F32) for _ in range(C_HEADS)]
        s = 1
        while s < c:
            grp = 2 * s
            p0 = (s - 1) if d == 0 else s
            if grp <= V7X_SUBLANES:
                refb = boundary_small(cum3, p0, s).reshape(c, w)
            else:
                gv = grp // V7X_SUBLANES
                refb = jnp.concatenate(
                    [jnp.broadcast_to(edge3[(gi * grp + p0) // V7X_SUBLANES:(gi * grp + p0) // V7X_SUBLANES + 1],
                                      (gv, V7X_SUBLANES, w)) for gi in range(c // grp)], axis=0).reshape(c, w)
            is_query = ((trow & s) != 0) if d == 0 else ((trow & s) == 0)
            m = (jnp.where(is_query, qs, kk) * jnp.exp2(jnp.abs(cum - refb) * NEG_LOG2E)).astype(BF16)
            take = pair_bits >= s
            for h in range(C_HEADS):
                mh = m[:, h * C_KEY:(h + 1) * C_KEY]
                scores[h] = jnp.where(take, _dot_nt(mh, mh), scores[h])
            s = grp
        strict = (rowi > coli) if d == 0 else (rowi < coli)
        out = out + jnp.concatenate(
            [jnp.dot(jnp.where(strict, scores[h], 0.0).astype(BF16), v[:, h * C_KEY:(h + 1) * C_KEY],
                     preferred_element_type=F32) for h in range(C_HEADS)], axis=1)
        o_ref[0, pl.ds(t0, c), :] = o_ref[0, pl.ds(t0, c), :] + out

    for seg in segs:
        n_chunks = seg[4].shape[1] // c

        def scan_body(i, carry, seg=seg, n_chunks=n_chunks):
            chunk(0, seg, i)
            chunk(1, seg, n_chunks - 1 - i)
            return carry

        lax.fori_loop(0, n_chunks, scan_body, 0)

    nw = nw_ref[...]
    for _, _, _, g_ref, o_ref in segs:
        def fin_body(i, carry, g_ref=g_ref, o_ref=o_ref):
            t0 = pl.multiple_of(i * c, c)
            o = o_ref[0, pl.ds(t0, c), :]
            ms = _dot01_right(o * o, bones) * (1.0 / C_KEY)
            y = o * lax.rsqrt(ms + EPS) * nw
            o_ref[0, pl.ds(t0, c), :] = y * _silu(g_ref[0, pl.ds(t0, c), :].astype(F32))
            return carry

        lax.fori_loop(0, o_ref.shape[1] // c, fin_body, 0)


def _hgrn(ctx_parts, lat_parts, lower, nw):
    qc, fc, ic, gc = ctx_parts
    ql, fl, il, gl = lat_parts
    bsz, n_ctx, w = qc.shape
    seq = ql.shape[1]
    half = lambda a, j: pl.BlockSpec((1, a.shape[1], w), lambda b: (b, 0, j))
    ins = (qc, ql, fc, fl, fc, fl, ic, il, gc, gl)
    specs = [_seq_spec(qc), _seq_spec(ql), half(fc, 0), half(fl, 0), half(fc, 1), half(fl, 1),
             _seq_spec(ic), _seq_spec(il), _seq_spec(gc), _seq_spec(gl)]
    est = 2 * (n_ctx + seq) * w * (2 + 8 + 2 + 2 + 4) + 20 * 1024 * 1024
    return pl.pallas_call(
        functools.partial(_hgrn_kernel, c=MIX_CHUNK),
        grid=(bsz,),
        in_specs=specs + [_full_spec(lower), _full_spec(nw)],
        out_specs=[pl.BlockSpec((1, n_ctx, w), lambda b: (b, 0, 0)), pl.BlockSpec((1, seq, w), lambda b: (b, 0, 0))],
        out_shape=[jax.ShapeDtypeStruct((bsz, n_ctx, w), F32), jax.ShapeDtypeStruct((bsz, seq, w), F32)],
        scratch_shapes=[pltpu.VMEM((2, w, w), F32)],
        compiler_params=pltpu.CompilerParams(dimension_semantics=("arbitrary",),
                                             vmem_limit_bytes=_vmem_limit(est)),
        name="hgrn_mixer",
    )(*ins, lower, nw)


def _s5_tables(lam_re, lam_im, log_step, b_re, b_im, c_re, c_im):
    q = S5_CHUNK
    hp = lax.Precision.HIGHEST
    lam_re = lam_re.astype(F32)
    lam_im = lam_im.astype(F32)
    step = jnp.exp(log_step.astype(F32))[..., None]
    tau = jnp.arange(q + 1, dtype=F32)[:, None, None, None]
    mag = jnp.exp(lam_re * step * tau)
    pr = mag * jnp.cos(lam_im * step * tau)
    pi = mag * jnp.sin(lam_im * step * tau)
    ar, ai = pr[1], pi[1]
    den = lam_re * lam_re + lam_im * lam_im
    nr = ar - 1.0
    kr = (nr * lam_re + ai * lam_im) / den
    ki = (ai * lam_re - nr * lam_im) / den
    b_re = b_re.astype(F32)
    b_im = b_im.astype(F32)
    br = kr[..., None] * b_re - ki[..., None] * b_im
    bi = kr[..., None] * b_im + ki[..., None] * b_re
    c_re = c_re.astype(F32)
    c_im = c_im.astype(F32)
    qr = pr[..., None] * br - pi[..., None] * bi
    qi = pr[..., None] * bi + pi[..., None] * br
    hk = jnp.einsum('dgon,tdgni->tdgoi', c_re, qr, precision=hp) - jnp.einsum('dgon,tdgni->tdgoi', c_im, qi, precision=hp)
    s_idx = jnp.arange(q)[:, None]
    t_idx = jnp.arange(q)[None, :]
    lag = jnp.stack([t_idx - s_idx, s_idx - t_idx])
    ok = lag >= 0
    lagc = jnp.clip(lag, 0, q)
    toe = jnp.stack([hk[lagc[d], d] for d in range(2)])
    toe = jnp.where(ok[:, :, :, None, None, None], toe, 0.0)
    toe = toe.transpose(0, 3, 1, 5, 2, 4)
    g = toe.shape[1]
    cg = toe.shape[3]
    toe = toe.reshape(2, g, q * cg, q * cg)
    tau_st = jnp.stack([q - 1 - jnp.arange(q), jnp.arange(q)])
    wst_r = jnp.stack([qr[tau_st[d], d] for d in range(2)])
    wst_i = jnp.stack([qi[tau_st[d], d] for d in range(2)])
    wst = jnp.concatenate([wst_r, wst_i], axis=3)
    wst = wst.transpose(0, 2, 1, 4, 3).reshape(2, g, q * cg, -1)
    tau_o = jnp.stack([jnp.arange(q) + 1, q - jnp.arange(q)])
    po_r = jnp.stack([pr[tau_o[d], d] for d in range(2)])
    po_i = jnp.stack([pi[tau_o[d], d] for d in range(2)])
    w_xr = c_re[:, None] * po_r[:, :, :, None, :] - c_im[:, None] * po_i[:, :, :, None, :]
    w_xi = -c_re[:, None] * po_i[:, :, :, None, :] - c_im[:, None] * po_r[:, :, :, None, :]
    wout = jnp.concatenate([w_xr, w_xi], axis=-1)
    wout = wout.transpose(0, 2, 4, 1, 3).reshape(2, g, -1, q * cg)
    dec = jnp.stack([pr[q], pi[q]], axis=1)
    ns = wst.shape[-1] // 2
    wst = jnp.stack([wst[..., :ns], wst[..., ns:]])
    wout = jnp.stack([wout[:, :, :ns, :], wout[:, :, ns:, :]])
    return toe.astype(BF16), wst.astype(BF16), wout.astype(BF16), dec


def _lane_block_transpose(arrs, blk):
    n = len(arrs)
    rows, width = arrs[0].shape
    lane_blk = lax.broadcasted_iota(jnp.int32, (rows, width), 1) >> int(math.log2(blk))
    k = 1
    while k < n:
        hi_half = (lane_blk & k) != 0
        new = list(arrs)
        for a in range(n):
            if a & k == 0:
                lo_arr, hi_arr = arrs[a], arrs[a | k]
                new[a] = jnp.where(hi_half, pltpu.roll(hi_arr, blk * k, axis=1), lo_arr)
                new[a | k] = jnp.where(hi_half, hi_arr, pltpu.roll(lo_arr, width - blk * k, axis=1))
        arrs = new
        k *= 2
    return arrs


def _gelu_tanh(x):
    return 0.5 * x * (1.0 + jnp.tanh(math.sqrt(2.0 / math.pi) * (x + 0.044715 * (x * x * x))))


def _s5_kernel(uc_ref, ul_ref, toe_ref, wst_ref, wout_ref, dec_ref, sd_ref, gw_ref, gb_ref,
               oc_ref, ol_ref, z_ref, xin_ref):
    q = uc_ref.shape[2]
    ncc = uc_ref.shape[1]
    n_chunks = ncc + ul_ref.shape[1]
    ng = toe_ref.shape[1]
    rows = [jnp.concatenate([uc_ref[0, :, s, :], ul_ref[0, :, s, :]], axis=0) for s in range(q)]
    ug = [a.astype(BF16) for a in _lane_block_transpose(rows, B_GROUP)]
    ys = [None] * ng
    for d in range(2):
        for g in range(ng):
            ys[g] = jnp.dot(ug[g], toe_ref[d, g], preferred_element_type=F32) + (0.0 if ys[g] is None else ys[g])
            for part in range(2):
                z_ref[part, d, :, g, :] = jnp.dot(ug[g], wst_ref[part, d, g], preferred_element_type=F32)

    ar = [dec_ref[d, 0] for d in range(2)]
    ai = [dec_ref[d, 1] for d in range(2)]

    def body(i, carry):
        nxt = []
        for d in range(2):
            if d == 0:
                c_idx = i
            else:
                c_idx = jnp.where(i < ncc, ncc - 1 - i, n_chunks - 1 - (i - ncc))
            xr, xi = carry[2 * d], carry[2 * d + 1]
            xin_ref[0, d, c_idx] = xr
            xin_ref[1, d, c_idx] = xi
            nxt.append(ar[d] * xr - ai[d] * xi + z_ref[0, d, c_idx])
            nxt.append(ar[d] * xi + ai[d] * xr + z_ref[1, d, c_idx])
        return tuple(nxt)

    zero = jnp.zeros(ar[0].shape, F32)
    lax.fori_loop(0, n_chunks, body, (zero, zero, zero, zero), unroll=2)
    for d in range(2):
        for g in range(ng):
            for part in range(2):
                ys[g] = ys[g] + jnp.dot(xin_ref[part, d, :, g, :].astype(BF16), wout_ref[part, d, g],
                                        preferred_element_type=F32)
    yt = _lane_block_transpose(ys, B_GROUP)
    sd = sd_ref[...]
    gw = gw_ref[...]
    gb = gb_ref[...]
    for t in range(q):
        y = _gelu_tanh(yt[t] + sd * rows[t])
        gate = jnp.dot(y.astype(BF16), gw, preferred_element_type=F32) + gb
        y = y * jax.nn.sigmoid(gate)
        oc_ref[0, :, t, :] = y[:ncc]
        ol_ref[0, :, t, :] = y[ncc:]


def _s5(pbc, pbl, toe, wst, wout, dec, s5_d, glu_w, glu_b):
    bsz, n_ctx, wd = pbc.shape
    seq = pbl.shape[1]
    q = S5_CHUNK
    ng = toe.shape[1]
    ns = wst.shape[-1]
    n_chunks = (n_ctx + seq) // q
    uc = pbc.reshape(bsz, n_ctx // q, q, wd)
    ul = pbl.reshape(bsz, seq // q, q, wd)
    consts = (toe, wst, wout, dec, s5_d, glu_w, glu_b)
    oc, ol = pl.pallas_call(
        _s5_kernel,
        grid=(bsz,),
        in_specs=[_seq_spec(uc), _seq_spec(ul)] + [_full_spec(a) for a in consts],
        out_specs=[_seq_spec(uc), _seq_spec(ul)],
        out_shape=[jax.ShapeDtypeStruct(uc.shape, F32), jax.ShapeDtypeStruct(ul.shape, F32)],
        scratch_shapes=[pltpu.VMEM((2, 2, n_chunks, ng, ns), F32), pltpu.VMEM((2, 2, n_chunks, ng, ns), F32)],
        compiler_params=pltpu.CompilerParams(dimension_semantics=("arbitrary",),
                                             vmem_limit_bytes=_vmem_limit(40 * 1024 * 1024)),
        name="s5_mixer",
    )(uc, ul, *consts)
    return oc.reshape(bsz, n_ctx, wd), ol.reshape(bsz, seq, wd)


def _outproj_kernel(h_ref, mod_ref, ya_ref, yb_ref, yc_ref, w_ref, o_ref):
    mix = jnp.concatenate([ya_ref[0], yb_ref[0], yc_ref[0]], axis=1).astype(BF16)
    y = jnp.dot(mix, w_ref[...], preferred_element_type=F32)
    o_ref[0] = h_ref[0] + mod_ref[0][5:6] * y


def _outproj(h, mod, ya, yb, yc, w_out):
    bsz, lt, dm = h.shape
    tm = _token_tile(lt)
    tok = lambda wd: pl.BlockSpec((1, tm, wd), lambda b, t: (b, t, 0))
    return pl.pallas_call(
        _outproj_kernel,
        grid=(bsz, lt // tm),
        in_specs=[tok(dm), _mod_spec(mod), tok(ya.shape[-1]), tok(yb.shape[-1]), tok(yc.shape[-1]),
                  pl.BlockSpec(w_out.shape, lambda b, t: (0, 0))],
        out_specs=tok(dm),
        out_shape=jax.ShapeDtypeStruct((bsz, lt, dm), F32),
        compiler_params=pltpu.CompilerParams(dimension_semantics=("arbitrary", "arbitrary"),
                                             vmem_limit_bytes=_vmem_limit(32 * 1024 * 1024)),
        name="outproj",
    )(h, mod, ya, yb, yc, w_out)


def _raster_to_column(t, rows):
    b, s, d = t.shape
    return t.reshape(b, rows, GRID_W, d).transpose(0, 2, 1, 3).reshape(b, s, d)


def _column_to_raster(t, rows):
    b, s, d = t.shape
    return t.reshape(b, GRID_W, rows, d).transpose(0, 2, 1, 3).reshape(b, s, d)


def kernel(x, c, ctx, c_ctx, mod_w, mod_b, ffn_w_in, ffn_w_out, w_in, w_out, a_conv_w, a_conv_b, a_dt_bias, a_log,
           a_d, a_norm_w, s5_lam_re, s5_lam_im, s5_log_step, s5_b_re, s5_b_im, s5_c_re, s5_c_im, s5_d, s5_glu_w,
           s5_glu_b, hg_lb_logits, hg_norm_w, final_norm_w):
    bsz, seq, dm = x.shape
    n_ctx = ctx.shape[1]
    depth = mod_w.shape[0]
    rows = seq // GRID_W
    assert n_ctx % MIX_CHUNK == 0 and seq % MIX_CHUNK == 0 and seq % GRID_W == 0
    a_inner = A_HEADS * A_HEAD_DIM
    a_conv_dim = a_conv_w.shape[-1]
    c_width = hg_norm_w.shape[-1]

    pad = (-(bsz + 1)) % V7X_SUBLANES
    cvec = jnp.concatenate([c, c_ctx[None, :], jnp.zeros((pad, dm), F32)], axis=0)
    mod_all = _mod_vectors(cvec, mod_w, mod_b)

    p_lb = jax.nn.softmax(hg_lb_logits.astype(F32), axis=0)
    lower_bounds = jnp.cumsum(p_lb, axis=0) - p_lb[:1]
    fw = final_norm_w.reshape(1, dm).astype(F32)

    h_lat, h_ctx = x, ctx
    col_order = False
    for l in range(depth):
        last = l == depth - 1
        m_lat = mod_all[l, :bsz].reshape(bsz, N_MOD, dm)
        m_ctx = mod_all[l, bsz].reshape(1, N_MOD, dm)
        wi0, wo0 = ffn_w_in[l, 0].astype(BF16), ffn_w_out[l, 0].astype(BF16)
        wi1, wo1 = ffn_w_in[l, 1].astype(BF16), ffn_w_out[l, 1].astype(BF16)

        want_col = l % 2 == 1
        if col_order and not want_col:
            h_lat = _column_to_raster(h_lat, rows)
            col_order = False
        h_lat = _ffn(h_lat, m_lat, wi0, wo0, fw, base=0, final=False, to_column=want_col and not col_order)
        col_order = want_col
        h_ctx = _ffn(h_ctx, m_ctx, wi0, wo0, fw, base=0, final=False)

        wl = w_in[l]
        o_dt = a_inner + a_conv_dim
        o_b = o_dt + 2 * A_HEADS
        w_cat = jnp.concatenate([wl[:, :o_b], jnp.zeros((dm, V7X_LANES - 2 * A_HEADS), F32), wl[:, o_b:]],
                                axis=1).astype(BF16)
        zl, xl, dtl, pbl, ql, fl, il, gl = _inproj(h_lat, m_lat, w_cat)
        zc, xc, dtc, pbc, qc, fc, ic, gc = _inproj(h_ctx, m_ctx, w_cat)

        lane_pad = jnp.zeros((V7X_LANES - 2 * A_HEADS,), F32)
        dtb = jnp.concatenate([a_dt_bias[l].reshape(-1), lane_pad]).reshape(1, V7X_LANES)
        aneg = jnp.concatenate([-jnp.exp(a_log[l].astype(F32)).reshape(-1), lane_pad]).reshape(1, V7X_LANES)
        dsk = jnp.repeat(a_d[l].astype(F32), A_HEAD_DIM).reshape(1, a_inner)
        ya_c, ya_l = _ssd(zc, zl, xc, xl, dtc, dtl, a_conv_w[l], a_conv_b[l].reshape(1, -1), dtb, aneg, dsk,
                          a_norm_w[l].reshape(1, -1))

        toe, wst, wout, dec = _s5_tables(s5_lam_re[l], s5_lam_im[l], s5_log_step[l], s5_b_re[l], s5_b_im[l],
                                         s5_c_re[l], s5_c_im[l])
        yb_c, yb_l = _s5(pbc, pbl, toe, wst, wout, dec, s5_d[l].reshape(1, -1), s5_glu_w[l].astype(BF16),
                         s5_glu_b[l].reshape(1, -1))

        yc_c, yc_l = _hgrn((qc, fc, ic, gc), (ql, fl, il, gl), lower_bounds[l].astype(F32),
                           hg_norm_w[l].reshape(1, c_width))

        w_o = w_out[l].astype(BF16)
        h_lat = _outproj(h_lat, m_lat, ya_l, yb_l, yc_l, w_o)
        h_lat = _ffn(h_lat, m_lat, wi1, wo1, fw, base=6, final=last, to_raster=last and col_order)
        if last:
            col_order = False
        else:
            h_ctx = _outproj(h_ctx, m_ctx, ya_c, yb_c, yc_c, w_o)
            h_ctx = _ffn(h_ctx, m_ctx, wi1, wo1, fw, base=6, final=False)
    return h_lat
```

```python
import functools
import math

import jax
import jax.numpy as jnp
from jax import lax
from jax.experimental import pallas as pl
from jax.experimental.pallas import tpu as pltpu

F32 = jnp.float32
BF16 = jnp.bfloat16
EPS = 1e-6
NEG_LOG2E = -1.4426950408889634

V7X_VMEM_BYTES = 64 * 1024 * 1024
V7X_LANES = 128
V7X_SUBLANES = 8
V7X_MXU_DIM = 256

GRID_W = 64
N_MOD = 9
A_HEADS = 8
A_HEAD_DIM = 64
A_GROUPS = 2
A_STATE = 64
A_CONV = 5
B_GROUP = 16
B_STATE = 64
C_HEADS = 4
C_KEY = 64

TOKEN_TILES = (512, 384, 256)
MIX_CHUNK = 256
SCAN_UNROLL = 2
S5_CHUNK = 16
GRID_COLS_PER_TILE = 2 * V7X_SUBLANES


def _vmem_limit(estimate_bytes):
    return int(min(V7X_VMEM_BYTES - 6 * 1024 * 1024, max(estimate_bytes, 16 * 1024 * 1024)))


def _token_tile(n):
    for tm in TOKEN_TILES:
        if n % tm == 0:
            return tm
    raise ValueError(f"token count {n} has no supported tile")


def _silu(x):
    return x * jax.nn.sigmoid(x)


def _rms(x):
    return x * lax.rsqrt(jnp.mean(x * x, axis=-1, keepdims=True) + EPS)


def _split3(x):
    hi = x.astype(BF16)
    r = x - hi.astype(F32)
    mid = r.astype(BF16)
    lo = (r - mid.astype(F32)).astype(BF16)
    return hi, mid, lo


def _dot01_left(m01, x):
    hi, mid, lo = _split3(x)
    d = lambda a: jnp.dot(m01, a, preferred_element_type=F32)
    return (d(lo) + d(mid)) + d(hi)


def _dot01_right(x, m01):
    hi, mid, lo = _split3(x)
    d = lambda a: jnp.dot(a, m01, preferred_element_type=F32)
    return (d(lo) + d(mid)) + d(hi)


def _dot_nt(a, b):
    return lax.dot_general(a, b, (((1,), (1,)), ((), ())), preferred_element_type=F32)


def _mod_spec(mod):
    per_sample = mod.shape[0] > 1
    return pl.BlockSpec((1,) + mod.shape[1:], lambda b, t: (b if per_sample else 0, 0, 0))


def _mod_kernel(c_ref, w_ref, b_ref, o_ref):
    s = _silu(c_ref[...]).astype(BF16)
    o_ref[0] = jnp.dot(s, w_ref[0].astype(BF16), preferred_element_type=F32) + b_ref[0]


def _mod_vectors(cvec, mod_w, mod_b):
    depth, dm, nm = mod_w.shape
    rows = cvec.shape[0]
    tn = 1024
    return pl.pallas_call(
        _mod_kernel,
        grid=(depth, nm // tn),
        in_specs=[pl.BlockSpec((rows, dm), lambda l, j: (0, 0)),
                  pl.BlockSpec((1, dm, tn), lambda l, j: (l, 0, j)),
                  pl.BlockSpec((1, 1, tn), lambda l, j: (l, 0, j))],
        out_specs=pl.BlockSpec((1, rows, tn), lambda l, j: (l, 0, j)),
        out_shape=jax.ShapeDtypeStruct((depth, rows, nm), F32),
        compiler_params=pltpu.CompilerParams(dimension_semantics=("arbitrary", "arbitrary")),
        name="mod_vectors",
    )(cvec, mod_w, mod_b.reshape(depth, 1, nm))


def _ff_chunks(d_ff):
    if d_ff % V7X_MXU_DIM:
        return ((0, d_ff),)
    tiles = d_ff // V7X_MXU_DIM
    first = (tiles + 1) // 2 * V7X_MXU_DIM
    return ((0, first), (first, d_ff - first)) if d_ff > first else ((0, d_ff),)


def _ffn_kernel(h_ref, mod_ref, win_ref, wout_ref, fw_ref, *rest, base, d_ff, final, cols_in, cols_out, mixed):
    o_ref = rest[-1]
    if cols_in:
        x = jnp.concatenate([h_ref[0, :, wv, :] for wv in range(cols_in)], axis=0)
    else:
        x = h_ref[0]
    m = mod_ref[0]
    if mixed:
        ya_ref, yb_ref, yc_ref, wmix_ref = rest[:4]
        mix = jnp.concatenate([ya_ref[0], yb_ref[0], yc_ref[0]], axis=1).astype(BF16)
        x = x + m[base - 1:base] * jnp.dot(mix, wmix_ref[...], preferred_element_type=F32)
    u = (_rms(x) * (1.0 + m[base + 1:base + 2]) + m[base:base + 1]).astype(BF16)
    acc = jnp.zeros(x.shape, F32)
    for c0, cw in _ff_chunks(d_ff):
        g = jnp.dot(u, win_ref[:, c0:c0 + cw], preferred_element_type=F32)
        up = jnp.dot(u, win_ref[:, d_ff + c0:d_ff + c0 + cw], preferred_element_type=F32)
        a = (_silu(g) * up).astype(BF16)
        acc = acc + jnp.dot(a, wout_ref[c0:c0 + cw, :], preferred_element_type=F32)
    y = x + 0.5 * m[base + 2:base + 3] * acc
    if final:
        y = _rms(y) * fw_ref[...]
    if cols_out:
        nr = y.shape[0] // cols_out
        for wv in range(cols_out):
            o_ref[0, :, wv, :] = y[wv * nr:(wv + 1) * nr]
    else:
        o_ref[0] = y


def _ffn(h, mod, w_in, w_out, final_w, *, base, final, to_column=False, to_raster=False, mixed=None):
    bsz, lt, dm = h.shape
    d_ff = w_out.shape[0]
    grid_rows = lt // GRID_W
    cols = GRID_COLS_PER_TILE
    tm = grid_rows * cols if (to_column or to_raster) else _token_tile(lt)
    tok = lambda wd: pl.BlockSpec((1, tm, wd), lambda b, t: (b, t, 0))
    const = lambda a: pl.BlockSpec(a.shape, lambda b, t: (0, 0), pipeline_mode=pl.Buffered(1))
    grid_blk = pl.BlockSpec((1, grid_rows, cols, dm), lambda b, t: (b, 0, t, 0))
    kern = functools.partial(_ffn_kernel, base=base, d_ff=d_ff, final=final, mixed=mixed is not None,
                             cols_in=cols if to_column else 0, cols_out=cols if to_raster else 0)
    extra, extra_specs = (), []
    if mixed is not None:
        extra = tuple(mixed)
        extra_specs = [tok(a.shape[-1]) for a in mixed[:3]] + [const(mixed[3])]
    w_bytes = (w_in.size + w_out.size + (mixed[3].size if mixed is not None else 0)) * 2
    est = w_bytes + 8 * tm * dm * 4 + 3 * tm * d_ff * 4 + 6 * 1024 * 1024
    out = pl.pallas_call(
        kern,
        grid=(bsz, lt // tm),
        in_specs=[grid_blk if to_column else tok(dm), _mod_spec(mod), const(w_in), const(w_out),
                  pl.BlockSpec((1, dm), lambda b, t: (0, 0))] + extra_specs,
        out_specs=grid_blk if to_raster else tok(dm),
        out_shape=jax.ShapeDtypeStruct((bsz, grid_rows, GRID_W, dm) if to_raster else (bsz, lt, dm), F32),
        compiler_params=pltpu.CompilerParams(dimension_semantics=("arbitrary", "arbitrary"),
                                             vmem_limit_bytes=_vmem_limit(est)),
        name="ffn",
    )(h.reshape(bsz, grid_rows, GRID_W, dm) if to_column else h, mod, w_in, w_out, final_w, *extra)
    return out.reshape(bsz, lt, dm)


IN_SPLIT = ((512, BF16), (768, F32), (128, F32), (256, F32), (256, BF16), (512, F32), (256, BF16), (256, BF16))


def _inproj_kernel(h_ref, mod_ref, w_ref, *out_refs):
    m = mod_ref[0]
    u = (_rms(h_ref[0]) * (1.0 + m[4:5]) + m[3:4]).astype(BF16)
    p = jnp.dot(u, w_ref[...], preferred_element_type=F32)
    off = 0
    for ref, (wd, dt) in zip(out_refs, IN_SPLIT):
        ref[0] = p[:, off:off + wd].astype(dt)
        off += wd


def _inproj(h, mod, w):
    bsz, lt, dm = h.shape
    tm = _token_tile(lt)
    tok = lambda wd: pl.BlockSpec((1, tm, wd), lambda b, t: (b, t, 0))
    return pl.pallas_call(
        _inproj_kernel,
        grid=(bsz, lt // tm),
        in_specs=[tok(dm), _mod_spec(mod),
                  pl.BlockSpec(w.shape, lambda b, t: (0, 0), pipeline_mode=pl.Buffered(1))],
        out_specs=[tok(wd) for wd, _ in IN_SPLIT],
        out_shape=[jax.ShapeDtypeStruct((bsz, lt, wd), dt) for wd, dt in IN_SPLIT],
        compiler_params=pltpu.CompilerParams(dimension_semantics=("arbitrary", "arbitrary"),
                                             vmem_limit_bytes=_vmem_limit(40 * 1024 * 1024)),
        name="inproj",
    )(h, mod, w)


def _seq_spec(a):
    return pl.BlockSpec((1,) + a.shape[1:], lambda b: (b,) + (0,) * (a.ndim - 1))


def _full_spec(a):
    return pl.BlockSpec(a.shape, lambda b: (0,) * a.ndim)


def _ssd_kernel(zc_ref, zl_ref, xc_in_ref, xl_in_ref, dtc_ref, dtl_ref, cw_ref, cb_ref, dtb_ref, aneg_ref,
                dsk_ref, nw_ref, oc_ref, ol_ref, xconv_ref, st_ref, *, q):
    inner = A_HEADS * A_HEAD_DIM
    gw = A_STATE
    hpg = A_HEADS // A_GROUPS
    gl = hpg * A_HEAD_DIM
    n_ctx = oc_ref.shape[1]
    segs = ((zc_ref, xc_in_ref, dtc_ref, oc_ref, 0), (zl_ref, xl_in_ref, dtl_ref, ol_ref, n_ctx))

    cw = cw_ref[...]
    cb = cb_ref[...]
    for _, xin_ref, _, o_ref, base in segs:
        ls = o_ref.shape[1]
        n_chunks = ls // q

        def conv_body(i, carry, xin_ref=xin_ref, base=base, ls=ls, n_chunks=n_chunks):
            t0 = pl.multiple_of(i * q, q)
            cur = xin_ref[0, pl.ds(t0, q), :]
            p0 = pl.multiple_of(jnp.maximum(t0 - V7X_SUBLANES, 0), V7X_SUBLANES)
            n0 = pl.multiple_of(jnp.minimum(t0 + q, ls - V7X_SUBLANES), V7X_SUBLANES)
            prev = jnp.where(i == 0, 0.0, xin_ref[0, pl.ds(p0, V7X_SUBLANES), :])
            nxt = jnp.where(i == n_chunks - 1, 0.0, xin_ref[0, pl.ds(n0, V7X_SUBLANES), :])
            ext = jnp.concatenate([prev, cur, nxt], axis=0)
            acc = cb
            for k in range(A_CONV):
                s0 = V7X_SUBLANES - A_CONV // 2 + k
                acc = acc + cw[k:k + 1] * ext[s0:s0 + q]
            xconv_ref[pl.ds(pl.multiple_of(base + t0, q), q), :] = _silu(acc)
            return carry

        lax.fori_loop(0, n_chunks, conv_body, 0)
        o_ref[...] = jnp.zeros(o_ref.shape, F32)

    st_ref[...] = jnp.zeros(st_ref.shape, F32)

    rowi = lax.broadcasted_iota(jnp.int32, (q, q), 0)
    coli = lax.broadcasted_iota(jnp.int32, (q, q), 1)
    er = lax.broadcasted_iota(jnp.int32, (V7X_LANES, inner), 0)
    ec = lax.broadcasted_iota(jnp.int32, (V7X_LANES, inner), 1)
    dtb = dtb_ref[...]
    aneg = aneg_ref[...]

    def load_chunk(seg, t0):
        dt_ref, base = seg[2], seg[4]
        xc = xconv_ref[pl.ds(pl.multiple_of(base + t0, q), q), :]
        dt = jax.nn.softplus(dt_ref[0, pl.ds(t0, q), :] + dtb)
        return xc[:, :inner], xc[:, inner:inner + A_GROUPS * gw], xc[:, inner + A_GROUPS * gw:], dt

    def scan_sum(d, d_a):
        tri = (rowi >= coli) if d == 0 else (rowi <= coli)
        return _dot01_left(jnp.where(tri, 1.0, 0.0).astype(BF16), d_a)

    def carried(d, seg, t0, xs, bm, cm, dt, cum):
        o_ref = seg[3]
        last = q - 1 if d == 0 else 0
        expand = jnp.where(er == d * A_HEADS + (ec >> int(math.log2(A_HEAD_DIM))), 1.0, 0.0).astype(BF16)
        e_hi, e_mid, _ = _split3(jnp.exp(cum))
        ecum = (jnp.dot(e_mid, expand, preferred_element_type=F32) + jnp.dot(e_hi, expand, preferred_element_type=F32))
        wcol = (jnp.exp(cum[last:last + 1, :] - cum) * dt).astype(BF16)
        xw = (xs * jnp.dot(wcol, expand, preferred_element_type=F32)).astype(BF16)
        e_last = ecum[last:last + 1, :]
        bm_t = bm.T
        ys = []
        for g in range(A_GROUPS):
            cg = cm[:, g * gw:(g + 1) * gw].astype(BF16)
            st = st_ref[d, g]
            ys.append(jnp.dot(cg, st.astype(BF16), preferred_element_type=F32) * ecum[:, g * gl:(g + 1) * gl])
            upd = jnp.dot(bm_t[g * gw:(g + 1) * gw].astype(BF16), xw[:, g * gl:(g + 1) * gl],
                          preferred_element_type=F32)
            st_ref[d, g] = st * e_last[:, g * gl:(g + 1) * gl] + upd
        o_ref[0, pl.ds(t0, q), :] = o_ref[0, pl.ds(t0, q), :] + jnp.concatenate(ys, axis=1)

    lower = rowi > coli
    diag = rowi == coli

    def step(seg, i, n_chunks):
        o_ref = seg[3]
        t0 = pl.multiple_of(i * q, q)
        xs, bm, cm, dt = load_chunk(seg, t0)
        d_a = dt * aneg
        cum_f = scan_sum(0, d_a)
        cum_b = scan_sum(1, d_a)
        log_dt = jnp.log(dt)
        col_f, col_b = cum_f, cum_b
        row_f = (cum_f - log_dt).T
        row_b = (cum_b - log_dt).T
        dt_t = dt.T
        xs_bf = xs.astype(BF16)
        ys = []
        for g in range(A_GROUPS):
            cg = cm[:, g * gw:(g + 1) * gw].astype(BF16)
            bg = bm[:, g * gw:(g + 1) * gw].astype(BF16)
            gmat = _dot_nt(cg, bg)
            for hh in range(hpg):
                h = g * hpg + hh
                hb = A_HEADS + h
                arg = jnp.where(lower, col_f[:, h:h + 1] - row_f[h:h + 1, :], col_b[:, hb:hb + 1] - row_b[hb:hb + 1, :])
                wgt = jnp.where(diag, dt_t[h:h + 1, :] + dt_t[hb:hb + 1, :], jnp.exp(arg))
                sc = (gmat * wgt).astype(BF16)
                ys.append(jnp.dot(sc, xs_bf[:, h * A_HEAD_DIM:(h + 1) * A_HEAD_DIM], preferred_element_type=F32))
        o_ref[0, pl.ds(t0, q), :] = o_ref[0, pl.ds(t0, q), :] + jnp.concatenate(ys, axis=1)

        carried(0, seg, t0, xs, bm, cm, dt, cum_f)
        tb = pl.multiple_of((n_chunks - 1 - i) * q, q)
        xsb, bmb, cmb, dtb_ = load_chunk(seg, tb)
        carried(1, seg, tb, xsb, bmb, cmb, dtb_, scan_sum(1, dtb_ * aneg))

    dsk = dsk_ref[...]
    nw = nw_ref[...]
    for seg in segs:
        n_chunks = seg[3].shape[1] // q
        lax.fori_loop(0, n_chunks, lambda i, carry, seg=seg, n_chunks=n_chunks: (step(seg, i, n_chunks), carry)[1], 0,
                      unroll=math.gcd(n_chunks, SCAN_UNROLL))

    for z_ref, _, _, o_ref, base in segs:
        def fin_body(i, carry, z_ref=z_ref, o_ref=o_ref, base=base):
            t0 = pl.multiple_of(i * q, q)
            xs = xconv_ref[pl.ds(pl.multiple_of(base + t0, q), q), :inner]
            y = o_ref[0, pl.ds(t0, q), :] + dsk * xs
            y = y * _silu(z_ref[0, pl.ds(t0, q), :].astype(F32))
            o_ref[0, pl.ds(t0, q), :] = _rms(y) * nw
            return carry

        lax.fori_loop(0, o_ref.shape[1] // q, fin_body, 0)


def _ssd(zc, zl, xc, xl, dtc, dtl, cw, cb, dtb, aneg, dsk, nw):
    bsz, n_ctx, inner = zc.shape
    seq = zl.shape[1]
    lt = n_ctx + seq
    cd = xc.shape[-1]
    ins = (zc, zl, xc, xl, dtc, dtl)
    consts = (cw, cb, dtb, aneg, dsk, nw)
    est = 2 * lt * (2 * inner + 4 * cd + 4 * V7X_LANES + 4 * inner) + 4 * lt * cd + 12 * 1024 * 1024
    return pl.pallas_call(
        functools.partial(_ssd_kernel, q=MIX_CHUNK),
        grid=(bsz,),
        in_specs=[_seq_spec(a) for a in ins] + [_full_spec(a) for a in consts],
        out_specs=[pl.BlockSpec((1, n_ctx, inner), lambda b: (b, 0, 0)),
                   pl.BlockSpec((1, seq, inner), lambda b: (b, 0, 0))],
        out_shape=[jax.ShapeDtypeStruct((bsz, n_ctx, inner), F32), jax.ShapeDtypeStruct((bsz, seq, inner), F32)],
        scratch_shapes=[pltpu.VMEM((lt, cd), F32),
                        pltpu.VMEM((2, A_GROUPS, A_STATE, inner // A_GROUPS), F32)],
        compiler_params=pltpu.CompilerParams(dimension_semantics=("arbitrary",),
                                             vmem_limit_bytes=_vmem_limit(est)),
        name="ssd_mixer",
    )(*ins, *consts)


def _hgrn_kernel(qc_ref, ql_ref, ffc_ref, ffl_ref, fbc_ref, fbl_ref, vc_ref, vl_ref, gc_ref, gl_ref,
                 lb_ref, nw_ref, oc_ref, ol_ref, st_ref, *, c):
    w = C_HEADS * C_KEY
    nv = c // V7X_SUBLANES
    segs = ((qc_ref, (ffc_ref, fbc_ref), vc_ref, gc_ref, oc_ref), (ql_ref, (ffl_ref, fbl_ref), vl_ref, gl_ref, ol_ref))
    oc_ref[...] = jnp.zeros(oc_ref.shape, F32)
    ol_ref[...] = jnp.zeros(ol_ref.shape, F32)
    st_ref[...] = jnp.zeros(st_ref.shape, F32)

    rowi = lax.broadcasted_iota(jnp.int32, (c, c), 0)
    coli = lax.broadcasted_iota(jnp.int32, (c, c), 1)
    pair_bits = rowi ^ coli
    hr = lax.broadcasted_iota(jnp.int32, (w, w), 0)
    hc = lax.broadcasted_iota(jnp.int32, (w, w), 1)
    same_head = (hr >> int(math.log2(C_KEY))) == (hc >> int(math.log2(C_KEY)))
    bones = jnp.where(same_head, 1.0, 0.0).astype(BF16)
    trow = lax.broadcasted_iota(jnp.int32, (c, w), 0)
    sub3 = lax.broadcasted_iota(jnp.int32, (nv, V7X_SUBLANES, w), 1)

    def boundary_small(x3, p0, s):
        b = 1
        while b <= s:
            if p0 & b:
                x3 = jnp.where((sub3 & b) != 0, x3, pltpu.roll(x3, V7X_SUBLANES - b, axis=1))
            else:
                x3 = jnp.where((sub3 & b) != 0, pltpu.roll(x3, b, axis=1), x3)
            b *= 2
        return x3

    def gates(d, seg, t0):
        zf = seg[1][d][0, pl.ds(t0, c), :]
        lb = lb_ref[d:d + 1, :]
        f = lb + (1.0 - lb) * jax.nn.sigmoid(zf)
        causal = (rowi >= coli) if d == 0 else (rowi <= coli)
        cum = _dot01_left(jnp.where(causal, 1.0, 0.0).astype(BF16), jnp.log(f))
        return 1.0 - f, cum

    def boundary(cum, d, s):
        p0 = (s - 1) if d == 0 else s
        grp = 2 * s
        cum3 = cum.reshape(nv, V7X_SUBLANES, w)
        if grp <= V7X_SUBLANES:
            return boundary_small(cum3, p0, s).reshape(c, w)
        edge = V7X_SUBLANES - 1 if d == 0 else 0
        edge3 = jnp.broadcast_to(cum3[:, edge:edge + 1, :], cum3.shape)
        gv = grp // V7X_SUBLANES
        return jnp.concatenate(
            [jnp.broadcast_to(edge3[(gi * grp + p0) // V7X_SUBLANES:(gi * grp + p0) // V7X_SUBLANES + 1],
                              (gv, V7X_SUBLANES, w)) for gi in range(c // grp)], axis=0).reshape(c, w)

    def carried(d, seg, t0, qs, kk, cum, v, v32):
        o_ref = seg[4]
        last = c - 1 if d == 0 else 0
        cum_last = cum[last:last + 1, :]
        st = st_ref[d]
        qe = (qs * jnp.exp(cum)).astype(BF16)
        kw = (kk * jnp.exp(cum_last - cum)).astype(BF16)
        upd = jnp.dot(v32.T.astype(BF16), kw, preferred_element_type=F32)
        st_ref[d] = jnp.where(same_head, st * jnp.exp(cum_last) + upd, 0.0)
        o_ref[0, pl.ds(t0, c), :] = o_ref[0, pl.ds(t0, c), :] + _dot_nt(qe, st.astype(BF16))

    def step(seg, i, n_chunks):
        q_ref, _, v_ref, _, o_ref = seg
        t0 = pl.multiple_of(i * c, c)
        qs = _silu(q_ref[0, pl.ds(t0, c), :].astype(F32))
        v = v_ref[0, pl.ds(t0, c), :]
        v32 = v.astype(F32)
        kk0, cum0 = gates(0, seg, t0)
        kk1, cum1 = gates(1, seg, t0)

        out = jnp.dot((qs * (kk0 + kk1)).astype(BF16), bones, preferred_element_type=F32) * v32
        scores = [jnp.zeros((c, c), F32) for _ in range(C_HEADS)]
        s = 1
        while s < c:
            e0 = jnp.exp2(jnp.abs(cum0 - boundary(cum0, 0, s)) * NEG_LOG2E)
            e1 = jnp.exp2(jnp.abs(cum1 - boundary(cum1, 1, s)) * NEG_LOG2E)
            has_bit = (trow & s) != 0
            qm = (qs * jnp.where(has_bit, e0, e1)).astype(BF16)
            km = jnp.where(has_bit, kk1 * e1, kk0 * e0).astype(BF16)
            take = pair_bits >= s
            for h in range(C_HEADS):
                sl = slice(h * C_KEY, (h + 1) * C_KEY)
                scores[h] = jnp.where(take, _dot_nt(qm[:, sl], km[:, sl]), scores[h])
            s *= 2
        out = out + jnp.concatenate(
            [jnp.dot(scores[h].astype(BF16), v[:, h * C_KEY:(h + 1) * C_KEY], preferred_element_type=F32)
             for h in range(C_HEADS)], axis=1)
        o_ref[0, pl.ds(t0, c), :] = o_ref[0, pl.ds(t0, c), :] + out

        carried(0, seg, t0, qs, kk0, cum0, v, v32)
        tb = pl.multiple_of((n_chunks - 1 - i) * c, c)
        qsb = _silu(q_ref[0, pl.ds(tb, c), :].astype(F32))
        vb = v_ref[0, pl.ds(tb, c), :]
        kkb, cumb = gates(1, seg, tb)
        carried(1, seg, tb, qsb, kkb, cumb, vb, vb.astype(F32))

    for seg in segs:
        n_chunks = seg[4].shape[1] // c
        lax.fori_loop(0, n_chunks, lambda i, carry, seg=seg, n_chunks=n_chunks: (step(seg, i, n_chunks), carry)[1], 0,
                      unroll=math.gcd(n_chunks, SCAN_UNROLL))

    nw = nw_ref[...]
    for _, _, _, g_ref, o_ref in segs:
        def fin_body(i, carry, g_ref=g_ref, o_ref=o_ref):
            t0 = pl.multiple_of(i * c, c)
            o = o_ref[0, pl.ds(t0, c), :]
            ms = _dot01_right(o * o, bones) * (1.0 / C_KEY)
            y = o * lax.rsqrt(ms + EPS) * nw
            o_ref[0, pl.ds(t0, c), :] = y * _silu(g_ref[0, pl.ds(t0, c), :].astype(F32))
            return carry

        lax.fori_loop(0, o_ref.shape[1] // c, fin_body, 0)


def _hgrn(ctx_parts, lat_parts, lower, nw):
    qc, fc, ic, gc = ctx_parts
    ql, fl, il, gl = lat_parts
    bsz, n_ctx, w = qc.shape
    seq = ql.shape[1]
    half = lambda a, j: pl.BlockSpec((1, a.shape[1], w), lambda b: (b, 0, j))
    ins = (qc, ql, fc, fl, fc, fl, ic, il, gc, gl)
    specs = [_seq_spec(qc), _seq_spec(ql), half(fc, 0), half(fl, 0), half(fc, 1), half(fl, 1),
             _seq_spec(ic), _seq_spec(il), _seq_spec(gc), _seq_spec(gl)]
    est = 2 * (n_ctx + seq) * w * (2 + 8 + 2 + 2 + 4) + 20 * 1024 * 1024
    return pl.pallas_call(
        functools.partial(_hgrn_kernel, c=MIX_CHUNK),
        grid=(bsz,),
        in_specs=specs + [_full_spec(lower), _full_spec(nw)],
        out_specs=[pl.BlockSpec((1, n_ctx, w), lambda b: (b, 0, 0)), pl.BlockSpec((1, seq, w), lambda b: (b, 0, 0))],
        out_shape=[jax.ShapeDtypeStruct((bsz, n_ctx, w), F32), jax.ShapeDtypeStruct((bsz, seq, w), F32)],
        scratch_shapes=[pltpu.VMEM((2, w, w), F32)],
        compiler_params=pltpu.CompilerParams(dimension_semantics=("arbitrary",),
                                             vmem_limit_bytes=_vmem_limit(est)),
        name="hgrn_mixer",
    )(*ins, lower, nw)


def _s5_tables(lam_re, lam_im, log_step, b_re, b_im, c_re, c_im):
    q = S5_CHUNK
    hp = lax.Precision.HIGHEST
    lam_re = lam_re.astype(F32)
    lam_im = lam_im.astype(F32)
    step = jnp.exp(log_step.astype(F32))[..., None]
    tau = jnp.arange(q + 1, dtype=F32)[:, None, None, None]
    mag = jnp.exp(lam_re * step * tau)
    pr = mag * jnp.cos(lam_im * step * tau)
    pi = mag * jnp.sin(lam_im * step * tau)
    ar, ai = pr[1], pi[1]
    den = lam_re * lam_re + lam_im * lam_im
    nr = ar - 1.0
    kr = (nr * lam_re + ai * lam_im) / den
    ki = (ai * lam_re - nr * lam_im) / den
    b_re = b_re.astype(F32)
    b_im = b_im.astype(F32)
    br = kr[..., None] * b_re - ki[..., None] * b_im
    bi = kr[..., None] * b_im + ki[..., None] * b_re
    c_re = c_re.astype(F32)
    c_im = c_im.astype(F32)
    qr = pr[..., None] * br - pi[..., None] * bi
    qi = pr[..., None] * bi + pi[..., None] * br
    hk = jnp.einsum('dgon,tdgni->tdgoi', c_re, qr, precision=hp) - jnp.einsum('dgon,tdgni->tdgoi', c_im, qi, precision=hp)
    s_idx = jnp.arange(q)[:, None]
    t_idx = jnp.arange(q)[None, :]
    lag = jnp.stack([t_idx - s_idx, s_idx - t_idx])
    ok = lag >= 0
    lagc = jnp.clip(lag, 0, q)
    toe = jnp.stack([hk[lagc[d], d] for d in range(2)])
    toe = jnp.where(ok[:, :, :, None, None, None], toe, 0.0)
    toe = toe.transpose(0, 3, 1, 5, 2, 4)
    g = toe.shape[1]
    cg = toe.shape[3]
    toe = toe.reshape(2, g, q * cg, q * cg)
    tau_st = jnp.stack([q - 1 - jnp.arange(q), jnp.arange(q)])
    wst_r = jnp.stack([qr[tau_st[d], d] for d in range(2)])
    wst_i = jnp.stack([qi[tau_st[d], d] for d in range(2)])
    wst = jnp.concatenate([wst_r, wst_i], axis=3)
    wst = wst.transpose(0, 2, 1, 4, 3).reshape(2, g, q * cg, -1)
    tau_o = jnp.stack([jnp.arange(q) + 1, q - jnp.arange(q)])
    po_r = jnp.stack([pr[tau_o[d], d] for d in range(2)])
    po_i = jnp.stack([pi[tau_o[d], d] for d in range(2)])
    w_xr = c_re[:, None] * po_r[:, :, :, None, :] - c_im[:, None] * po_i[:, :, :, None, :]
    w_xi = -c_re[:, None] * po_i[:, :, :, None, :] - c_im[:, None] * po_r[:, :, :, None, :]
    wout = jnp.concatenate([w_xr, w_xi], axis=-1)
    wout = wout.transpose(0, 2, 4, 1, 3).reshape(2, g, -1, q * cg)
    dec = jnp.stack([pr[q], pi[q]], axis=1)
    ns = wst.shape[-1] // 2
    wst = jnp.stack([wst[..., :ns], wst[..., ns:]])
    wout = jnp.stack([wout[:, :, :ns, :], wout[:, :, ns:, :]])
    return toe.astype(BF16), wst.astype(BF16), wout.astype(BF16), dec


def _lane_block_transpose(arrs, blk):
    n = len(arrs)
    rows, width = arrs[0].shape
    lane_blk = lax.broadcasted_iota(jnp.int32, (rows, width), 1) >> int(math.log2(blk))
    k = 1
    while k < n:
        hi_half = (lane_blk & k) != 0
        new = list(arrs)
        for a in range(n):
            if a & k == 0:
                lo_arr, hi_arr = arrs[a], arrs[a | k]
                new[a] = jnp.where(hi_half, pltpu.roll(hi_arr, blk * k, axis=1), lo_arr)
                new[a | k] = jnp.where(hi_half, hi_arr, pltpu.roll(lo_arr, width - blk * k, axis=1))
        arrs = new
        k *= 2
    return arrs


def _gelu_tanh(x):
    return 0.5 * x * (1.0 + jnp.tanh(math.sqrt(2.0 / math.pi) * (x + 0.044715 * (x * x * x))))


def _s5_kernel(uc_ref, ul_ref, toe_ref, wst_ref, wout_ref, dec_ref, sd_ref, gw_ref, gb_ref,
               oc_ref, ol_ref, z_ref, xin_ref):
    q = uc_ref.shape[2]
    ncc = uc_ref.shape[1]
    n_chunks = ncc + ul_ref.shape[1]
    ng = toe_ref.shape[1]
    rows = [jnp.concatenate([uc_ref[0, :, s, :], ul_ref[0, :, s, :]], axis=0) for s in range(q)]
    ug = [a.astype(BF16) for a in _lane_block_transpose(rows, B_GROUP)]
    ys = [None] * ng
    for d in range(2):
        for g in range(ng):
            ys[g] = jnp.dot(ug[g], toe_ref[d, g], preferred_element_type=F32) + (0.0 if ys[g] is None else ys[g])
            for part in range(2):
                z_ref[part, d, :, g, :] = jnp.dot(ug[g], wst_ref[part, d, g], preferred_element_type=F32)

    ar = [dec_ref[d, 0] for d in range(2)]
    ai = [dec_ref[d, 1] for d in range(2)]

    def body(i, carry):
        nxt = []
        for d in range(2):
            if d == 0:
                c_idx = i
            else:
                c_idx = jnp.where(i < ncc, ncc - 1 - i, n_chunks - 1 - (i - ncc))
            xr, xi = carry[2 * d], carry[2 * d + 1]
            xin_ref[0, d, c_idx] = xr
            xin_ref[1, d, c_idx] = xi
            nxt.append(ar[d] * xr - ai[d] * xi + z_ref[0, d, c_idx])
            nxt.append(ar[d] * xi + ai[d] * xr + z_ref[1, d, c_idx])
        return tuple(nxt)

    zero = jnp.zeros(ar[0].shape, F32)
    lax.fori_loop(0, n_chunks, body, (zero, zero, zero, zero), unroll=2)
    for d in range(2):
        for g in range(ng):
            for part in range(2):
                ys[g] = ys[g] + jnp.dot(xin_ref[part, d, :, g, :].astype(BF16), wout_ref[part, d, g],
                                        preferred_element_type=F32)
    yt = _lane_block_transpose(ys, B_GROUP)
    sd = sd_ref[...]
    gw = gw_ref[...]
    gb = gb_ref[...]
    for t in range(q):
        y = _gelu_tanh(yt[t] + sd * rows[t])
        gate = jnp.dot(y.astype(BF16), gw, preferred_element_type=F32) + gb
        y = y * jax.nn.sigmoid(gate)
        oc_ref[0, :, t, :] = y[:ncc]
        ol_ref[0, :, t, :] = y[ncc:]


def _s5(pbc, pbl, toe, wst, wout, dec, s5_d, glu_w, glu_b):
    bsz, n_ctx, wd = pbc.shape
    seq = pbl.shape[1]
    q = S5_CHUNK
    ng = toe.shape[1]
    ns = wst.shape[-1]
    n_chunks = (n_ctx + seq) // q
    uc = pbc.reshape(bsz, n_ctx // q, q, wd)
    ul = pbl.reshape(bsz, seq // q, q, wd)
    consts = (toe, wst, wout, dec, s5_d, glu_w, glu_b)
    oc, ol = pl.pallas_call(
        _s5_kernel,
        grid=(bsz,),
        in_specs=[_seq_spec(uc), _seq_spec(ul)] + [_full_spec(a) for a in consts],
        out_specs=[_seq_spec(uc), _seq_spec(ul)],
        out_shape=[jax.ShapeDtypeStruct(uc.shape, F32), jax.ShapeDtypeStruct(ul.shape, F32)],
        scratch_shapes=[pltpu.VMEM((2, 2, n_chunks, ng, ns), F32), pltpu.VMEM((2, 2, n_chunks, ng, ns), F32)],
        compiler_params=pltpu.CompilerParams(dimension_semantics=("arbitrary",),
                                             vmem_limit_bytes=_vmem_limit(40 * 1024 * 1024)),
        name="s5_mixer",
    )(uc, ul, *consts)
    return oc.reshape(bsz, n_ctx, wd), ol.reshape(bsz, seq, wd)


def _raster_to_column(t, rows):
    b, s, d = t.shape
    return t.reshape(b, rows, GRID_W, d).transpose(0, 2, 1, 3).reshape(b, s, d)


def _column_to_raster(t, rows):
    b, s, d = t.shape
    return t.reshape(b, GRID_W, rows, d).transpose(0, 2, 1, 3).reshape(b, s, d)


def kernel(x, c, ctx, c_ctx, mod_w, mod_b, ffn_w_in, ffn_w_out, w_in, w_out, a_conv_w, a_conv_b, a_dt_bias, a_log,
           a_d, a_norm_w, s5_lam_re, s5_lam_im, s5_log_step, s5_b_re, s5_b_im, s5_c_re, s5_c_im, s5_d, s5_glu_w,
           s5_glu_b, hg_lb_logits, hg_norm_w, final_norm_w):
    bsz, seq, dm = x.shape
    n_ctx = ctx.shape[1]
    depth = mod_w.shape[0]
    rows = seq // GRID_W
    assert n_ctx % MIX_CHUNK == 0 and seq % MIX_CHUNK == 0 and seq % GRID_W == 0
    a_inner = A_HEADS * A_HEAD_DIM
    a_conv_dim = a_conv_w.shape[-1]
    c_width = hg_norm_w.shape[-1]

    pad = (-(bsz + 1)) % V7X_SUBLANES
    cvec = jnp.concatenate([c, c_ctx[None, :], jnp.zeros((pad, dm), F32)], axis=0)
    mod_all = _mod_vectors(cvec, mod_w, mod_b)

    p_lb = jax.nn.softmax(hg_lb_logits.astype(F32), axis=0)
    lower_bounds = jnp.cumsum(p_lb, axis=0) - p_lb[:1]
    fw = final_norm_w.reshape(1, dm).astype(F32)

    h_lat, h_ctx = x, ctx
    col_order = False
    for l in range(depth):
        last = l == depth - 1
        m_lat = mod_all[l, :bsz].reshape(bsz, N_MOD, dm)
        m_ctx = mod_all[l, bsz].reshape(1, N_MOD, dm)
        wi0, wo0 = ffn_w_in[l, 0].astype(BF16), ffn_w_out[l, 0].astype(BF16)
        wi1, wo1 = ffn_w_in[l, 1].astype(BF16), ffn_w_out[l, 1].astype(BF16)

        want_col = l % 2 == 1
        if col_order and not want_col:
            h_lat = _column_to_raster(h_lat, rows)
            col_order = False
        h_lat = _ffn(h_lat, m_lat, wi0, wo0, fw, base=0, final=False, to_column=want_col and not col_order)
        col_order = want_col
        h_ctx = _ffn(h_ctx, m_ctx, wi0, wo0, fw, base=0, final=False)

        wl = w_in[l]
        o_dt = a_inner + a_conv_dim
        o_b = o_dt + 2 * A_HEADS
        w_cat = jnp.concatenate([wl[:, :o_b], jnp.zeros((dm, V7X_LANES - 2 * A_HEADS), F32), wl[:, o_b:]],
                                axis=1).astype(BF16)
        zl, xl, dtl, pbl, ql, fl, il, gl = _inproj(h_lat, m_lat, w_cat)
        zc, xc, dtc, pbc, qc, fc, ic, gc = _inproj(h_ctx, m_ctx, w_cat)

        lane_pad = jnp.zeros((V7X_LANES - 2 * A_HEADS,), F32)
        dtb = jnp.concatenate([a_dt_bias[l].reshape(-1), lane_pad]).reshape(1, V7X_LANES)
        aneg = jnp.concatenate([-jnp.exp(a_log[l].astype(F32)).reshape(-1), lane_pad]).reshape(1, V7X_LANES)
        dsk = jnp.repeat(a_d[l].astype(F32), A_HEAD_DIM).reshape(1, a_inner)
        ya_c, ya_l = _ssd(zc, zl, xc, xl, dtc, dtl, a_conv_w[l], a_conv_b[l].reshape(1, -1), dtb, aneg, dsk,
                          a_norm_w[l].reshape(1, -1))

        toe, wst, wout, dec = _s5_tables(s5_lam_re[l], s5_lam_im[l], s5_log_step[l], s5_b_re[l], s5_b_im[l],
                                         s5_c_re[l], s5_c_im[l])
        yb_c, yb_l = _s5(pbc, pbl, toe, wst, wout, dec, s5_d[l].reshape(1, -1), s5_glu_w[l].astype(BF16),
                         s5_glu_b[l].reshape(1, -1))

        yc_c, yc_l = _hgrn((qc, fc, ic, gc), (ql, fl, il, gl), lower_bounds[l].astype(F32),
                           hg_norm_w[l].reshape(1, c_width))

        w_o = w_out[l].astype(BF16)
        h_lat = _ffn(h_lat, m_lat, wi1, wo1, fw, base=6, final=last, to_raster=last and col_order,
                     mixed=(ya_l, yb_l, yc_l, w_o))
        if last:
            col_order = False
        else:
            h_ctx = _ffn(h_ctx, m_ctx, wi1, wo1, fw, base=6, final=False, mixed=(ya_c, yb_c, yc_c, w_o))
    return h_lat
```

```python
import functools
import math

import jax
import jax.numpy as jnp
from jax import lax
from jax.experimental import pallas as pl
from jax.experimental.pallas import tpu as pltpu

F32 = jnp.float32
BF16 = jnp.bfloat16
EPS = 1e-6
NEG_LOG2E = -1.4426950408889634

V7X_VMEM_BYTES = 64 * 1024 * 1024
V7X_LANES = 128
V7X_SUBLANES = 8
V7X_MXU_DIM = 256

GRID_W = 64
N_MOD = 9
A_HEADS = 8
A_HEAD_DIM = 64
A_GROUPS = 2
A_STATE = 64
A_CONV = 5
B_GROUP = 16
B_STATE = 64
C_HEADS = 4
C_KEY = 64

TOKEN_TILES = (512, 384, 256)
MIX_CHUNK = 256
SCAN_UNROLL = 2
HG_SAMPLES_PER_STEP = 1
S5_CHUNK = 16
GRID_COLS_PER_TILE = 2 * V7X_SUBLANES


def _vmem_limit(estimate_bytes):
    return int(min(V7X_VMEM_BYTES - 6 * 1024 * 1024, max(estimate_bytes, 16 * 1024 * 1024)))


def _token_tile(n):
    for tm in TOKEN_TILES:
        if n % tm == 0:
            return tm
    raise ValueError(f"token count {n} has no supported tile")


def _silu(x):
    return x * jax.nn.sigmoid(x)


def _rms(x):
    return x * lax.rsqrt(jnp.mean(x * x, axis=-1, keepdims=True) + EPS)


def _split3(x):
    hi = x.astype(BF16)
    r = x - hi.astype(F32)
    mid = r.astype(BF16)
    lo = (r - mid.astype(F32)).astype(BF16)
    return hi, mid, lo


def _dot01_left(m01, x):
    hi, mid, lo = _split3(x)
    d = lambda a: jnp.dot(m01, a, preferred_element_type=F32)
    return (d(lo) + d(mid)) + d(hi)


def _dot01_right(x, m01):
    hi, mid, lo = _split3(x)
    d = lambda a: jnp.dot(a, m01, preferred_element_type=F32)
    return (d(lo) + d(mid)) + d(hi)


def _dot_nt(a, b):
    return lax.dot_general(a, b, (((1,), (1,)), ((), ())), preferred_element_type=F32)


def _mod_spec(mod):
    per_sample = mod.shape[0] > 1
    return pl.BlockSpec((1,) + mod.shape[1:], lambda b, t: (b if per_sample else 0, 0, 0))


def _mod_kernel(c_ref, w_ref, b_ref, o_ref):
    s = _silu(c_ref[...]).astype(BF16)
    o_ref[0] = jnp.dot(s, w_ref[0].astype(BF16), preferred_element_type=F32) + b_ref[0]


def _mod_vectors(cvec, mod_w, mod_b):
    depth, dm, nm = mod_w.shape
    rows = cvec.shape[0]
    tn = 1024
    return pl.pallas_call(
        _mod_kernel,
        grid=(depth, nm // tn),
        in_specs=[pl.BlockSpec((rows, dm), lambda l, j: (0, 0)),
                  pl.BlockSpec((1, dm, tn), lambda l, j: (l, 0, j)),
                  pl.BlockSpec((1, 1, tn), lambda l, j: (l, 0, j))],
        out_specs=pl.BlockSpec((1, rows, tn), lambda l, j: (l, 0, j)),
        out_shape=jax.ShapeDtypeStruct((depth, rows, nm), F32),
        compiler_params=pltpu.CompilerParams(dimension_semantics=("arbitrary", "arbitrary")),
        name="mod_vectors",
    )(cvec, mod_w, mod_b.reshape(depth, 1, nm))


def _ff_chunks(d_ff):
    if d_ff % V7X_MXU_DIM:
        return ((0, d_ff),)
    tiles = d_ff // V7X_MXU_DIM
    first = (tiles + 1) // 2 * V7X_MXU_DIM
    return ((0, first), (first, d_ff - first)) if d_ff > first else ((0, d_ff),)


def _ffn_kernel(h_ref, mod_ref, win_ref, wout_ref, fw_ref, *rest, base, d_ff, final, cols_in, cols_out, mixed):
    o_ref = rest[-1]
    if cols_in:
        x = jnp.concatenate([h_ref[0, :, wv, :] for wv in range(cols_in)], axis=0)
    else:
        x = h_ref[0]
    m = mod_ref[0]
    if mixed:
        ya_ref, yb_ref, yc_ref, wmix_ref = rest[:4]
        mix = jnp.concatenate([ya_ref[0], yb_ref[0], yc_ref[0]], axis=1).astype(BF16)
        x = x + m[base - 1:base] * jnp.dot(mix, wmix_ref[...], preferred_element_type=F32)
    u = (_rms(x) * (1.0 + m[base + 1:base + 2]) + m[base:base + 1]).astype(BF16)
    acc = jnp.zeros(x.shape, F32)
    for c0, cw in _ff_chunks(d_ff):
        g = jnp.dot(u, win_ref[:, c0:c0 + cw], preferred_element_type=F32)
        up = jnp.dot(u, win_ref[:, d_ff + c0:d_ff + c0 + cw], preferred_element_type=F32)
        a = (_silu(g) * up).astype(BF16)
        acc = acc + jnp.dot(a, wout_ref[c0:c0 + cw, :], preferred_element_type=F32)
    y = x + 0.5 * m[base + 2:base + 3] * acc
    if final:
        y = _rms(y) * fw_ref[...]
    if cols_out:
        nr = y.shape[0] // cols_out
        for wv in range(cols_out):
            o_ref[0, :, wv, :] = y[wv * nr:(wv + 1) * nr]
    else:
        o_ref[0] = y


def _ffn(h, mod, w_in, w_out, final_w, *, base, final, to_column=False, to_raster=False, mixed=None):
    bsz, lt, dm = h.shape
    d_ff = w_out.shape[0]
    grid_rows = lt // GRID_W
    cols = GRID_COLS_PER_TILE
    tm = grid_rows * cols if (to_column or to_raster) else _token_tile(lt)
    tok = lambda wd: pl.BlockSpec((1, tm, wd), lambda b, t: (b, t, 0))
    const = lambda a: pl.BlockSpec(a.shape, lambda b, t: (0, 0), pipeline_mode=pl.Buffered(1))
    grid_blk = pl.BlockSpec((1, grid_rows, cols, dm), lambda b, t: (b, 0, t, 0))
    kern = functools.partial(_ffn_kernel, base=base, d_ff=d_ff, final=final, mixed=mixed is not None,
                             cols_in=cols if to_column else 0, cols_out=cols if to_raster else 0)
    extra, extra_specs = (), []
    if mixed is not None:
        extra = tuple(mixed)
        extra_specs = [tok(a.shape[-1]) for a in mixed[:3]] + [const(mixed[3])]
    w_bytes = (w_in.size + w_out.size + (mixed[3].size if mixed is not None else 0)) * 2
    est = w_bytes + 8 * tm * dm * 4 + 3 * tm * d_ff * 4 + 6 * 1024 * 1024
    out = pl.pallas_call(
        kern,
        grid=(bsz, lt // tm),
        in_specs=[grid_blk if to_column else tok(dm), _mod_spec(mod), const(w_in), const(w_out),
                  pl.BlockSpec((1, dm), lambda b, t: (0, 0))] + extra_specs,
        out_specs=grid_blk if to_raster else tok(dm),
        out_shape=jax.ShapeDtypeStruct((bsz, grid_rows, GRID_W, dm) if to_raster else (bsz, lt, dm), F32),
        compiler_params=pltpu.CompilerParams(dimension_semantics=("arbitrary", "arbitrary"),
                                             vmem_limit_bytes=_vmem_limit(est)),
        name="ffn",
    )(h.reshape(bsz, grid_rows, GRID_W, dm) if to_column else h, mod, w_in, w_out, final_w, *extra)
    return out.reshape(bsz, lt, dm)


IN_SPLIT = ((512, BF16), (768, F32), (128, F32), (256, F32), (256, BF16), (512, F32), (256, BF16), (256, BF16))


def _inproj_kernel(h_ref, mod_ref, w_ref, *out_refs):
    m = mod_ref[0]
    u = (_rms(h_ref[0]) * (1.0 + m[4:5]) + m[3:4]).astype(BF16)
    p = jnp.dot(u, w_ref[...], preferred_element_type=F32)
    off = 0
    for ref, (wd, dt) in zip(out_refs, IN_SPLIT):
        ref[0] = p[:, off:off + wd].astype(dt)
        off += wd


def _inproj(h, mod, w):
    bsz, lt, dm = h.shape
    tm = _token_tile(lt)
    tok = lambda wd: pl.BlockSpec((1, tm, wd), lambda b, t: (b, t, 0))
    return pl.pallas_call(
        _inproj_kernel,
        grid=(bsz, lt // tm),
        in_specs=[tok(dm), _mod_spec(mod),
                  pl.BlockSpec(w.shape, lambda b, t: (0, 0), pipeline_mode=pl.Buffered(1))],
        out_specs=[tok(wd) for wd, _ in IN_SPLIT],
        out_shape=[jax.ShapeDtypeStruct((bsz, lt, wd), dt) for wd, dt in IN_SPLIT],
        compiler_params=pltpu.CompilerParams(dimension_semantics=("arbitrary", "arbitrary"),
                                             vmem_limit_bytes=_vmem_limit(40 * 1024 * 1024)),
        name="inproj",
    )(h, mod, w)


def _seq_spec(a):
    return pl.BlockSpec((1,) + a.shape[1:], lambda b: (b,) + (0,) * (a.ndim - 1))


def _full_spec(a):
    return pl.BlockSpec(a.shape, lambda b: (0,) * a.ndim)


def _ssd_kernel(zc_ref, zl_ref, xc_in_ref, xl_in_ref, dtc_ref, dtl_ref, cw_ref, cb_ref, dtb_ref, aneg_ref,
                dsk_ref, nw_ref, oc_ref, ol_ref, xconv_ref, yb_ref, st_ref, *, q):
    inner = A_HEADS * A_HEAD_DIM
    gw = A_STATE
    hpg = A_HEADS // A_GROUPS
    gl = hpg * A_HEAD_DIM
    n_ctx = oc_ref.shape[1]
    segs = ((zc_ref, xc_in_ref, dtc_ref, oc_ref, 0), (zl_ref, xl_in_ref, dtl_ref, ol_ref, n_ctx))

    cw = cw_ref[...]
    cb = cb_ref[...]
    for _, xin_ref, _, o_ref, base in segs:
        ls = o_ref.shape[1]
        n_chunks = ls // q

        def conv_body(i, carry, xin_ref=xin_ref, base=base, ls=ls, n_chunks=n_chunks):
            t0 = pl.multiple_of(i * q, q)
            cur = xin_ref[0, pl.ds(t0, q), :]
            p0 = pl.multiple_of(jnp.maximum(t0 - V7X_SUBLANES, 0), V7X_SUBLANES)
            n0 = pl.multiple_of(jnp.minimum(t0 + q, ls - V7X_SUBLANES), V7X_SUBLANES)
            prev = jnp.where(i == 0, 0.0, xin_ref[0, pl.ds(p0, V7X_SUBLANES), :])
            nxt = jnp.where(i == n_chunks - 1, 0.0, xin_ref[0, pl.ds(n0, V7X_SUBLANES), :])
            ext = jnp.concatenate([prev, cur, nxt], axis=0)
            acc = cb
            for k in range(A_CONV):
                s0 = V7X_SUBLANES - A_CONV // 2 + k
                acc = acc + cw[k:k + 1] * ext[s0:s0 + q]
            xconv_ref[pl.ds(pl.multiple_of(base + t0, q), q), :] = _silu(acc)
            return carry

        lax.fori_loop(0, n_chunks, conv_body, 0)

    st_ref[...] = jnp.zeros(st_ref.shape, F32)

    rowi = lax.broadcasted_iota(jnp.int32, (q, q), 0)
    coli = lax.broadcasted_iota(jnp.int32, (q, q), 1)
    er = lax.broadcasted_iota(jnp.int32, (V7X_LANES, inner), 0)
    ec = lax.broadcasted_iota(jnp.int32, (V7X_LANES, inner), 1)
    dtb = dtb_ref[...]
    aneg = aneg_ref[...]

    def load_chunk(seg, t0):
        dt_ref, base = seg[2], seg[4]
        xc = xconv_ref[pl.ds(pl.multiple_of(base + t0, q), q), :]
        dt = jax.nn.softplus(dt_ref[0, pl.ds(t0, q), :] + dtb)
        return xc[:, :inner], xc[:, inner:inner + A_GROUPS * gw], xc[:, inner + A_GROUPS * gw:], dt

    def scan_sum(d, d_a):
        tri = (rowi >= coli) if d == 0 else (rowi <= coli)
        return _dot01_left(jnp.where(tri, 1.0, 0.0).astype(BF16), d_a)

    def carried(d, xs, bm, cm, dt, cum):
        last = q - 1 if d == 0 else 0
        expand = jnp.where(er == d * A_HEADS + (ec >> int(math.log2(A_HEAD_DIM))), 1.0, 0.0).astype(BF16)
        e_hi, e_mid, _ = _split3(jnp.exp(cum))
        ecum = (jnp.dot(e_mid, expand, preferred_element_type=F32) + jnp.dot(e_hi, expand, preferred_element_type=F32))
        wcol = (jnp.exp(cum[last:last + 1, :] - cum) * dt).astype(BF16)
        xw = (xs * jnp.dot(wcol, expand, preferred_element_type=F32)).astype(BF16)
        e_last = ecum[last:last + 1, :]
        bm_t = bm.T
        ys = []
        for g in range(A_GROUPS):
            cg = cm[:, g * gw:(g + 1) * gw].astype(BF16)
            st = st_ref[d, g]
            ys.append(jnp.dot(cg, st.astype(BF16), preferred_element_type=F32) * ecum[:, g * gl:(g + 1) * gl])
            upd = jnp.dot(bm_t[g * gw:(g + 1) * gw].astype(BF16), xw[:, g * gl:(g + 1) * gl],
                          preferred_element_type=F32)
            st_ref[d, g] = st * e_last[:, g * gl:(g + 1) * gl] + upd
        return jnp.concatenate(ys, axis=1)

    lower = rowi > coli
    diag = rowi == coli

    def step(seg, i, n_chunks):
        o_ref = seg[3]
        t0 = pl.multiple_of(i * q, q)
        xs, bm, cm, dt = load_chunk(seg, t0)
        d_a = dt * aneg
        cum_f = scan_sum(0, d_a)
        cum_b = scan_sum(1, d_a)
        log_dt = jnp.log(dt)
        col_f, col_b = cum_f, cum_b
        row_f = (cum_f - log_dt).T
        row_b = (cum_b - log_dt).T
        dt_t = dt.T
        xs_bf = xs.astype(BF16)
        ys = []
        for g in range(A_GROUPS):
            cg = cm[:, g * gw:(g + 1) * gw].astype(BF16)
            bg = bm[:, g * gw:(g + 1) * gw].astype(BF16)
            gmat = _dot_nt(cg, bg)
            for hh in range(hpg):
                h = g * hpg + hh
                hb = A_HEADS + h
                arg = jnp.where(lower, col_f[:, h:h + 1] - row_f[h:h + 1, :], col_b[:, hb:hb + 1] - row_b[hb:hb + 1, :])
                wgt = jnp.where(diag, dt_t[h:h + 1, :] + dt_t[hb:hb + 1, :], jnp.exp(arg))
                sc = (gmat * wgt).astype(BF16)
                ys.append(jnp.dot(sc, xs_bf[:, h * A_HEAD_DIM:(h + 1) * A_HEAD_DIM], preferred_element_type=F32))
        o_ref[0, pl.ds(t0, q), :] = jnp.concatenate(ys, axis=1) + carried(0, xs, bm, cm, dt, cum_f)
        tb = pl.multiple_of((n_chunks - 1 - i) * q, q)
        xsb, bmb, cmb, dtb_ = load_chunk(seg, tb)
        yb_ref[pl.ds(pl.multiple_of(seg[4] + tb, q), q), :] = carried(1, xsb, bmb, cmb, dtb_, scan_sum(1, dtb_ * aneg))

    dsk = dsk_ref[...]
    nw = nw_ref[...]
    for seg in segs:
        n_chunks = seg[3].shape[1] // q
        lax.fori_loop(0, n_chunks, lambda i, carry, seg=seg, n_chunks=n_chunks: (step(seg, i, n_chunks), carry)[1], 0,
                      unroll=math.gcd(n_chunks, SCAN_UNROLL))

    for z_ref, _, _, o_ref, base in segs:
        def fin_body(i, carry, z_ref=z_ref, o_ref=o_ref, base=base):
            t0 = pl.multiple_of(i * q, q)
            xs = xconv_ref[pl.ds(pl.multiple_of(base + t0, q), q), :inner]
            y = o_ref[0, pl.ds(t0, q), :] + yb_ref[pl.ds(pl.multiple_of(base + t0, q), q), :] + dsk * xs
            y = y * _silu(z_ref[0, pl.ds(t0, q), :].astype(F32))
            o_ref[0, pl.ds(t0, q), :] = _rms(y) * nw
            return carry

        lax.fori_loop(0, o_ref.shape[1] // q, fin_body, 0)


def _ssd(zc, zl, xc, xl, dtc, dtl, cw, cb, dtb, aneg, dsk, nw):
    bsz, n_ctx, inner = zc.shape
    seq = zl.shape[1]
    lt = n_ctx + seq
    cd = xc.shape[-1]
    ins = (zc, zl, xc, xl, dtc, dtl)
    consts = (cw, cb, dtb, aneg, dsk, nw)
    est = 2 * lt * (2 * inner + 4 * cd + 4 * V7X_LANES + 4 * inner) + 4 * lt * (cd + inner) + 12 * 1024 * 1024
    return pl.pallas_call(
        functools.partial(_ssd_kernel, q=MIX_CHUNK),
        grid=(bsz,),
        in_specs=[_seq_spec(a) for a in ins] + [_full_spec(a) for a in consts],
        out_specs=[pl.BlockSpec((1, n_ctx, inner), lambda b: (b, 0, 0)),
                   pl.BlockSpec((1, seq, inner), lambda b: (b, 0, 0))],
        out_shape=[jax.ShapeDtypeStruct((bsz, n_ctx, inner), F32), jax.ShapeDtypeStruct((bsz, seq, inner), F32)],
        scratch_shapes=[pltpu.VMEM((lt, cd), F32),
                        pltpu.VMEM((lt, inner), F32),
                        pltpu.VMEM((2, A_GROUPS, A_STATE, inner // A_GROUPS), F32)],
        compiler_params=pltpu.CompilerParams(dimension_semantics=("arbitrary",),
                                             vmem_limit_bytes=_vmem_limit(est)),
        name="ssd_mixer",
    )(*ins, *consts)


def _hgrn_kernel(qc_ref, ql_ref, ffc_ref, ffl_ref, fbc_ref, fbl_ref, vc_ref, vl_ref, gc_ref, gl_ref,
                 lb_ref, nw_ref, oc_ref, ol_ref, ob_ref, st_ref, *, c):
    w = C_HEADS * C_KEY
    nv = c // V7X_SUBLANES
    segs = ((qc_ref, (ffc_ref, fbc_ref), vc_ref, gc_ref, oc_ref, 0),
            (ql_ref, (ffl_ref, fbl_ref), vl_ref, gl_ref, ol_ref, oc_ref.shape[1]))
    st_ref[...] = jnp.zeros(st_ref.shape, F32)

    rowi = lax.broadcasted_iota(jnp.int32, (c, c), 0)
    coli = lax.broadcasted_iota(jnp.int32, (c, c), 1)
    pair_bits = rowi ^ coli
    hr = lax.broadcasted_iota(jnp.int32, (w, w), 0)
    hc = lax.broadcasted_iota(jnp.int32, (w, w), 1)
    same_head = (hr >> int(math.log2(C_KEY))) == (hc >> int(math.log2(C_KEY)))
    bones = jnp.where(same_head, 1.0, 0.0).astype(BF16)
    trow = lax.broadcasted_iota(jnp.int32, (c, w), 0)
    sub3 = lax.broadcasted_iota(jnp.int32, (nv, V7X_SUBLANES, w), 1)

    def boundary_small(x3, p0, s):
        b = 1
        while b <= s:
            if p0 & b:
                x3 = jnp.where((sub3 & b) != 0, x3, pltpu.roll(x3, V7X_SUBLANES - b, axis=1))
            else:
                x3 = jnp.where((sub3 & b) != 0, pltpu.roll(x3, b, axis=1), x3)
            b *= 2
        return x3

    def gates(d, seg, smp, t0):
        zf = seg[1][d][smp, pl.ds(t0, c), :]
        lb = lb_ref[d:d + 1, :]
        f = lb + (1.0 - lb) * jax.nn.sigmoid(zf)
        causal = (rowi >= coli) if d == 0 else (rowi <= coli)
        cum = _dot01_left(jnp.where(causal, 1.0, 0.0).astype(BF16), jnp.log(f))
        return 1.0 - f, cum

    def boundary(cum, d, s):
        p0 = (s - 1) if d == 0 else s
        grp = 2 * s
        cum3 = cum.reshape(nv, V7X_SUBLANES, w)
        if grp <= V7X_SUBLANES:
            return boundary_small(cum3, p0, s).reshape(c, w)
        edge = V7X_SUBLANES - 1 if d == 0 else 0
        edge3 = jnp.broadcast_to(cum3[:, edge:edge + 1, :], cum3.shape)
        gv = grp // V7X_SUBLANES
        return jnp.concatenate(
            [jnp.broadcast_to(edge3[(gi * grp + p0) // V7X_SUBLANES:(gi * grp + p0) // V7X_SUBLANES + 1],
                              (gv, V7X_SUBLANES, w)) for gi in range(c // grp)], axis=0).reshape(c, w)

    def carried(d, smp, qs, kk, cum, v32):
        last = c - 1 if d == 0 else 0
        cum_last = cum[last:last + 1, :]
        st = st_ref[smp, d]
        qe = (qs * jnp.exp(cum)).astype(BF16)
        kw = (kk * jnp.exp(cum_last - cum)).astype(BF16)
        upd = jnp.dot(v32.T.astype(BF16), kw, preferred_element_type=F32)
        st_ref[smp, d] = jnp.where(same_head, st * jnp.exp(cum_last) + upd, 0.0)
        return _dot_nt(qe, st.astype(BF16))

    def step(seg, smp, i, n_chunks):
        q_ref, _, v_ref, _, o_ref, base = seg
        t0 = pl.multiple_of(i * c, c)
        qs = _silu(q_ref[smp, pl.ds(t0, c), :].astype(F32))
        v = v_ref[smp, pl.ds(t0, c), :]
        v32 = v.astype(F32)
        kk0, cum0 = gates(0, seg, smp, t0)
        kk1, cum1 = gates(1, seg, smp, t0)

        out = jnp.dot((qs * (kk0 + kk1)).astype(BF16), bones, preferred_element_type=F32) * v32
        scores = [jnp.zeros((c, c), F32) for _ in range(C_HEADS)]
        s = 1
        while s < c:
            e0 = jnp.exp2(jnp.abs(cum0 - boundary(cum0, 0, s)) * NEG_LOG2E)
            e1 = jnp.exp2(jnp.abs(cum1 - boundary(cum1, 1, s)) * NEG_LOG2E)
            has_bit = (trow & s) != 0
            qm = (qs * jnp.where(has_bit, e0, e1)).astype(BF16)
            km = jnp.where(has_bit, kk1 * e1, kk0 * e0).astype(BF16)
            take = pair_bits >= s
            for h in range(C_HEADS):
                sl = slice(h * C_KEY, (h + 1) * C_KEY)
                scores[h] = jnp.where(take, _dot_nt(qm[:, sl], km[:, sl]), scores[h])
            s *= 2
        out = out + jnp.concatenate(
            [jnp.dot(scores[h].astype(BF16), v[:, h * C_KEY:(h + 1) * C_KEY], preferred_element_type=F32)
             for h in range(C_HEADS)], axis=1)
        o_ref[smp, pl.ds(t0, c), :] = out + carried(0, smp, qs, kk0, cum0, v32)
        tb = pl.multiple_of((n_chunks - 1 - i) * c, c)
        qsb = _silu(q_ref[smp, pl.ds(tb, c), :].astype(F32))
        kkb, cumb = gates(1, seg, smp, tb)
        ob_ref[smp, pl.ds(pl.multiple_of(base + tb, c), c), :] = carried(
            1, smp, qsb, kkb, cumb, v_ref[smp, pl.ds(tb, c), :].astype(F32))

    n_smp = oc_ref.shape[0]
    for seg in segs:
        n_chunks = seg[4].shape[1] // c

        def scan_body(i, carry, seg=seg, n_chunks=n_chunks):
            for smp in range(n_smp):
                step(seg, smp, i, n_chunks)
            return carry

        lax.fori_loop(0, n_chunks, scan_body, 0, unroll=math.gcd(n_chunks, SCAN_UNROLL))

    nw = nw_ref[...]
    for _, _, _, g_ref, o_ref, base in segs:
        def fin_body(i, carry, g_ref=g_ref, o_ref=o_ref, base=base):
            t0 = pl.multiple_of(i * c, c)
            for smp in range(n_smp):
                o = o_ref[smp, pl.ds(t0, c), :] + ob_ref[smp, pl.ds(pl.multiple_of(base + t0, c), c), :]
                ms = _dot01_right(o * o, bones) * (1.0 / C_KEY)
                y = o * lax.rsqrt(ms + EPS) * nw
                o_ref[smp, pl.ds(t0, c), :] = y * _silu(g_ref[smp, pl.ds(t0, c), :].astype(F32))
            return carry

        lax.fori_loop(0, o_ref.shape[1] // c, fin_body, 0)


def _hgrn(ctx_parts, lat_parts, lower, nw):
    qc, fc, ic, gc = ctx_parts
    ql, fl, il, gl = lat_parts
    bsz, n_ctx, w = qc.shape
    seq = ql.shape[1]
    ns = math.gcd(bsz, HG_SAMPLES_PER_STEP)
    blk = lambda a, j: pl.BlockSpec((ns, a.shape[1], w), lambda b: (b, 0, j))
    ins = (qc, ql, fc, fl, fc, fl, ic, il, gc, gl)
    specs = [blk(qc, 0), blk(ql, 0), blk(fc, 0), blk(fl, 0), blk(fc, 1), blk(fl, 1),
             blk(ic, 0), blk(il, 0), blk(gc, 0), blk(gl, 0)]
    est = ns * 2 * (n_ctx + seq) * w * (2 + 8 + 2 + 2 + 4) + 20 * 1024 * 1024
    return pl.pallas_call(
        functools.partial(_hgrn_kernel, c=MIX_CHUNK),
        grid=(bsz // ns,),
        in_specs=specs + [_full_spec(lower), _full_spec(nw)],
        out_specs=[blk(qc, 0), blk(ql, 0)],
        out_shape=[jax.ShapeDtypeStruct((bsz, n_ctx, w), F32), jax.ShapeDtypeStruct((bsz, seq, w), F32)],
        scratch_shapes=[pltpu.VMEM((ns, n_ctx + seq, w), F32), pltpu.VMEM((ns, 2, w, w), F32)],
        compiler_params=pltpu.CompilerParams(dimension_semantics=("arbitrary",),
                                             vmem_limit_bytes=_vmem_limit(est)),
        name="hgrn_mixer",
    )(*ins, lower, nw)


def _s5_tables(lam_re, lam_im, log_step, b_re, b_im, c_re, c_im):
    q = S5_CHUNK
    hp = lax.Precision.HIGHEST
    lam_re = lam_re.astype(F32)
    lam_im = lam_im.astype(F32)
    step = jnp.exp(log_step.astype(F32))[..., None]
    tau = jnp.arange(q + 1, dtype=F32)[:, None, None, None]
    mag = jnp.exp(lam_re * step * tau)
    pr = mag * jnp.cos(lam_im * step * tau)
    pi = mag * jnp.sin(lam_im * step * tau)
    ar, ai = pr[1], pi[1]
    den = lam_re * lam_re + lam_im * lam_im
    nr = ar - 1.0
    kr = (nr * lam_re + ai * lam_im) / den
    ki = (ai * lam_re - nr * lam_im) / den
    b_re = b_re.astype(F32)
    b_im = b_im.astype(F32)
    br = kr[..., None] * b_re - ki[..., None] * b_im
    bi = kr[..., None] * b_im + ki[..., None] * b_re
    c_re = c_re.astype(F32)
    c_im = c_im.astype(F32)
    qr = pr[..., None] * br - pi[..., None] * bi
    qi = pr[..., None] * bi + pi[..., None] * br
    hk = jnp.einsum('dgon,tdgni->tdgoi', c_re, qr, precision=hp) - jnp.einsum('dgon,tdgni->tdgoi', c_im, qi, precision=hp)
    g, cg = b_re.shape[1], b_re.shape[3]
    lags = jnp.arange(q + 1)
    pos = jnp.arange(q)
    onehot = lambda idx: (idx[..., None] == lags).astype(F32)
    lag = jnp.stack([pos[None, :] - pos[:, None], pos[:, None] - pos[None, :]])
    toe = jnp.einsum('dstl,ldgoi->dgsito', onehot(lag), hk, precision=hp).reshape(2, g, q * cg, q * cg)
    m_st = onehot(jnp.stack([q - 1 - pos, pos]))
    wst = jnp.concatenate([jnp.einsum('dsl,ldgni->dgsin', m_st, qr, precision=hp),
                           jnp.einsum('dsl,ldgni->dgsin', m_st, qi, precision=hp)], axis=-1)
    wst = wst.reshape(2, g, q * cg, -1)
    m_o = onehot(jnp.stack([pos + 1, q - pos]))
    po_r = jnp.einsum('dtl,ldgn->dtgn', m_o, pr, precision=hp)
    po_i = jnp.einsum('dtl,ldgn->dtgn', m_o, pi, precision=hp)
    ein = lambda cc, pp: jnp.einsum('dgon,dtgn->dgnto', cc, pp, precision=hp)
    wout = jnp.concatenate([ein(c_re, po_r) - ein(c_im, po_i), -ein(c_re, po_i) - ein(c_im, po_r)], axis=2)
    wout = wout.reshape(2, g, -1, q * cg)
    dec = jnp.stack([jnp.concatenate([pr[q], pr[q]], axis=-1),
                     jnp.concatenate([-pi[q], pi[q]], axis=-1),
                     jnp.concatenate([pi[q], -pi[q]], axis=-1)], axis=1)
    return toe.astype(BF16), wst.astype(BF16), wout.astype(BF16), dec


def _lane_block_transpose(arrs, blk):
    n = len(arrs)
    rows, width = arrs[0].shape
    lane_blk = lax.broadcasted_iota(jnp.int32, (rows, width), 1) >> int(math.log2(blk))
    k = 1
    while k < n:
        hi_half = (lane_blk & k) != 0
        new = list(arrs)
        for a in range(n):
            if a & k == 0:
                lo_arr, hi_arr = arrs[a], arrs[a | k]
                new[a] = jnp.where(hi_half, pltpu.roll(hi_arr, blk * k, axis=1), lo_arr)
                new[a | k] = jnp.where(hi_half, hi_arr, pltpu.roll(lo_arr, width - blk * k, axis=1))
        arrs = new
        k *= 2
    return arrs


def _gelu_tanh(x):
    return 0.5 * x * (1.0 + jnp.tanh(math.sqrt(2.0 / math.pi) * (x + 0.044715 * (x * x * x))))


def _s5_kernel(uc_ref, ul_ref, toe_ref, wst_ref, wout_ref, dec_ref, sd_ref, gw_ref, gb_ref,
               oc_ref, ol_ref, z_ref, xin_ref):
    q = uc_ref.shape[2]
    ncc = uc_ref.shape[1]
    n_chunks = ncc + ul_ref.shape[1]
    ng = toe_ref.shape[1]
    half = dec_ref.shape[-1] // 2
    rows = [jnp.concatenate([uc_ref[0, :, s, :], ul_ref[0, :, s, :]], axis=0) for s in range(q)]
    ug = [a.astype(BF16) for a in _lane_block_transpose(rows, B_GROUP)]
    ys = [None] * ng
    for d in range(2):
        zs = []
        for g in range(ng):
            ys[g] = jnp.dot(ug[g], toe_ref[d, g], preferred_element_type=F32) + (0.0 if ys[g] is None else ys[g])
            zs.append(jnp.dot(ug[g], wst_ref[d, g], preferred_element_type=F32))
        z = pltpu.einshape("gcn->cgn", jnp.stack(zs))
        z_ref[0, d] = z
        z_ref[1, d] = pltpu.roll(z, half, axis=2)

    coef = [[dec_ref[d, j] for j in range(3)] for d in range(2)]

    def body(i, carry):
        nxt = []
        for d in range(2):
            if d == 0:
                c_idx = i
            else:
                c_idx = jnp.where(i < ncc, ncc - 1 - i, n_chunks - 1 - (i - ncc))
            x, xs = carry[2 * d], carry[2 * d + 1]
            a1, a2, a2s = coef[d]
            xin_ref[d, c_idx] = x
            nxt.append(a1 * x + a2 * xs + z_ref[0, d, c_idx])
            nxt.append(a1 * xs + a2s * x + z_ref[1, d, c_idx])
        return tuple(nxt)

    zero = jnp.zeros(coef[0][0].shape, F32)
    lax.fori_loop(0, n_chunks, body, (zero, zero, zero, zero), unroll=2)
    for d in range(2):
        xg = pltpu.einshape("cgn->gcn", xin_ref[d]).astype(BF16)
        for g in range(ng):
            ys[g] = ys[g] + jnp.dot(xg[g], wout_ref[d, g], preferred_element_type=F32)
    yt = _lane_block_transpose(ys, B_GROUP)
    sd = sd_ref[...]
    gw = gw_ref[...]
    gb = gb_ref[...]
    for t in range(q):
        y = _gelu_tanh(yt[t] + sd * rows[t])
        gate = jnp.dot(y.astype(BF16), gw, preferred_element_type=F32) + gb
        y = y * jax.nn.sigmoid(gate)
        oc_ref[0, :, t, :] = y[:ncc]
        ol_ref[0, :, t, :] = y[ncc:]


def _s5(pbc, pbl, toe, wst, wout, dec, s5_d, glu_w, glu_b):
    bsz, n_ctx, wd = pbc.shape
    seq = pbl.shape[1]
    q = S5_CHUNK
    ng = toe.shape[1]
    ns = wst.shape[-1]
    n_chunks = (n_ctx + seq) // q
    uc = pbc.reshape(bsz, n_ctx // q, q, wd)
    ul = pbl.reshape(bsz, seq // q, q, wd)
    consts = (toe, wst, wout, dec, s5_d, glu_w, glu_b)
    oc, ol = pl.pallas_call(
        _s5_kernel,
        grid=(bsz,),
        in_specs=[_seq_spec(uc), _seq_spec(ul)] + [_full_spec(a) for a in consts],
        out_specs=[_seq_spec(uc), _seq_spec(ul)],
        out_shape=[jax.ShapeDtypeStruct(uc.shape, F32), jax.ShapeDtypeStruct(ul.shape, F32)],
        scratch_shapes=[pltpu.VMEM((2, 2, n_chunks, ng, ns), F32), pltpu.VMEM((2, n_chunks, ng, ns), F32)],
        compiler_params=pltpu.CompilerParams(dimension_semantics=("arbitrary",),
                                             vmem_limit_bytes=_vmem_limit(40 * 1024 * 1024)),
        name="s5_mixer",
    )(uc, ul, *consts)
    return oc.reshape(bsz, n_ctx, wd), ol.reshape(bsz, seq, wd)


def _raster_to_column(t, rows):
    b, s, d = t.shape
    return t.reshape(b, rows, GRID_W, d).transpose(0, 2, 1, 3).reshape(b, s, d)


def _column_to_raster(t, rows):
    b, s, d = t.shape
    return t.reshape(b, GRID_W, rows, d).transpose(0, 2, 1, 3).reshape(b, s, d)


def kernel(x, c, ctx, c_ctx, mod_w, mod_b, ffn_w_in, ffn_w_out, w_in, w_out, a_conv_w, a_conv_b, a_dt_bias, a_log,
           a_d, a_norm_w, s5_lam_re, s5_lam_im, s5_log_step, s5_b_re, s5_b_im, s5_c_re, s5_c_im, s5_d, s5_glu_w,
           s5_glu_b, hg_lb_logits, hg_norm_w, final_norm_w):
    bsz, seq, dm = x.shape
    n_ctx = ctx.shape[1]
    depth = mod_w.shape[0]
    rows = seq // GRID_W
    assert n_ctx % MIX_CHUNK == 0 and seq % MIX_CHUNK == 0 and seq % GRID_W == 0
    a_inner = A_HEADS * A_HEAD_DIM
    a_conv_dim = a_conv_w.shape[-1]
    c_width = hg_norm_w.shape[-1]

    pad = (-(bsz + 1)) % V7X_SUBLANES
    cvec = jnp.concatenate([c, c_ctx[None, :], jnp.zeros((pad, dm), F32)], axis=0)
    mod_all = _mod_vectors(cvec, mod_w, mod_b)

    p_lb = jax.nn.softmax(hg_lb_logits.astype(F32), axis=0)
    lower_bounds = jnp.cumsum(p_lb, axis=0) - p_lb[:1]
    fw = final_norm_w.reshape(1, dm).astype(F32)
    toe_all, wst_all, wout_all, dec_all = jax.vmap(_s5_tables)(s5_lam_re, s5_lam_im, s5_log_step, s5_b_re, s5_b_im,
                                                               s5_c_re, s5_c_im)

    h_lat, h_ctx = x, ctx
    col_order = False
    for l in range(depth):
        last = l == depth - 1
        m_lat = mod_all[l, :bsz].reshape(bsz, N_MOD, dm)
        m_ctx = mod_all[l, bsz].reshape(1, N_MOD, dm)
        wi0, wo0 = ffn_w_in[l, 0].astype(BF16), ffn_w_out[l, 0].astype(BF16)
        wi1, wo1 = ffn_w_in[l, 1].astype(BF16), ffn_w_out[l, 1].astype(BF16)

        want_col = l % 2 == 1
        if col_order and not want_col:
            h_lat = _column_to_raster(h_lat, rows)
            col_order = False
        h_lat = _ffn(h_lat, m_lat, wi0, wo0, fw, base=0, final=False, to_column=want_col and not col_order)
        col_order = want_col
        h_ctx = _ffn(h_ctx, m_ctx, wi0, wo0, fw, base=0, final=False)

        wl = w_in[l]
        o_dt = a_inner + a_conv_dim
        o_b = o_dt + 2 * A_HEADS
        w_cat = jnp.concatenate([wl[:, :o_b], jnp.zeros((dm, V7X_LANES - 2 * A_HEADS), F32), wl[:, o_b:]],
                                axis=1).astype(BF16)
        zl, xl, dtl, pbl, ql, fl, il, gl = _inproj(h_lat, m_lat, w_cat)
        zc, xc, dtc, pbc, qc, fc, ic, gc = _inproj(h_ctx, m_ctx, w_cat)

        lane_pad = jnp.zeros((V7X_LANES - 2 * A_HEADS,), F32)
        dtb = jnp.concatenate([a_dt_bias[l].reshape(-1), lane_pad]).reshape(1, V7X_LANES)
        aneg = jnp.concatenate([-jnp.exp(a_log[l].astype(F32)).reshape(-1), lane_pad]).reshape(1, V7X_LANES)
        dsk = jnp.repeat(a_d[l].astype(F32), A_HEAD_DIM).reshape(1, a_inner)
        ya_c, ya_l = _ssd(zc, zl, xc, xl, dtc, dtl, a_conv_w[l], a_conv_b[l].reshape(1, -1), dtb, aneg, dsk,
                          a_norm_w[l].reshape(1, -1))

        yb_c, yb_l = _s5(pbc, pbl, toe_all[l], wst_all[l], wout_all[l], dec_all[l], s5_d[l].reshape(1, -1),
                         s5_glu_w[l].astype(BF16), s5_glu_b[l].reshape(1, -1))

        yc_c, yc_l = _hgrn((qc, fc, ic, gc), (ql, fl, il, gl), lower_bounds[l].astype(F32),
                           hg_norm_w[l].reshape(1, c_width))

        w_o = w_out[l].astype(BF16)
        h_lat = _ffn(h_lat, m_lat, wi1, wo1, fw, base=6, final=last, to_raster=last and col_order,
                     mixed=(ya_l, yb_l, yc_l, w_o))
        if last:
            col_order = False
        else:
            h_ctx = _ffn(h_ctx, m_ctx, wi1, wo1, fw, base=6, final=False, mixed=(ya_c, yb_c, yc_c, w_o))
    return h_lat
```

```python
import functools
import math

import jax
import jax.numpy as jnp
from jax import lax
from jax.experimental import pallas as pl
from jax.experimental.pallas import tpu as pltpu

F32 = jnp.float32
BF16 = jnp.bfloat16
EPS = 1e-6
NEG_LOG2E = -1.4426950408889634

V7X_VMEM_BYTES = 64 * 1024 * 1024
V7X_LANES = 128
V7X_SUBLANES = 8
V7X_MXU_DIM = 256

GRID_W = 64
N_MOD = 9
A_HEADS = 8
A_HEAD_DIM = 64
A_GROUPS = 2
A_STATE = 64
A_CONV = 5
B_GROUP = 16
B_STATE = 64
C_HEADS = 4
C_KEY = 64

TOKEN_TILES = (512, 384, 256)
MIX_CHUNK = 256
SCAN_UNROLL = 2
HG_SAMPLES_PER_STEP = 1
S5_CHUNK = 16
GRID_COLS_PER_TILE = 2 * V7X_SUBLANES


def _vmem_limit(estimate_bytes):
    return int(min(V7X_VMEM_BYTES - 6 * 1024 * 1024, max(estimate_bytes, 16 * 1024 * 1024)))


def _token_tile(n):
    for tm in TOKEN_TILES:
        if n % tm == 0:
            return tm
    raise ValueError(f"token count {n} has no supported tile")


def _silu(x):
    return x * jax.nn.sigmoid(x)


def _rms(x):
    return x * lax.rsqrt(jnp.mean(x * x, axis=-1, keepdims=True) + EPS)


def _split3(x):
    hi = x.astype(BF16)
    r = x - hi.astype(F32)
    mid = r.astype(BF16)
    lo = (r - mid.astype(F32)).astype(BF16)
    return hi, mid, lo


def _dot01_left(m01, x):
    hi, mid, lo = _split3(x)
    d = lambda a: jnp.dot(m01, a, preferred_element_type=F32)
    return (d(lo) + d(mid)) + d(hi)


def _dot01_right(x, m01):
    hi, mid, lo = _split3(x)
    d = lambda a: jnp.dot(a, m01, preferred_element_type=F32)
    return (d(lo) + d(mid)) + d(hi)


def _dot_nt(a, b):
    return lax.dot_general(a, b, (((1,), (1,)), ((), ())), preferred_element_type=F32)


def _mod_spec(mod):
    per_sample = mod.shape[0] > 1
    return pl.BlockSpec((1,) + mod.shape[1:], lambda b, t: (b if per_sample else 0, 0, 0))


def _mod_kernel(c_ref, w_ref, b_ref, o_ref):
    s = _silu(c_ref[...]).astype(BF16)
    o_ref[0] = jnp.dot(s, w_ref[0].astype(BF16), preferred_element_type=F32) + b_ref[0]


def _mod_vectors(cvec, mod_w, mod_b):
    depth, dm, nm = mod_w.shape
    rows = cvec.shape[0]
    tn = 1024
    return pl.pallas_call(
        _mod_kernel,
        grid=(depth, nm // tn),
        in_specs=[pl.BlockSpec((rows, dm), lambda l, j: (0, 0)),
                  pl.BlockSpec((1, dm, tn), lambda l, j: (l, 0, j)),
                  pl.BlockSpec((1, 1, tn), lambda l, j: (l, 0, j))],
        out_specs=pl.BlockSpec((1, rows, tn), lambda l, j: (l, 0, j)),
        out_shape=jax.ShapeDtypeStruct((depth, rows, nm), F32),
        compiler_params=pltpu.CompilerParams(dimension_semantics=("arbitrary", "arbitrary")),
        name="mod_vectors",
    )(cvec, mod_w, mod_b.reshape(depth, 1, nm))


def _ff_chunks(d_ff):
    if d_ff % V7X_MXU_DIM:
        return ((0, d_ff),)
    tiles = d_ff // V7X_MXU_DIM
    first = (tiles + 1) // 2 * V7X_MXU_DIM
    return ((0, first), (first, d_ff - first)) if d_ff > first else ((0, d_ff),)


def _ffn_kernel(h_ref, mod_ref, win_ref, wout_ref, fw_ref, *rest, base, d_ff, final, cols_in, cols_out, mixed):
    o_ref = rest[-1]
    if cols_in:
        x = jnp.concatenate([h_ref[0, :, wv, :] for wv in range(cols_in)], axis=0)
    else:
        x = h_ref[0]
    m = mod_ref[0]
    if mixed:
        ya_ref, yb_ref, yc_ref, wmix_ref = rest[:4]
        mix = jnp.concatenate([r[0].astype(BF16) for r in (ya_ref, yb_ref, yc_ref)], axis=1)
        x = x + m[base - 1:base] * jnp.dot(mix, wmix_ref[...], preferred_element_type=F32)
    u = (_rms(x) * (1.0 + m[base + 1:base + 2]) + m[base:base + 1]).astype(BF16)
    acc = jnp.zeros(x.shape, F32)
    for c0, cw in _ff_chunks(d_ff):
        g = jnp.dot(u, win_ref[:, c0:c0 + cw], preferred_element_type=F32)
        up = jnp.dot(u, win_ref[:, d_ff + c0:d_ff + c0 + cw], preferred_element_type=F32)
        a = (_silu(g) * up).astype(BF16)
        acc = acc + jnp.dot(a, wout_ref[c0:c0 + cw, :], preferred_element_type=F32)
    y = x + 0.5 * m[base + 2:base + 3] * acc
    if final:
        y = _rms(y) * fw_ref[...]
    if cols_out:
        nr = y.shape[0] // cols_out
        for wv in range(cols_out):
            o_ref[0, :, wv, :] = y[wv * nr:(wv + 1) * nr]
    else:
        o_ref[0] = y


def _ffn(h, mod, w_in, w_out, final_w, *, base, final, to_column=False, to_raster=False, mixed=None):
    bsz, lt, dm = h.shape
    d_ff = w_out.shape[0]
    grid_rows = lt // GRID_W
    cols = GRID_COLS_PER_TILE
    tm = grid_rows * cols if (to_column or to_raster) else _token_tile(lt)
    tok = lambda wd: pl.BlockSpec((1, tm, wd), lambda b, t: (b, t, 0))
    const = lambda a: pl.BlockSpec(a.shape, lambda b, t: (0, 0), pipeline_mode=pl.Buffered(1))
    grid_blk = pl.BlockSpec((1, grid_rows, cols, dm), lambda b, t: (b, 0, t, 0))
    kern = functools.partial(_ffn_kernel, base=base, d_ff=d_ff, final=final, mixed=mixed is not None,
                             cols_in=cols if to_column else 0, cols_out=cols if to_raster else 0)
    extra, extra_specs = (), []
    if mixed is not None:
        extra = tuple(mixed)
        extra_specs = [tok(a.shape[-1]) for a in mixed[:3]] + [const(mixed[3])]
    w_bytes = (w_in.size + w_out.size + (mixed[3].size if mixed is not None else 0)) * 2
    est = w_bytes + 8 * tm * dm * 4 + 3 * tm * d_ff * 4 + 6 * 1024 * 1024
    out = pl.pallas_call(
        kern,
        grid=(bsz, lt // tm),
        in_specs=[grid_blk if to_column else tok(dm), _mod_spec(mod), const(w_in), const(w_out),
                  pl.BlockSpec((1, dm), lambda b, t: (0, 0))] + extra_specs,
        out_specs=grid_blk if to_raster else tok(dm),
        out_shape=jax.ShapeDtypeStruct((bsz, grid_rows, GRID_W, dm) if to_raster else (bsz, lt, dm), F32),
        compiler_params=pltpu.CompilerParams(dimension_semantics=("arbitrary", "arbitrary"),
                                             vmem_limit_bytes=_vmem_limit(est)),
        name="ffn",
    )(h.reshape(bsz, grid_rows, GRID_W, dm) if to_column else h, mod, w_in, w_out, final_w, *extra)
    return out.reshape(bsz, lt, dm)


IN_SPLIT = ((512, BF16), (768, F32), (128, F32), (256, F32), (256, BF16), (512, F32), (256, BF16), (256, BF16))


def _inproj_kernel(h_ref, mod_ref, w_ref, *out_refs):
    m = mod_ref[0]
    u = (_rms(h_ref[0]) * (1.0 + m[4:5]) + m[3:4]).astype(BF16)
    p = jnp.dot(u, w_ref[...], preferred_element_type=F32)
    off = 0
    for ref, (wd, dt) in zip(out_refs, IN_SPLIT):
        ref[0] = p[:, off:off + wd].astype(dt)
        off += wd


def _inproj(h, mod, w):
    bsz, lt, dm = h.shape
    tm = _token_tile(lt)
    tok = lambda wd: pl.BlockSpec((1, tm, wd), lambda b, t: (b, t, 0))
    return pl.pallas_call(
        _inproj_kernel,
        grid=(bsz, lt // tm),
        in_specs=[tok(dm), _mod_spec(mod),
                  pl.BlockSpec(w.shape, lambda b, t: (0, 0), pipeline_mode=pl.Buffered(1))],
        out_specs=[tok(wd) for wd, _ in IN_SPLIT],
        out_shape=[jax.ShapeDtypeStruct((bsz, lt, wd), dt) for wd, dt in IN_SPLIT],
        compiler_params=pltpu.CompilerParams(dimension_semantics=("arbitrary", "arbitrary"),
                                             vmem_limit_bytes=_vmem_limit(40 * 1024 * 1024)),
        name="inproj",
    )(h, mod, w)


def _seq_spec(a):
    return pl.BlockSpec((1,) + a.shape[1:], lambda b: (b,) + (0,) * (a.ndim - 1))


def _full_spec(a):
    return pl.BlockSpec(a.shape, lambda b: (0,) * a.ndim)


def _ssd_kernel(zc_ref, zl_ref, xc_in_ref, xl_in_ref, dtc_ref, dtl_ref, cw_ref, cb_ref, dtb_ref, aneg_ref,
                dsk_ref, nw_ref, oc_ref, ol_ref, xconv_ref, yf_ref, yb_ref, st_ref, *, q):
    inner = A_HEADS * A_HEAD_DIM
    gw = A_STATE
    hpg = A_HEADS // A_GROUPS
    gl = hpg * A_HEAD_DIM
    n_ctx = oc_ref.shape[1]
    segs = ((zc_ref, xc_in_ref, dtc_ref, oc_ref, 0), (zl_ref, xl_in_ref, dtl_ref, ol_ref, n_ctx))

    cw = cw_ref[...]
    cb = cb_ref[...]
    for _, xin_ref, _, o_ref, base in segs:
        ls = o_ref.shape[1]
        n_chunks = ls // q

        def conv_body(i, carry, xin_ref=xin_ref, base=base, ls=ls, n_chunks=n_chunks):
            t0 = pl.multiple_of(i * q, q)
            cur = xin_ref[0, pl.ds(t0, q), :]
            p0 = pl.multiple_of(jnp.maximum(t0 - V7X_SUBLANES, 0), V7X_SUBLANES)
            n0 = pl.multiple_of(jnp.minimum(t0 + q, ls - V7X_SUBLANES), V7X_SUBLANES)
            prev = jnp.where(i == 0, 0.0, xin_ref[0, pl.ds(p0, V7X_SUBLANES), :])
            nxt = jnp.where(i == n_chunks - 1, 0.0, xin_ref[0, pl.ds(n0, V7X_SUBLANES), :])
            ext = jnp.concatenate([prev, cur, nxt], axis=0)
            acc = cb
            for k in range(A_CONV):
                s0 = V7X_SUBLANES - A_CONV // 2 + k
                acc = acc + cw[k:k + 1] * ext[s0:s0 + q]
            xconv_ref[pl.ds(pl.multiple_of(base + t0, q), q), :] = _silu(acc)
            return carry

        lax.fori_loop(0, n_chunks, conv_body, 0)

    st_ref[...] = jnp.zeros(st_ref.shape, F32)

    rowi = lax.broadcasted_iota(jnp.int32, (q, q), 0)
    coli = lax.broadcasted_iota(jnp.int32, (q, q), 1)
    er = lax.broadcasted_iota(jnp.int32, (V7X_LANES, inner), 0)
    ec = lax.broadcasted_iota(jnp.int32, (V7X_LANES, inner), 1)
    dtb = dtb_ref[...]
    aneg = aneg_ref[...]

    def load_chunk(seg, t0):
        dt_ref, base = seg[2], seg[4]
        xc = xconv_ref[pl.ds(pl.multiple_of(base + t0, q), q), :]
        dt = jax.nn.softplus(dt_ref[0, pl.ds(t0, q), :] + dtb)
        return xc[:, :inner], xc[:, inner:inner + A_GROUPS * gw], xc[:, inner + A_GROUPS * gw:], dt

    def scan_sum(d, d_a):
        tri = (rowi >= coli) if d == 0 else (rowi <= coli)
        return _dot01_left(jnp.where(tri, 1.0, 0.0).astype(BF16), d_a)

    def carried(d, xs, bm, cm, dt, cum):
        last = q - 1 if d == 0 else 0
        expand = jnp.where(er == d * A_HEADS + (ec >> int(math.log2(A_HEAD_DIM))), 1.0, 0.0).astype(BF16)
        e_hi, e_mid, _ = _split3(jnp.exp(cum))
        ecum = (jnp.dot(e_mid, expand, preferred_element_type=F32) + jnp.dot(e_hi, expand, preferred_element_type=F32))
        wcol = (jnp.exp(cum[last:last + 1, :] - cum) * dt).astype(BF16)
        xw = (xs * jnp.dot(wcol, expand, preferred_element_type=F32)).astype(BF16)
        e_last = ecum[last:last + 1, :]
        bm_t = bm.T
        ys = []
        for g in range(A_GROUPS):
            cg = cm[:, g * gw:(g + 1) * gw].astype(BF16)
            st = st_ref[d, g]
            ys.append(jnp.dot(cg, st.astype(BF16), preferred_element_type=F32) * ecum[:, g * gl:(g + 1) * gl])
            upd = jnp.dot(bm_t[g * gw:(g + 1) * gw].astype(BF16), xw[:, g * gl:(g + 1) * gl],
                          preferred_element_type=F32)
            st_ref[d, g] = st * e_last[:, g * gl:(g + 1) * gl] + upd
        return jnp.concatenate(ys, axis=1)

    lower = rowi > coli
    diag = rowi == coli

    def step(seg, i, n_chunks):
        o_ref = seg[3]
        t0 = pl.multiple_of(i * q, q)
        xs, bm, cm, dt = load_chunk(seg, t0)
        d_a = dt * aneg
        cum_f = scan_sum(0, d_a)
        cum_b = scan_sum(1, d_a)
        log_dt = jnp.log(dt)
        col_f, col_b = cum_f, cum_b
        row_f = (cum_f - log_dt).T
        row_b = (cum_b - log_dt).T
        dt_t = dt.T
        xs_bf = xs.astype(BF16)
        ys = []
        for g in range(A_GROUPS):
            cg = cm[:, g * gw:(g + 1) * gw].astype(BF16)
            bg = bm[:, g * gw:(g + 1) * gw].astype(BF16)
            gmat = _dot_nt(cg, bg)
            for hh in range(hpg):
                h = g * hpg + hh
                hb = A_HEADS + h
                arg = jnp.where(lower, col_f[:, h:h + 1] - row_f[h:h + 1, :], col_b[:, hb:hb + 1] - row_b[hb:hb + 1, :])
                wgt = jnp.where(diag, dt_t[h:h + 1, :] + dt_t[hb:hb + 1, :], jnp.exp(arg))
                sc = (gmat * wgt).astype(BF16)
                ys.append(jnp.dot(sc, xs_bf[:, h * A_HEAD_DIM:(h + 1) * A_HEAD_DIM], preferred_element_type=F32))
        yf_ref[pl.ds(pl.multiple_of(seg[4] + t0, q), q), :] = (jnp.concatenate(ys, axis=1)
                                                              + carried(0, xs, bm, cm, dt, cum_f))
        tb = pl.multiple_of((n_chunks - 1 - i) * q, q)
        xsb, bmb, cmb, dtb_ = load_chunk(seg, tb)
        yb_ref[pl.ds(pl.multiple_of(seg[4] + tb, q), q), :] = carried(1, xsb, bmb, cmb, dtb_, scan_sum(1, dtb_ * aneg))

    dsk = dsk_ref[...]
    nw = nw_ref[...]
    for seg in segs:
        n_chunks = seg[3].shape[1] // q
        lax.fori_loop(0, n_chunks, lambda i, carry, seg=seg, n_chunks=n_chunks: (step(seg, i, n_chunks), carry)[1], 0,
                      unroll=math.gcd(n_chunks, SCAN_UNROLL))

    for z_ref, _, _, o_ref, base in segs:
        def fin_body(i, carry, z_ref=z_ref, o_ref=o_ref, base=base):
            t0 = pl.multiple_of(i * q, q)
            xs = xconv_ref[pl.ds(pl.multiple_of(base + t0, q), q), :inner]
            rows = pl.ds(pl.multiple_of(base + t0, q), q)
            y = yf_ref[rows, :] + yb_ref[rows, :] + dsk * xs
            y = y * _silu(z_ref[0, pl.ds(t0, q), :].astype(F32))
            o_ref[0, pl.ds(t0, q), :] = (_rms(y) * nw).astype(o_ref.dtype)
            return carry

        lax.fori_loop(0, o_ref.shape[1] // q, fin_body, 0)


def _ssd(zc, zl, xc, xl, dtc, dtl, cw, cb, dtb, aneg, dsk, nw):
    bsz, n_ctx, inner = zc.shape
    seq = zl.shape[1]
    lt = n_ctx + seq
    cd = xc.shape[-1]
    ins = (zc, zl, xc, xl, dtc, dtl)
    consts = (cw, cb, dtb, aneg, dsk, nw)
    est = 2 * lt * (2 * inner + 4 * cd + 4 * V7X_LANES + 4 * inner) + 4 * lt * (cd + inner) + 12 * 1024 * 1024
    return pl.pallas_call(
        functools.partial(_ssd_kernel, q=MIX_CHUNK),
        grid=(bsz,),
        in_specs=[_seq_spec(a) for a in ins] + [_full_spec(a) for a in consts],
        out_specs=[pl.BlockSpec((1, n_ctx, inner), lambda b: (b, 0, 0)),
                   pl.BlockSpec((1, seq, inner), lambda b: (b, 0, 0))],
        out_shape=[jax.ShapeDtypeStruct((bsz, n_ctx, inner), BF16), jax.ShapeDtypeStruct((bsz, seq, inner), BF16)],
        scratch_shapes=[pltpu.VMEM((lt, cd), F32),
                        pltpu.VMEM((lt, inner), F32),
                        pltpu.VMEM((lt, inner), F32),
                        pltpu.VMEM((2, A_GROUPS, A_STATE, inner // A_GROUPS), F32)],
        compiler_params=pltpu.CompilerParams(dimension_semantics=("arbitrary",),
                                             vmem_limit_bytes=_vmem_limit(est)),
        name="ssd_mixer",
    )(*ins, *consts)


def _hgrn_kernel(qc_ref, ql_ref, ffc_ref, ffl_ref, fbc_ref, fbl_ref, vc_ref, vl_ref, gc_ref, gl_ref,
                 lb_ref, nw_ref, oc_ref, ol_ref, of_ref, ob_ref, st_ref, *, c):
    w = C_HEADS * C_KEY
    nv = c // V7X_SUBLANES
    segs = ((qc_ref, (ffc_ref, fbc_ref), vc_ref, gc_ref, oc_ref, 0),
            (ql_ref, (ffl_ref, fbl_ref), vl_ref, gl_ref, ol_ref, oc_ref.shape[1]))
    st_ref[...] = jnp.zeros(st_ref.shape, F32)

    rowi = lax.broadcasted_iota(jnp.int32, (c, c), 0)
    coli = lax.broadcasted_iota(jnp.int32, (c, c), 1)
    pair_bits = rowi ^ coli
    hr = lax.broadcasted_iota(jnp.int32, (w, w), 0)
    hc = lax.broadcasted_iota(jnp.int32, (w, w), 1)
    same_head = (hr >> int(math.log2(C_KEY))) == (hc >> int(math.log2(C_KEY)))
    bones = jnp.where(same_head, 1.0, 0.0).astype(BF16)
    trow = lax.broadcasted_iota(jnp.int32, (c, w), 0)
    sub3 = lax.broadcasted_iota(jnp.int32, (nv, V7X_SUBLANES, w), 1)

    def boundary_small(x3, p0, s):
        b = 1
        while b <= s:
            if p0 & b:
                x3 = jnp.where((sub3 & b) != 0, x3, pltpu.roll(x3, V7X_SUBLANES - b, axis=1))
            else:
                x3 = jnp.where((sub3 & b) != 0, pltpu.roll(x3, b, axis=1), x3)
            b *= 2
        return x3

    def gates(d, seg, smp, t0):
        zf = seg[1][d][smp, pl.ds(t0, c), :]
        lb = lb_ref[d:d + 1, :]
        f = lb + (1.0 - lb) * jax.nn.sigmoid(zf)
        causal = (rowi >= coli) if d == 0 else (rowi <= coli)
        cum = _dot01_left(jnp.where(causal, 1.0, 0.0).astype(BF16), jnp.log(f))
        return 1.0 - f, cum

    def boundary(cum, d, s):
        p0 = (s - 1) if d == 0 else s
        return boundary_small(cum.reshape(nv, V7X_SUBLANES, w), p0, s).reshape(c, w)

    def carried(d, smp, qs, kk, cum, v32):
        last = c - 1 if d == 0 else 0
        cum_last = cum[last:last + 1, :]
        st = st_ref[smp, d]
        qe = (qs * jnp.exp(cum)).astype(BF16)
        kw = (kk * jnp.exp(cum_last - cum)).astype(BF16)
        upd = jnp.dot(v32.T.astype(BF16), kw, preferred_element_type=F32)
        st_ref[smp, d] = jnp.where(same_head, st * jnp.exp(cum_last) + upd, 0.0)
        return _dot_nt(qe, st.astype(BF16))

    def step(seg, smp, i, n_chunks):
        q_ref, _, v_ref, _, o_ref, base = seg
        t0 = pl.multiple_of(i * c, c)
        qs = _silu(q_ref[smp, pl.ds(t0, c), :].astype(F32))
        v = v_ref[smp, pl.ds(t0, c), :]
        v32 = v.astype(F32)
        kk0, cum0 = gates(0, seg, smp, t0)
        kk1, cum1 = gates(1, seg, smp, t0)

        out = jnp.dot((qs * (kk0 + kk1)).astype(BF16), bones, preferred_element_type=F32) * v32
        scores = [jnp.zeros((c, c), F32) for _ in range(C_HEADS)]
        s = 1
        while s < c:
            if s < V7X_SUBLANES:
                e0 = jnp.exp2(jnp.abs(cum0 - boundary(cum0, 0, s)) * NEG_LOG2E)
                e1 = jnp.exp2(jnp.abs(cum1 - boundary(cum1, 1, s)) * NEG_LOG2E)
                has_bit = (trow & s) != 0
                qm = (qs * jnp.where(has_bit, e0, e1)).astype(BF16)
                km = jnp.where(has_bit, kk1 * e1, kk0 * e0).astype(BF16)
            else:
                halves = lambda a: a.reshape(c // (2 * s), 2, s, w)
                c0, c1, q2 = halves(cum0), halves(cum1), halves(qs)
                x0 = c0 - c0[:, 0:1, s - 1:s, :]
                x1 = c1 - c1[:, 1:2, 0:1, :]
                qarg = jnp.concatenate([x1[:, 0:1], x0[:, 1:2]], axis=1)
                karg = jnp.concatenate([x0[:, 0:1], x1[:, 1:2]], axis=1)
                ksel = jnp.concatenate([halves(kk0)[:, 0:1], halves(kk1)[:, 1:2]], axis=1)
                qm = (q2 * jnp.exp2(qarg * (-NEG_LOG2E))).reshape(c, w).astype(BF16)
                km = (ksel * jnp.exp2(karg * NEG_LOG2E)).reshape(c, w).astype(BF16)
            take = pair_bits >= s
            for h in range(C_HEADS):
                sl = slice(h * C_KEY, (h + 1) * C_KEY)
                lvl = _dot_nt(qm[:, sl], km[:, sl])
                if 2 * s == c and s % V7X_LANES == 0:
                    scores[h] = jnp.concatenate(
                        [jnp.concatenate([scores[h][:s, :s], lvl[:s, s:]], axis=1),
                         jnp.concatenate([lvl[s:, :s], scores[h][s:, s:]], axis=1)], axis=0)
                else:
                    scores[h] = jnp.where(take, lvl, scores[h])
            s *= 2
        out = out + jnp.concatenate(
            [jnp.dot(scores[h].astype(BF16), v[:, h * C_KEY:(h + 1) * C_KEY], preferred_element_type=F32)
             for h in range(C_HEADS)], axis=1)
        of_ref[smp, pl.ds(pl.multiple_of(base + t0, c), c), :] = out + carried(0, smp, qs, kk0, cum0, v32)
        tb = pl.multiple_of((n_chunks - 1 - i) * c, c)
        qsb = _silu(q_ref[smp, pl.ds(tb, c), :].astype(F32))
        kkb, cumb = gates(1, seg, smp, tb)
        ob_ref[smp, pl.ds(pl.multiple_of(base + tb, c), c), :] = carried(
            1, smp, qsb, kkb, cumb, v_ref[smp, pl.ds(tb, c), :].astype(F32))

    n_smp = oc_ref.shape[0]
    for seg in segs:
        n_chunks = seg[4].shape[1] // c

        def scan_body(i, carry, seg=seg, n_chunks=n_chunks):
            for smp in range(n_smp):
                step(seg, smp, i, n_chunks)
            return carry

        lax.fori_loop(0, n_chunks, scan_body, 0, unroll=math.gcd(n_chunks, SCAN_UNROLL))

    nw = nw_ref[...]
    for _, _, _, g_ref, o_ref, base in segs:
        def fin_body(i, carry, g_ref=g_ref, o_ref=o_ref, base=base):
            t0 = pl.multiple_of(i * c, c)
            for smp in range(n_smp):
                rows = pl.ds(pl.multiple_of(base + t0, c), c)
                o = of_ref[smp, rows, :] + ob_ref[smp, rows, :]
                ms = _dot01_right(o * o, bones) * (1.0 / C_KEY)
                y = o * lax.rsqrt(ms + EPS) * nw
                o_ref[smp, pl.ds(t0, c), :] = (y * _silu(g_ref[smp, pl.ds(t0, c), :].astype(F32))).astype(o_ref.dtype)
            return carry

        lax.fori_loop(0, o_ref.shape[1] // c, fin_body, 0)


def _hgrn(ctx_parts, lat_parts, lower, nw):
    qc, fc, ic, gc = ctx_parts
    ql, fl, il, gl = lat_parts
    bsz, n_ctx, w = qc.shape
    seq = ql.shape[1]
    ns = math.gcd(bsz, HG_SAMPLES_PER_STEP)
    blk = lambda a, j: pl.BlockSpec((ns, a.shape[1], w), lambda b: (b, 0, j))
    ins = (qc, ql, fc, fl, fc, fl, ic, il, gc, gl)
    specs = [blk(qc, 0), blk(ql, 0), blk(fc, 0), blk(fl, 0), blk(fc, 1), blk(fl, 1),
             blk(ic, 0), blk(il, 0), blk(gc, 0), blk(gl, 0)]
    est = ns * 2 * (n_ctx + seq) * w * (2 + 8 + 2 + 2 + 4) + 20 * 1024 * 1024
    return pl.pallas_call(
        functools.partial(_hgrn_kernel, c=MIX_CHUNK),
        grid=(bsz // ns,),
        in_specs=specs + [_full_spec(lower), _full_spec(nw)],
        out_specs=[blk(qc, 0), blk(ql, 0)],
        out_shape=[jax.ShapeDtypeStruct((bsz, n_ctx, w), BF16), jax.ShapeDtypeStruct((bsz, seq, w), BF16)],
        scratch_shapes=[pltpu.VMEM((ns, n_ctx + seq, w), F32), pltpu.VMEM((ns, n_ctx + seq, w), F32),
                        pltpu.VMEM((ns, 2, w, w), F32)],
        compiler_params=pltpu.CompilerParams(dimension_semantics=("arbitrary",),
                                             vmem_limit_bytes=_vmem_limit(est)),
        name="hgrn_mixer",
    )(*ins, lower, nw)


def _s5_tables(lam_re, lam_im, log_step, b_re, b_im, c_re, c_im):
    q = S5_CHUNK
    hp = lax.Precision.HIGHEST
    lam_re = lam_re.astype(F32)
    lam_im = lam_im.astype(F32)
    step = jnp.exp(log_step.astype(F32))[..., None]
    tau = jnp.arange(q + 1, dtype=F32)[:, None, None, None]
    mag = jnp.exp(lam_re * step * tau)
    pr = mag * jnp.cos(lam_im * step * tau)
    pi = mag * jnp.sin(lam_im * step * tau)
    ar, ai = pr[1], pi[1]
    den = lam_re * lam_re + lam_im * lam_im
    nr = ar - 1.0
    kr = (nr * lam_re + ai * lam_im) / den
    ki = (ai * lam_re - nr * lam_im) / den
    b_re = b_re.astype(F32)
    b_im = b_im.astype(F32)
    br = kr[..., None] * b_re - ki[..., None] * b_im
    bi = kr[..., None] * b_im + ki[..., None] * b_re
    c_re = c_re.astype(F32)
    c_im = c_im.astype(F32)
    qr = pr[..., None] * br - pi[..., None] * bi
    qi = pr[..., None] * bi + pi[..., None] * br
    hk = jnp.einsum('dgon,tdgni->tdgoi', c_re, qr, precision=hp) - jnp.einsum('dgon,tdgni->tdgoi', c_im, qi, precision=hp)
    g, cg = b_re.shape[1], b_re.shape[3]
    lags = jnp.arange(q + 1)
    pos = jnp.arange(q)
    onehot = lambda idx: (idx[..., None] == lags).astype(F32)
    lag = jnp.stack([pos[None, :] - pos[:, None], pos[:, None] - pos[None, :]])
    toe = jnp.einsum('dstl,ldgoi->dgsito', onehot(lag), hk, precision=hp).reshape(2, g, q * cg, q * cg)
    m_st = onehot(jnp.stack([q - 1 - pos, pos]))
    wst = jnp.concatenate([jnp.einsum('dsl,ldgni->dgsin', m_st, qr, precision=hp),
                           jnp.einsum('dsl,ldgni->dgsin', m_st, qi, precision=hp)], axis=-1)
    wst = wst.reshape(2, g, q * cg, -1)
    m_o = onehot(jnp.stack([pos + 1, q - pos]))
    po_r = jnp.einsum('dtl,ldgn->dtgn', m_o, pr, precision=hp)
    po_i = jnp.einsum('dtl,ldgn->dtgn', m_o, pi, precision=hp)
    ein = lambda cc, pp: jnp.einsum('dgon,dtgn->dgnto', cc, pp, precision=hp)
    wout = jnp.concatenate([ein(c_re, po_r) - ein(c_im, po_i), -ein(c_re, po_i) - ein(c_im, po_r)], axis=2)
    wout = wout.reshape(2, g, -1, q * cg)
    dec = jnp.stack([jnp.concatenate([pr[q], pr[q]], axis=-1),
                     jnp.concatenate([-pi[q], pi[q]], axis=-1),
                     jnp.concatenate([pi[q], -pi[q]], axis=-1)], axis=1)
    return toe.astype(BF16), wst.astype(BF16), wout.astype(BF16), dec


def _lane_block_transpose(arrs, blk):
    n = len(arrs)
    rows, width = arrs[0].shape
    lane_blk = lax.broadcasted_iota(jnp.int32, (rows, width), 1) >> int(math.log2(blk))
    k = 1
    while k < n:
        hi_half = (lane_blk & k) != 0
        new = list(arrs)
        for a in range(n):
            if a & k == 0:
                lo_arr, hi_arr = arrs[a], arrs[a | k]
                new[a] = jnp.where(hi_half, pltpu.roll(hi_arr, blk * k, axis=1), lo_arr)
                new[a | k] = jnp.where(hi_half, hi_arr, pltpu.roll(lo_arr, width - blk * k, axis=1))
        arrs = new
        k *= 2
    return arrs


def _gelu_tanh(x):
    return 0.5 * x * (1.0 + jnp.tanh(math.sqrt(2.0 / math.pi) * (x + 0.044715 * (x * x * x))))


def _s5_kernel(uc_ref, ul_ref, toe_ref, wst_ref, wout_ref, dec_ref, sd_ref, gw_ref, gb_ref,
               oc_ref, ol_ref, z_ref, xin_ref):
    q = uc_ref.shape[2]
    ncc = uc_ref.shape[1]
    n_chunks = ncc + ul_ref.shape[1]
    ng = toe_ref.shape[1]
    half = dec_ref.shape[-1] // 2
    rows = [jnp.concatenate([uc_ref[0, :, s, :], ul_ref[0, :, s, :]], axis=0) for s in range(q)]
    ug = [a.astype(BF16) for a in _lane_block_transpose(rows, B_GROUP)]
    ys = [None] * ng
    for d in range(2):
        zs = []
        for g in range(ng):
            ys[g] = jnp.dot(ug[g], toe_ref[d, g], preferred_element_type=F32) + (0.0 if ys[g] is None else ys[g])
            zs.append(jnp.dot(ug[g], wst_ref[d, g], preferred_element_type=F32))
        z = pltpu.einshape("gcn->cgn", jnp.stack(zs))
        z_ref[0, d] = z
        z_ref[1, d] = pltpu.roll(z, half, axis=2)

    coef = [[dec_ref[d, j] for j in range(3)] for d in range(2)]

    def body(i, carry):
        nxt = []
        for d in range(2):
            if d == 0:
                c_idx = i
            else:
                c_idx = jnp.where(i < ncc, ncc - 1 - i, n_chunks - 1 - (i - ncc))
            x, xs = carry[2 * d], carry[2 * d + 1]
            a1, a2, a2s = coef[d]
            xin_ref[d, c_idx] = x
            nxt.append(a1 * x + a2 * xs + z_ref[0, d, c_idx])
            nxt.append(a1 * xs + a2s * x + z_ref[1, d, c_idx])
        return tuple(nxt)

    zero = jnp.zeros(coef[0][0].shape, F32)
    lax.fori_loop(0, n_chunks, body, (zero, zero, zero, zero), unroll=2)
    for d in range(2):
        xg = pltpu.einshape("cgn->gcn", xin_ref[d]).astype(BF16)
        for g in range(ng):
            ys[g] = ys[g] + jnp.dot(xg[g], wout_ref[d, g], preferred_element_type=F32)
    yt = _lane_block_transpose(ys, B_GROUP)
    sd = sd_ref[...]
    gw = gw_ref[...]
    gb = gb_ref[...]
    for t in range(q):
        y = _gelu_tanh(yt[t] + sd * rows[t])
        gate = jnp.dot(y.astype(BF16), gw, preferred_element_type=F32) + gb
        y = y * jax.nn.sigmoid(gate)
        oc_ref[0, :, t, :] = y[:ncc]
        ol_ref[0, :, t, :] = y[ncc:]


def _s5(pbc, pbl, toe, wst, wout, dec, s5_d, glu_w, glu_b):
    bsz, n_ctx, wd = pbc.shape
    seq = pbl.shape[1]
    q = S5_CHUNK
    ng = toe.shape[1]
    ns = wst.shape[-1]
    n_chunks = (n_ctx + seq) // q
    uc = pbc.reshape(bsz, n_ctx // q, q, wd)
    ul = pbl.reshape(bsz, seq // q, q, wd)
    consts = (toe, wst, wout, dec, s5_d, glu_w, glu_b)
    oc, ol = pl.pallas_call(
        _s5_kernel,
        grid=(bsz,),
        in_specs=[_seq_spec(uc), _seq_spec(ul)] + [_full_spec(a) for a in consts],
        out_specs=[_seq_spec(uc), _seq_spec(ul)],
        out_shape=[jax.ShapeDtypeStruct(uc.shape, F32), jax.ShapeDtypeStruct(ul.shape, F32)],
        scratch_shapes=[pltpu.VMEM((2, 2, n_chunks, ng, ns), F32), pltpu.VMEM((2, n_chunks, ng, ns), F32)],
        compiler_params=pltpu.CompilerParams(dimension_semantics=("arbitrary",),
                                             vmem_limit_bytes=_vmem_limit(40 * 1024 * 1024)),
        name="s5_mixer",
    )(uc, ul, *consts)
    return oc.reshape(bsz, n_ctx, wd), ol.reshape(bsz, seq, wd)


def _raster_to_column(t, rows):
    b, s, d = t.shape
    return t.reshape(b, rows, GRID_W, d).transpose(0, 2, 1, 3).reshape(b, s, d)


def _column_to_raster(t, rows):
    b, s, d = t.shape
    return t.reshape(b, GRID_W, rows, d).transpose(0, 2, 1, 3).reshape(b, s, d)


def kernel(x, c, ctx, c_ctx, mod_w, mod_b, ffn_w_in, ffn_w_out, w_in, w_out, a_conv_w, a_conv_b, a_dt_bias, a_log,
           a_d, a_norm_w, s5_lam_re, s5_lam_im, s5_log_step, s5_b_re, s5_b_im, s5_c_re, s5_c_im, s5_d, s5_glu_w,
           s5_glu_b, hg_lb_logits, hg_norm_w, final_norm_w):
    bsz, seq, dm = x.shape
    n_ctx = ctx.shape[1]
    depth = mod_w.shape[0]
    rows = seq // GRID_W
    assert n_ctx % MIX_CHUNK == 0 and seq % MIX_CHUNK == 0 and seq % GRID_W == 0
    a_inner = A_HEADS * A_HEAD_DIM
    a_conv_dim = a_conv_w.shape[-1]
    c_width = hg_norm_w.shape[-1]

    pad = (-(bsz + 1)) % V7X_SUBLANES
    cvec = jnp.concatenate([c, c_ctx[None, :], jnp.zeros((pad, dm), F32)], axis=0)
    mod_all = _mod_vectors(cvec, mod_w, mod_b)

    p_lb = jax.nn.softmax(hg_lb_logits.astype(F32), axis=0)
    lower_bounds = jnp.cumsum(p_lb, axis=0) - p_lb[:1]
    fw = final_norm_w.reshape(1, dm).astype(F32)
    toe_all, wst_all, wout_all, dec_all = jax.vmap(_s5_tables)(s5_lam_re, s5_lam_im, s5_log_step, s5_b_re, s5_b_im,
                                                               s5_c_re, s5_c_im)

    h_lat, h_ctx = x, ctx
    col_order = False
    for l in range(depth):
        last = l == depth - 1
        m_lat = mod_all[l, :bsz].reshape(bsz, N_MOD, dm)
        m_ctx = mod_all[l, bsz].reshape(1, N_MOD, dm)
        wi0, wo0 = ffn_w_in[l, 0].astype(BF16), ffn_w_out[l, 0].astype(BF16)
        wi1, wo1 = ffn_w_in[l, 1].astype(BF16), ffn_w_out[l, 1].astype(BF16)

        want_col = l % 2 == 1
        if col_order and not want_col:
            h_lat = _column_to_raster(h_lat, rows)
            col_order = False
        h_lat = _ffn(h_lat, m_lat, wi0, wo0, fw, base=0, final=False, to_column=want_col and not col_order)
        col_order = want_col
        h_ctx = _ffn(h_ctx, m_ctx, wi0, wo0, fw, base=0, final=False)

        wl = w_in[l]
        o_dt = a_inner + a_conv_dim
        o_b = o_dt + 2 * A_HEADS
        w_cat = jnp.concatenate([wl[:, :o_b], jnp.zeros((dm, V7X_LANES - 2 * A_HEADS), F32), wl[:, o_b:]],
                                axis=1).astype(BF16)
        zl, xl, dtl, pbl, ql, fl, il, gl = _inproj(h_lat, m_lat, w_cat)
        zc, xc, dtc, pbc, qc, fc, ic, gc = _inproj(h_ctx, m_ctx, w_cat)

        lane_pad = jnp.zeros((V7X_LANES - 2 * A_HEADS,), F32)
        dtb = jnp.concatenate([a_dt_bias[l].reshape(-1), lane_pad]).reshape(1, V7X_LANES)
        aneg = jnp.concatenate([-jnp.exp(a_log[l].astype(F32)).reshape(-1), lane_pad]).reshape(1, V7X_LANES)
        dsk = jnp.repeat(a_d[l].astype(F32), A_HEAD_DIM).reshape(1, a_inner)
        ya_c, ya_l = _ssd(zc, zl, xc, xl, dtc, dtl, a_conv_w[l], a_conv_b[l].reshape(1, -1), dtb, aneg, dsk,
                          a_norm_w[l].reshape(1, -1))

        yb_c, yb_l = _s5(pbc, pbl, toe_all[l], wst_all[l], wout_all[l], dec_all[l], s5_d[l].reshape(1, -1),
                         s5_glu_w[l].astype(BF16), s5_glu_b[l].reshape(1, -1))

        yc_c, yc_l = _hgrn((qc, fc, ic, gc), (ql, fl, il, gl), lower_bounds[l].astype(F32),
                           hg_norm_w[l].reshape(1, c_width))

        w_o = w_out[l].astype(BF16)
        h_lat = _ffn(h_lat, m_lat, wi1, wo1, fw, base=6, final=last, to_raster=last and col_order,
                     mixed=(ya_l, yb_l, yc_l, w_o))
        if last:
            col_order = False
        else:
            h_ctx = _ffn(h_ctx, m_ctx, wi1, wo1, fw, base=6, final=False, mixed=(ya_c, yb_c, yc_c, w_o))
    return h_lat
```

```python
import functools
import math

import jax
import jax.numpy as jnp
from jax import lax
from jax.experimental import pallas as pl
from jax.experimental.pallas import tpu as pltpu

F32 = jnp.float32
BF16 = jnp.bfloat16
EPS = 1e-6
LOG2E = 1.4426950408889634

V7X_VMEM_BYTES = 64 * 1024 * 1024
V7X_LANES = 128
V7X_SUBLANES = 8
V7X_MXU_DIM = 256

GRID_W = 64
N_MOD = 9
A_HEADS = 8
A_HEAD_DIM = 64
A_GROUPS = 2
A_STATE = 64
A_CONV = 5
B_GROUP = 16
B_STATE = 64
C_HEADS = 4
C_KEY = 64

TOKEN_TILES = (512, 384, 256)
MIX_CHUNK = 256
HG_CHUNK = 256
SCAN_UNROLL = 2
HG_SAMPLES_PER_STEP = 1
S5_CHUNK = 16
GRID_COLS_PER_TILE = 2 * V7X_SUBLANES


def _vmem_limit(estimate_bytes):
    return int(min(V7X_VMEM_BYTES - 6 * 1024 * 1024, max(estimate_bytes, 16 * 1024 * 1024)))


def _token_tile(n):
    for tm in TOKEN_TILES:
        if n % tm == 0:
            return tm
    raise ValueError(f"token count {n} has no supported tile")


def _silu(x):
    return x * jax.nn.sigmoid(x)


def _rms(x):
    return x * lax.rsqrt(jnp.mean(x * x, axis=-1, keepdims=True) + EPS)


def _split3(x):
    hi = x.astype(BF16)
    r = x - hi.astype(F32)
    mid = r.astype(BF16)
    lo = (r - mid.astype(F32)).astype(BF16)
    return hi, mid, lo


def _dot01_left(m01, x):
    hi, mid, lo = _split3(x)
    d = lambda a: jnp.dot(m01, a, preferred_element_type=F32)
    return (d(lo) + d(mid)) + d(hi)


def _dot01_right(x, m01):
    hi, mid, lo = _split3(x)
    d = lambda a: jnp.dot(a, m01, preferred_element_type=F32)
    return (d(lo) + d(mid)) + d(hi)


def _dot_nt(a, b):
    return lax.dot_general(a, b, (((1,), (1,)), ((), ())), preferred_element_type=F32)


def _mod_spec(mod):
    per_sample = mod.shape[0] > 1
    return pl.BlockSpec((1,) + mod.shape[1:], lambda b, t: (b if per_sample else 0, 0, 0))


def _mod_kernel(c_ref, w_ref, b_ref, o_ref):
    s = _silu(c_ref[...]).astype(BF16)
    o_ref[0] = jnp.dot(s, w_ref[0].astype(BF16), preferred_element_type=F32) + b_ref[0]


def _mod_vectors(cvec, mod_w, mod_b):
    depth, dm, nm = mod_w.shape
    rows = cvec.shape[0]
    tn = 1024
    return pl.pallas_call(
        _mod_kernel,
        grid=(depth, nm // tn),
        in_specs=[pl.BlockSpec((rows, dm), lambda l, j: (0, 0)),
                  pl.BlockSpec((1, dm, tn), lambda l, j: (l, 0, j)),
                  pl.BlockSpec((1, 1, tn), lambda l, j: (l, 0, j))],
        out_specs=pl.BlockSpec((1, rows, tn), lambda l, j: (l, 0, j)),
        out_shape=jax.ShapeDtypeStruct((depth, rows, nm), F32),
        compiler_params=pltpu.CompilerParams(dimension_semantics=("arbitrary", "arbitrary")),
        name="mod_vectors",
    )(cvec, mod_w, mod_b.reshape(depth, 1, nm))


def _ff_chunks(d_ff):
    if d_ff % V7X_MXU_DIM:
        return ((0, d_ff),)
    tiles = d_ff // V7X_MXU_DIM
    first = (tiles + 1) // 2 * V7X_MXU_DIM
    return ((0, first), (first, d_ff - first)) if d_ff > first else ((0, d_ff),)


def _ffn_kernel(h_ref, mod_ref, win_ref, wout_ref, fw_ref, *rest, base, d_ff, final, cols_in, cols_out, mixed):
    o_ref = rest[-1]
    if cols_in:
        x = jnp.concatenate([h_ref[0, :, wv, :] for wv in range(cols_in)], axis=0)
    else:
        x = h_ref[0]
    m = mod_ref[0]
    if mixed:
        ya_ref, yb_ref, yc_ref, wmix_ref = rest[:4]
        mix = jnp.concatenate([r[0].astype(BF16) for r in (ya_ref, yb_ref, yc_ref)], axis=1)
        x = x + m[base - 1:base] * jnp.dot(mix, wmix_ref[...], preferred_element_type=F32)
    u = (_rms(x) * (1.0 + m[base + 1:base + 2]) + m[base:base + 1]).astype(BF16)
    acc = jnp.zeros(x.shape, F32)
    for c0, cw in _ff_chunks(d_ff):
        g = jnp.dot(u, win_ref[:, c0:c0 + cw], preferred_element_type=F32)
        up = jnp.dot(u, win_ref[:, d_ff + c0:d_ff + c0 + cw], preferred_element_type=F32)
        a = (_silu(g) * up).astype(BF16)
        acc = acc + jnp.dot(a, wout_ref[c0:c0 + cw, :], preferred_element_type=F32)
    y = x + 0.5 * m[base + 2:base + 3] * acc
    if final:
        y = _rms(y) * fw_ref[...]
    if cols_out:
        nr = y.shape[0] // cols_out
        for wv in range(cols_out):
            o_ref[0, :, wv, :] = y[wv * nr:(wv + 1) * nr]
    else:
        o_ref[0] = y


def _ffn(h, mod, w_in, w_out, final_w, *, base, final, to_column=False, to_raster=False, mixed=None):
    bsz, lt, dm = h.shape
    d_ff = w_out.shape[0]
    grid_rows = lt // GRID_W
    cols = GRID_COLS_PER_TILE
    tm = grid_rows * cols if (to_column or to_raster) else _token_tile(lt)
    tok = lambda wd: pl.BlockSpec((1, tm, wd), lambda b, t: (b, t, 0))
    const = lambda a: pl.BlockSpec(a.shape, lambda b, t: (0, 0), pipeline_mode=pl.Buffered(1))
    grid_blk = pl.BlockSpec((1, grid_rows, cols, dm), lambda b, t: (b, 0, t, 0))
    kern = functools.partial(_ffn_kernel, base=base, d_ff=d_ff, final=final, mixed=mixed is not None,
                             cols_in=cols if to_column else 0, cols_out=cols if to_raster else 0)
    extra, extra_specs = (), []
    if mixed is not None:
        extra = tuple(mixed)
        extra_specs = [tok(a.shape[-1]) for a in mixed[:3]] + [const(mixed[3])]
    w_bytes = (w_in.size + w_out.size + (mixed[3].size if mixed is not None else 0)) * 2
    est = w_bytes + 8 * tm * dm * 4 + 3 * tm * d_ff * 4 + 6 * 1024 * 1024
    out = pl.pallas_call(
        kern,
        grid=(bsz, lt // tm),
        in_specs=[grid_blk if to_column else tok(dm), _mod_spec(mod), const(w_in), const(w_out),
                  pl.BlockSpec((1, dm), lambda b, t: (0, 0))] + extra_specs,
        out_specs=grid_blk if to_raster else tok(dm),
        out_shape=jax.ShapeDtypeStruct((bsz, grid_rows, GRID_W, dm) if to_raster else (bsz, lt, dm), F32),
        compiler_params=pltpu.CompilerParams(dimension_semantics=("arbitrary", "arbitrary"),
                                             vmem_limit_bytes=_vmem_limit(est)),
        name="ffn",
    )(h.reshape(bsz, grid_rows, GRID_W, dm) if to_column else h, mod, w_in, w_out, final_w, *extra)
    return out.reshape(bsz, lt, dm)


IN_SPLIT = ((512, BF16), (768, F32), (128, F32), (256, F32), (256, BF16), (512, F32), (256, BF16), (256, BF16))


def _inproj_kernel(h_ref, mod_ref, w_ref, *out_refs):
    m = mod_ref[0]
    u = (_rms(h_ref[0]) * (1.0 + m[4:5]) + m[3:4]).astype(BF16)
    p = jnp.dot(u, w_ref[...], preferred_element_type=F32)
    off = 0
    for ref, (wd, dt) in zip(out_refs, IN_SPLIT):
        ref[0] = p[:, off:off + wd].astype(dt)
        off += wd


def _inproj(h, mod, w):
    bsz, lt, dm = h.shape
    tm = _token_tile(lt)
    tok = lambda wd: pl.BlockSpec((1, tm, wd), lambda b, t: (b, t, 0))
    return pl.pallas_call(
        _inproj_kernel,
        grid=(bsz, lt // tm),
        in_specs=[tok(dm), _mod_spec(mod),
                  pl.BlockSpec(w.shape, lambda b, t: (0, 0), pipeline_mode=pl.Buffered(1))],
        out_specs=[tok(wd) for wd, _ in IN_SPLIT],
        out_shape=[jax.ShapeDtypeStruct((bsz, lt, wd), dt) for wd, dt in IN_SPLIT],
        compiler_params=pltpu.CompilerParams(dimension_semantics=("arbitrary", "arbitrary"),
                                             vmem_limit_bytes=_vmem_limit(40 * 1024 * 1024)),
        name="inproj",
    )(h, mod, w)


def _seq_spec(a):
    return pl.BlockSpec((1,) + a.shape[1:], lambda b: (b,) + (0,) * (a.ndim - 1))


def _full_spec(a):
    return pl.BlockSpec(a.shape, lambda b: (0,) * a.ndim)


def _ssd_kernel(zc_ref, zl_ref, xc_in_ref, xl_in_ref, dtc_ref, dtl_ref, cw_ref, cb_ref, dtb_ref, aneg_ref,
                dsk_ref, nw_ref, oc_ref, ol_ref, xconv_ref, yf_ref, yb_ref, st_ref, *, q):
    inner = A_HEADS * A_HEAD_DIM
    gw = A_STATE
    hpg = A_HEADS // A_GROUPS
    gl = hpg * A_HEAD_DIM
    n_ctx = oc_ref.shape[1]
    segs = ((zc_ref, xc_in_ref, dtc_ref, oc_ref, 0), (zl_ref, xl_in_ref, dtl_ref, ol_ref, n_ctx))

    cw = cw_ref[...]
    cb = cb_ref[...]
    for _, xin_ref, _, o_ref, base in segs:
        ls = o_ref.shape[1]
        n_chunks = ls // q

        def conv_body(i, carry, xin_ref=xin_ref, base=base, ls=ls, n_chunks=n_chunks):
            t0 = pl.multiple_of(i * q, q)
            cur = xin_ref[0, pl.ds(t0, q), :]
            p0 = pl.multiple_of(jnp.maximum(t0 - V7X_SUBLANES, 0), V7X_SUBLANES)
            n0 = pl.multiple_of(jnp.minimum(t0 + q, ls - V7X_SUBLANES), V7X_SUBLANES)
            prev = jnp.where(i == 0, 0.0, xin_ref[0, pl.ds(p0, V7X_SUBLANES), :])
            nxt = jnp.where(i == n_chunks - 1, 0.0, xin_ref[0, pl.ds(n0, V7X_SUBLANES), :])
            ext = jnp.concatenate([prev, cur, nxt], axis=0)
            acc = cb
            for k in range(A_CONV):
                s0 = V7X_SUBLANES - A_CONV // 2 + k
                acc = acc + cw[k:k + 1] * ext[s0:s0 + q]
            xconv_ref[pl.ds(pl.multiple_of(base + t0, q), q), :] = _silu(acc)
            return carry

        lax.fori_loop(0, n_chunks, conv_body, 0)

    st_ref[...] = jnp.zeros(st_ref.shape, F32)

    rowi = lax.broadcasted_iota(jnp.int32, (q, q), 0)
    coli = lax.broadcasted_iota(jnp.int32, (q, q), 1)
    er = lax.broadcasted_iota(jnp.int32, (V7X_LANES, inner), 0)
    ec = lax.broadcasted_iota(jnp.int32, (V7X_LANES, inner), 1)
    dtb = dtb_ref[...]
    aneg = aneg_ref[...]

    def load_chunk(seg, t0):
        dt_ref, base = seg[2], seg[4]
        xc = xconv_ref[pl.ds(pl.multiple_of(base + t0, q), q), :]
        dt = jax.nn.softplus(dt_ref[0, pl.ds(t0, q), :] + dtb)
        return xc[:, :inner], xc[:, inner:inner + A_GROUPS * gw], xc[:, inner + A_GROUPS * gw:], dt

    def scan_sum(d, d_a):
        tri = (rowi >= coli) if d == 0 else (rowi <= coli)
        return _dot01_left(jnp.where(tri, 1.0, 0.0).astype(BF16), d_a)

    def carried(d, xs, bm, cm, dt, cum):
        last = q - 1 if d == 0 else 0
        expand = jnp.where(er == d * A_HEADS + (ec >> int(math.log2(A_HEAD_DIM))), 1.0, 0.0).astype(BF16)
        e_hi, e_mid, _ = _split3(jnp.exp(cum))
        ecum = (jnp.dot(e_mid, expand, preferred_element_type=F32) + jnp.dot(e_hi, expand, preferred_element_type=F32))
        wcol = (jnp.exp(cum[last:last + 1, :] - cum) * dt).astype(BF16)
        xw = (xs * jnp.dot(wcol, expand, preferred_element_type=F32)).astype(BF16)
        e_last = ecum[last:last + 1, :]
        bm_t = bm.T
        ys = []
        for g in range(A_GROUPS):
            cg = cm[:, g * gw:(g + 1) * gw].astype(BF16)
            st = st_ref[d, g]
            ys.append(jnp.dot(cg, st.astype(BF16), preferred_element_type=F32) * ecum[:, g * gl:(g + 1) * gl])
            upd = jnp.dot(bm_t[g * gw:(g + 1) * gw].astype(BF16), xw[:, g * gl:(g + 1) * gl],
                          preferred_element_type=F32)
            st_ref[d, g] = st * e_last[:, g * gl:(g + 1) * gl] + upd
        return jnp.concatenate(ys, axis=1)

    lower = rowi > coli
    diag = rowi == coli

    def step(seg, i, n_chunks):
        o_ref = seg[3]
        t0 = pl.multiple_of(i * q, q)
        xs, bm, cm, dt = load_chunk(seg, t0)
        d_a = dt * aneg
        cum_f = scan_sum(0, d_a)
        cum_b = scan_sum(1, d_a)
        log_dt = jnp.log(dt)
        col_f, col_b = cum_f, cum_b
        row_f = (cum_f - log_dt).T
        row_b = (cum_b - log_dt).T
        dt_t = dt.T
        xs_bf = xs.astype(BF16)
        ys = []
        for g in range(A_GROUPS):
            cg = cm[:, g * gw:(g + 1) * gw].astype(BF16)
            bg = bm[:, g * gw:(g + 1) * gw].astype(BF16)
            gmat = _dot_nt(cg, bg)
            for hh in range(hpg):
                h = g * hpg + hh
                hb = A_HEADS + h
                arg = jnp.where(lower, col_f[:, h:h + 1] - row_f[h:h + 1, :], col_b[:, hb:hb + 1] - row_b[hb:hb + 1, :])
                wgt = jnp.where(diag, dt_t[h:h + 1, :] + dt_t[hb:hb + 1, :], jnp.exp(arg))
                sc = (gmat * wgt).astype(BF16)
                ys.append(jnp.dot(sc, xs_bf[:, h * A_HEAD_DIM:(h + 1) * A_HEAD_DIM], preferred_element_type=F32))
        yf_ref[pl.ds(pl.multiple_of(seg[4] + t0, q), q), :] = (jnp.concatenate(ys, axis=1)
                                                              + carried(0, xs, bm, cm, dt, cum_f))
        tb = pl.multiple_of((n_chunks - 1 - i) * q, q)
        xsb, bmb, cmb, dtb_ = load_chunk(seg, tb)
        yb_ref[pl.ds(pl.multiple_of(seg[4] + tb, q), q), :] = carried(1, xsb, bmb, cmb, dtb_, scan_sum(1, dtb_ * aneg))

    dsk = dsk_ref[...]
    nw = nw_ref[...]
    for seg in segs:
        n_chunks = seg[3].shape[1] // q
        lax.fori_loop(0, n_chunks, lambda i, carry, seg=seg, n_chunks=n_chunks: (step(seg, i, n_chunks), carry)[1], 0,
                      unroll=math.gcd(n_chunks, SCAN_UNROLL))

    for z_ref, _, _, o_ref, base in segs:
        def fin_body(i, carry, z_ref=z_ref, o_ref=o_ref, base=base):
            t0 = pl.multiple_of(i * q, q)
            xs = xconv_ref[pl.ds(pl.multiple_of(base + t0, q), q), :inner]
            rows = pl.ds(pl.multiple_of(base + t0, q), q)
            y = yf_ref[rows, :] + yb_ref[rows, :] + dsk * xs
            y = y * _silu(z_ref[0, pl.ds(t0, q), :].astype(F32))
            o_ref[0, pl.ds(t0, q), :] = (_rms(y) * nw).astype(o_ref.dtype)
            return carry

        lax.fori_loop(0, o_ref.shape[1] // q, fin_body, 0)


def _ssd(zc, zl, xc, xl, dtc, dtl, cw, cb, dtb, aneg, dsk, nw):
    bsz, n_ctx, inner = zc.shape
    seq = zl.shape[1]
    lt = n_ctx + seq
    cd = xc.shape[-1]
    ins = (zc, zl, xc, xl, dtc, dtl)
    consts = (cw, cb, dtb, aneg, dsk, nw)
    est = 2 * lt * (2 * inner + 4 * cd + 4 * V7X_LANES + 4 * inner) + 4 * lt * (cd + inner) + 12 * 1024 * 1024
    return pl.pallas_call(
        functools.partial(_ssd_kernel, q=MIX_CHUNK),
        grid=(bsz,),
        in_specs=[_seq_spec(a) for a in ins] + [_full_spec(a) for a in consts],
        out_specs=[pl.BlockSpec((1, n_ctx, inner), lambda b: (b, 0, 0)),
                   pl.BlockSpec((1, seq, inner), lambda b: (b, 0, 0))],
        out_shape=[jax.ShapeDtypeStruct((bsz, n_ctx, inner), BF16), jax.ShapeDtypeStruct((bsz, seq, inner), BF16)],
        scratch_shapes=[pltpu.VMEM((lt, cd), F32),
                        pltpu.VMEM((lt, inner), F32),
                        pltpu.VMEM((lt, inner), F32),
                        pltpu.VMEM((2, A_GROUPS, A_STATE, inner // A_GROUPS), F32)],
        compiler_params=pltpu.CompilerParams(dimension_semantics=("arbitrary",),
                                             vmem_limit_bytes=_vmem_limit(est)),
        name="ssd_mixer",
    )(*ins, *consts)


def _hgrn_kernel(qc_ref, ql_ref, ffc_ref, ffl_ref, fbc_ref, fbl_ref, vc_ref, vl_ref, gc_ref, gl_ref,
                 lb_ref, nw_ref, oc_ref, ol_ref, of_ref, ob_ref, st_ref, *, c):
    w = C_HEADS * C_KEY
    nv = c // V7X_SUBLANES
    segs = ((qc_ref, (ffc_ref, fbc_ref), vc_ref, gc_ref, oc_ref, 0),
            (ql_ref, (ffl_ref, fbl_ref), vl_ref, gl_ref, ol_ref, oc_ref.shape[1]))
    st_ref[...] = jnp.zeros(st_ref.shape, F32)

    rowi = lax.broadcasted_iota(jnp.int32, (c, c), 0)
    coli = lax.broadcasted_iota(jnp.int32, (c, c), 1)
    pair_bits = rowi ^ coli
    hr = lax.broadcasted_iota(jnp.int32, (w, w), 0)
    hc = lax.broadcasted_iota(jnp.int32, (w, w), 1)
    same_head = (hr >> int(math.log2(C_KEY))) == (hc >> int(math.log2(C_KEY)))
    bones = jnp.where(same_head, 1.0, 0.0).astype(BF16)
    trow = lax.broadcasted_iota(jnp.int32, (c, w), 0)
    sub3 = lax.broadcasted_iota(jnp.int32, (nv, V7X_SUBLANES, w), 1)

    def boundary_small(x3, p0, s):
        b = 1
        while b <= s:
            if p0 & b:
                x3 = jnp.where((sub3 & b) != 0, x3, pltpu.roll(x3, V7X_SUBLANES - b, axis=1))
            else:
                x3 = jnp.where((sub3 & b) != 0, pltpu.roll(x3, b, axis=1), x3)
            b *= 2
        return x3

    def gates(d, seg, smp, t0):
        zf = seg[1][d][smp, pl.ds(t0, c), :]
        lb = lb_ref[d:d + 1, :]
        f = lb + (1.0 - lb) * jax.nn.sigmoid(zf)
        causal = (rowi >= coli) if d == 0 else (rowi <= coli)
        cum = _dot01_left(jnp.where(causal, 1.0, 0.0).astype(BF16), jnp.log(f))
        return 1.0 - f, cum * LOG2E

    def boundary(cum, d, s):
        p0 = (s - 1) if d == 0 else s
        return boundary_small(cum.reshape(nv, V7X_SUBLANES, w), p0, s).reshape(c, w)

    def carried(d, smp, qs, kk, cum, v32):
        last = c - 1 if d == 0 else 0
        cum_last = cum[last:last + 1, :]
        st = st_ref[smp, d]
        qe = (qs * jnp.exp2(cum)).astype(BF16)
        kw = (kk * jnp.exp2(cum_last - cum)).astype(BF16)
        upd = jnp.dot(v32.T.astype(BF16), kw, preferred_element_type=F32)
        st_ref[smp, d] = jnp.where(same_head, st * jnp.exp2(cum_last) + upd, 0.0)
        return _dot_nt(qe, st.astype(BF16))

    def step(seg, smp, i, n_chunks):
        q_ref, _, v_ref, _, o_ref, base = seg
        t0 = pl.multiple_of(i * c, c)
        qs = _silu(q_ref[smp, pl.ds(t0, c), :].astype(F32))
        v = v_ref[smp, pl.ds(t0, c), :]
        v32 = v.astype(F32)
        kk0, cum0 = gates(0, seg, smp, t0)
        kk1, cum1 = gates(1, seg, smp, t0)

        out = jnp.dot((qs * (kk0 + kk1)).astype(BF16), bones, preferred_element_type=F32) * v32
        hc = c // 2
        quad = [[jnp.zeros((hc, hc), F32) for _ in range(2)] for _ in range(C_HEADS)]
        cross = [None] * C_HEADS
        s = 1
        while s < c:
            if s < V7X_SUBLANES:
                e0 = jnp.exp2(-jnp.abs(cum0 - boundary(cum0, 0, s)))
                e1 = jnp.exp2(-jnp.abs(cum1 - boundary(cum1, 1, s)))
                has_bit = (trow & s) != 0
                qm = (qs * jnp.where(has_bit, e0, e1)).astype(BF16)
                km = jnp.where(has_bit, kk1 * e1, kk0 * e0).astype(BF16)
            else:
                halves = lambda a: a.reshape(c // (2 * s), 2, s, w)
                c0, c1, q2 = halves(cum0), halves(cum1), halves(qs)
                x0 = c0 - c0[:, 0:1, s - 1:s, :]
                x1 = c1 - c1[:, 1:2, 0:1, :]
                qarg = jnp.concatenate([x1[:, 0:1], x0[:, 1:2]], axis=1)
                karg = jnp.concatenate([x0[:, 0:1], x1[:, 1:2]], axis=1)
                ksel = jnp.concatenate([halves(kk0)[:, 0:1], halves(kk1)[:, 1:2]], axis=1)
                qm = (q2 * jnp.exp2(qarg)).reshape(c, w).astype(BF16)
                km = (ksel * jnp.exp2(-karg)).reshape(c, w).astype(BF16)
            take = pair_bits[:hc, :hc] >= s
            for h in range(C_HEADS):
                sl = slice(h * C_KEY, (h + 1) * C_KEY)
                if s == hc:
                    cross[h] = (_dot_nt(qm[:hc, sl], km[hc:, sl]), _dot_nt(qm[hc:, sl], km[:hc, sl]))
                else:
                    for half in range(2):
                        rs = slice(half * hc, (half + 1) * hc)
                        quad[h][half] = jnp.where(take, _dot_nt(qm[rs, sl], km[rs, sl]), quad[h][half])
            s *= 2
        scores = [jnp.concatenate([jnp.concatenate([quad[h][0], cross[h][0]], axis=1),
                                   jnp.concatenate([cross[h][1], quad[h][1]], axis=1)], axis=0).astype(BF16)
                  for h in range(C_HEADS)]
        out = out + jnp.concatenate(
            [jnp.dot(scores[h], v[:, h * C_KEY:(h + 1) * C_KEY], preferred_element_type=F32)
             for h in range(C_HEADS)], axis=1)
        of_ref[smp, pl.ds(pl.multiple_of(base + t0, c), c), :] = out + carried(0, smp, qs, kk0, cum0, v32)
        tb = pl.multiple_of((n_chunks - 1 - i) * c, c)
        qsb = _silu(q_ref[smp, pl.ds(tb, c), :].astype(F32))
        kkb, cumb = gates(1, seg, smp, tb)
        ob_ref[smp, pl.ds(pl.multiple_of(base + tb, c), c), :] = carried(
            1, smp, qsb, kkb, cumb, v_ref[smp, pl.ds(tb, c), :].astype(F32))

    n_smp = oc_ref.shape[0]
    for seg in segs:
        n_chunks = seg[4].shape[1] // c

        def scan_body(i, carry, seg=seg, n_chunks=n_chunks):
            for smp in range(n_smp):
                step(seg, smp, i, n_chunks)
            return carry

        lax.fori_loop(0, n_chunks, scan_body, 0, unroll=math.gcd(n_chunks, SCAN_UNROLL))

    nw = nw_ref[...]
    for _, _, _, g_ref, o_ref, base in segs:
        def fin_body(i, carry, g_ref=g_ref, o_ref=o_ref, base=base):
            t0 = pl.multiple_of(i * c, c)
            for smp in range(n_smp):
                rows = pl.ds(pl.multiple_of(base + t0, c), c)
                o = of_ref[smp, rows, :] + ob_ref[smp, rows, :]
                ms = _dot01_right(o * o, bones) * (1.0 / C_KEY)
                y = o * lax.rsqrt(ms + EPS) * nw
                o_ref[smp, pl.ds(t0, c), :] = (y * _silu(g_ref[smp, pl.ds(t0, c), :].astype(F32))).astype(o_ref.dtype)
            return carry

        lax.fori_loop(0, o_ref.shape[1] // c, fin_body, 0)


def _hgrn(ctx_parts, lat_parts, lower, nw):
    qc, fc, ic, gc = ctx_parts
    ql, fl, il, gl = lat_parts
    bsz, n_ctx, w = qc.shape
    seq = ql.shape[1]
    ns = math.gcd(bsz, HG_SAMPLES_PER_STEP)
    blk = lambda a, j: pl.BlockSpec((ns, a.shape[1], w), lambda b: (b, 0, j))
    ins = (qc, ql, fc, fl, fc, fl, ic, il, gc, gl)
    specs = [blk(qc, 0), blk(ql, 0), blk(fc, 0), blk(fl, 0), blk(fc, 1), blk(fl, 1),
             blk(ic, 0), blk(il, 0), blk(gc, 0), blk(gl, 0)]
    est = ns * 2 * (n_ctx + seq) * w * (2 + 8 + 2 + 2 + 4) + 20 * 1024 * 1024
    return pl.pallas_call(
        functools.partial(_hgrn_kernel, c=HG_CHUNK),
        grid=(bsz // ns,),
        in_specs=specs + [_full_spec(lower), _full_spec(nw)],
        out_specs=[blk(qc, 0), blk(ql, 0)],
        out_shape=[jax.ShapeDtypeStruct((bsz, n_ctx, w), BF16), jax.ShapeDtypeStruct((bsz, seq, w), BF16)],
        scratch_shapes=[pltpu.VMEM((ns, n_ctx + seq, w), F32), pltpu.VMEM((ns, n_ctx + seq, w), F32),
                        pltpu.VMEM((ns, 2, w, w), F32)],
        compiler_params=pltpu.CompilerParams(dimension_semantics=("arbitrary",),
                                             vmem_limit_bytes=_vmem_limit(est)),
        name="hgrn_mixer",
    )(*ins, lower, nw)


def _s5_tables(lam_re, lam_im, log_step, b_re, b_im, c_re, c_im):
    q = S5_CHUNK
    hp = lax.Precision.HIGHEST
    lam_re = lam_re.astype(F32)
    lam_im = lam_im.astype(F32)
    step = jnp.exp(log_step.astype(F32))[..., None]
    tau = jnp.arange(q + 1, dtype=F32)[:, None, None, None]
    mag = jnp.exp(lam_re * step * tau)
    pr = mag * jnp.cos(lam_im * step * tau)
    pi = mag * jnp.sin(lam_im * step * tau)
    ar, ai = pr[1], pi[1]
    den = lam_re * lam_re + lam_im * lam_im
    nr = ar - 1.0
    kr = (nr * lam_re + ai * lam_im) / den
    ki = (ai * lam_re - nr * lam_im) / den
    b_re = b_re.astype(F32)
    b_im = b_im.astype(F32)
    br = kr[..., None] * b_re - ki[..., None] * b_im
    bi = kr[..., None] * b_im + ki[..., None] * b_re
    c_re = c_re.astype(F32)
    c_im = c_im.astype(F32)
    qr = pr[..., None] * br - pi[..., None] * bi
    qi = pr[..., None] * bi + pi[..., None] * br
    hk = jnp.einsum('dgon,tdgni->tdgoi', c_re, qr, precision=hp) - jnp.einsum('dgon,tdgni->tdgoi', c_im, qi, precision=hp)
    g, cg = b_re.shape[1], b_re.shape[3]
    width = q * cg
    strip = lambda h: h.transpose(1, 3, 0, 2).reshape(g, cg, width)
    zeros = jnp.zeros((g, cg, width - cg), F32)
    pad0 = jnp.concatenate([zeros, strip(hk[:q, 0])], axis=-1)
    pad1 = jnp.concatenate([strip(hk[:q, 1][::-1]), zeros], axis=-1)
    rows = lambda pad: jnp.stack([pad[..., (q - 1 - s) * cg:(q - 1 - s) * cg + width] for s in range(q)], axis=1)
    toe = jnp.stack([rows(pad0), rows(pad1)]).reshape(2, g, width, width)
    by_dir = lambda a: jnp.stack([a[:q, 0][::-1], a[:q, 1]])
    wst = jnp.concatenate([by_dir(qr), by_dir(qi)], axis=3)
    wst = wst.transpose(0, 2, 1, 4, 3).reshape(2, g, width, -1)
    out_pow = lambda a: jnp.stack([a[1:, 0], a[1:, 1][::-1]])
    po_r, po_i = out_pow(pr), out_pow(pi)
    w_xr = c_re[:, None] * po_r[:, :, :, None, :] - c_im[:, None] * po_i[:, :, :, None, :]
    w_xi = -c_re[:, None] * po_i[:, :, :, None, :] - c_im[:, None] * po_r[:, :, :, None, :]
    wout = jnp.concatenate([w_xr, w_xi], axis=-1)
    wout = wout.transpose(0, 2, 4, 1, 3).reshape(2, g, -1, width)
    dec = jnp.stack([jnp.concatenate([pr[q], pr[q]], axis=-1),
                     jnp.concatenate([-pi[q], pi[q]], axis=-1),
                     jnp.concatenate([pi[q], -pi[q]], axis=-1)], axis=1)
    return toe.astype(BF16), wst.astype(BF16), wout.astype(BF16), dec


def _lane_block_transpose(arrs, blk):
    n = len(arrs)
    rows, width = arrs[0].shape
    lane_blk = lax.broadcasted_iota(jnp.int32, (rows, width), 1) >> int(math.log2(blk))
    k = 1
    while k < n:
        hi_half = (lane_blk & k) != 0
        new = list(arrs)
        for a in range(n):
            if a & k == 0:
                lo_arr, hi_arr = arrs[a], arrs[a | k]
                new[a] = jnp.where(hi_half, pltpu.roll(hi_arr, blk * k, axis=1), lo_arr)
                new[a | k] = jnp.where(hi_half, hi_arr, pltpu.roll(lo_arr, width - blk * k, axis=1))
        arrs = new
        k *= 2
    return arrs


def _gelu_tanh(x):
    return 0.5 * x * (1.0 + jnp.tanh(math.sqrt(2.0 / math.pi) * (x + 0.044715 * (x * x * x))))


def _s5_kernel(uc_ref, ul_ref, toe_ref, wst_ref, wout_ref, dec_ref, sd_ref, gw_ref, gb_ref,
               oc_ref, ol_ref, z_ref, xin_ref):
    q = uc_ref.shape[2]
    ncc = uc_ref.shape[1]
    n_chunks = ncc + ul_ref.shape[1]
    ng = toe_ref.shape[1]
    half = dec_ref.shape[-1] // 2
    rows = [jnp.concatenate([uc_ref[0, :, s, :], ul_ref[0, :, s, :]], axis=0) for s in range(q)]
    ug = [a.astype(BF16) for a in _lane_block_transpose(rows, B_GROUP)]
    ys = [None] * ng
    for d in range(2):
        zs = []
        for g in range(ng):
            ys[g] = jnp.dot(ug[g], toe_ref[d, g], preferred_element_type=F32) + (0.0 if ys[g] is None else ys[g])
            zs.append(jnp.dot(ug[g], wst_ref[d, g], preferred_element_type=F32))
        z = pltpu.einshape("gcn->cgn", jnp.stack(zs))
        z_ref[0, d] = z
        z_ref[1, d] = pltpu.roll(z, half, axis=2)

    coef = [[dec_ref[d, j] for j in range(3)] for d in range(2)]

    def body(i, carry):
        nxt = []
        for d in range(2):
            if d == 0:
                c_idx = i
            else:
                c_idx = jnp.where(i < ncc, ncc - 1 - i, n_chunks - 1 - (i - ncc))
            x, xs = carry[2 * d], carry[2 * d + 1]
            a1, a2, a2s = coef[d]
            xin_ref[d, c_idx] = x
            nxt.append(a1 * x + a2 * xs + z_ref[0, d, c_idx])
            nxt.append(a1 * xs + a2s * x + z_ref[1, d, c_idx])
        return tuple(nxt)

    zero = jnp.zeros(coef[0][0].shape, F32)
    lax.fori_loop(0, n_chunks, body, (zero, zero, zero, zero), unroll=2)
    for d in range(2):
        xg = pltpu.einshape("cgn->gcn", xin_ref[d]).astype(BF16)
        for g in range(ng):
            ys[g] = ys[g] + jnp.dot(xg[g], wout_ref[d, g], preferred_element_type=F32)
    yt = _lane_block_transpose(ys, B_GROUP)
    sd = sd_ref[...]
    gw = gw_ref[...]
    gb = gb_ref[...]
    for t in range(q):
        y = _gelu_tanh(yt[t] + sd * rows[t])
        gate = jnp.dot(y.astype(BF16), gw, preferred_element_type=F32) + gb
        y = y * jax.nn.sigmoid(gate)
        oc_ref[0, :, t, :] = y[:ncc]
        ol_ref[0, :, t, :] = y[ncc:]


def _s5(pbc, pbl, toe, wst, wout, dec, s5_d, glu_w, glu_b):
    bsz, n_ctx, wd = pbc.shape
    seq = pbl.shape[1]
    q = S5_CHUNK
    ng = toe.shape[1]
    ns = wst.shape[-1]
    n_chunks = (n_ctx + seq) // q
    uc = pbc.reshape(bsz, n_ctx // q, q, wd)
    ul = pbl.reshape(bsz, seq // q, q, wd)
    consts = (toe, wst, wout, dec, s5_d, glu_w, glu_b)
    oc, ol = pl.pallas_call(
        _s5_kernel,
        grid=(bsz,),
        in_specs=[_seq_spec(uc), _seq_spec(ul)] + [_full_spec(a) for a in consts],
        out_specs=[_seq_spec(uc), _seq_spec(ul)],
        out_shape=[jax.ShapeDtypeStruct(uc.shape, F32), jax.ShapeDtypeStruct(ul.shape, F32)],
        scratch_shapes=[pltpu.VMEM((2, 2, n_chunks, ng, ns), F32), pltpu.VMEM((2, n_chunks, ng, ns), F32)],
        compiler_params=pltpu.CompilerParams(dimension_semantics=("arbitrary",),
                                             vmem_limit_bytes=_vmem_limit(40 * 1024 * 1024)),
        name="s5_mixer",
    )(uc, ul, *consts)
    return oc.reshape(bsz, n_ctx, wd), ol.reshape(bsz, seq, wd)


def _raster_to_column(t, rows):
    b, s, d = t.shape
    return t.reshape(b, rows, GRID_W, d).transpose(0, 2, 1, 3).reshape(b, s, d)


def _column_to_raster(t, rows):
    b, s, d = t.shape
    return t.reshape(b, GRID_W, rows, d).transpose(0, 2, 1, 3).reshape(b, s, d)


def kernel(x, c, ctx, c_ctx, mod_w, mod_b, ffn_w_in, ffn_w_out, w_in, w_out, a_conv_w, a_conv_b, a_dt_bias, a_log,
           a_d, a_norm_w, s5_lam_re, s5_lam_im, s5_log_step, s5_b_re, s5_b_im, s5_c_re, s5_c_im, s5_d, s5_glu_w,
           s5_glu_b, hg_lb_logits, hg_norm_w, final_norm_w):
    bsz, seq, dm = x.shape
    n_ctx = ctx.shape[1]
    depth = mod_w.shape[0]
    rows = seq // GRID_W
    assert n_ctx % MIX_CHUNK == 0 and seq % MIX_CHUNK == 0 and seq % GRID_W == 0
    a_inner = A_HEADS * A_HEAD_DIM
    a_conv_dim = a_conv_w.shape[-1]
    c_width = hg_norm_w.shape[-1]

    pad = (-(bsz + 1)) % V7X_SUBLANES
    cvec = jnp.concatenate([c, c_ctx[None, :], jnp.zeros((pad, dm), F32)], axis=0)
    mod_all = _mod_vectors(cvec, mod_w, mod_b)

    p_lb = jax.nn.softmax(hg_lb_logits.astype(F32), axis=0)
    lower_bounds = jnp.cumsum(p_lb, axis=0) - p_lb[:1]
    fw = final_norm_w.reshape(1, dm).astype(F32)
    toe_all, wst_all, wout_all, dec_all = jax.vmap(_s5_tables)(s5_lam_re, s5_lam_im, s5_log_step, s5_b_re, s5_b_im,
                                                               s5_c_re, s5_c_im)

    h_lat, h_ctx = x, ctx
    col_order = False
    for l in range(depth):
        last = l == depth - 1
        m_lat = mod_all[l, :bsz].reshape(bsz, N_MOD, dm)
        m_ctx = mod_all[l, bsz].reshape(1, N_MOD, dm)
        wi0, wo0 = ffn_w_in[l, 0].astype(BF16), ffn_w_out[l, 0].astype(BF16)
        wi1, wo1 = ffn_w_in[l, 1].astype(BF16), ffn_w_out[l, 1].astype(BF16)

        want_col = l % 2 == 1
        if col_order and not want_col:
            h_lat = _column_to_raster(h_lat, rows)
            col_order = False
        h_lat = _ffn(h_lat, m_lat, wi0, wo0, fw, base=0, final=False, to_column=want_col and not col_order)
        col_order = want_col
        h_ctx = _ffn(h_ctx, m_ctx, wi0, wo0, fw, base=0, final=False)

        wl = w_in[l]
        o_dt = a_inner + a_conv_dim
        o_b = o_dt + 2 * A_HEADS
        w_cat = jnp.concatenate([wl[:, :o_b], jnp.zeros((dm, V7X_LANES - 2 * A_HEADS), F32), wl[:, o_b:]],
                                axis=1).astype(BF16)
        zl, xl, dtl, pbl, ql, fl, il, gl = _inproj(h_lat, m_lat, w_cat)
        zc, xc, dtc, pbc, qc, fc, ic, gc = _inproj(h_ctx, m_ctx, w_cat)

        lane_pad = jnp.zeros((V7X_LANES - 2 * A_HEADS,), F32)
        dtb = jnp.concatenate([a_dt_bias[l].reshape(-1), lane_pad]).reshape(1, V7X_LANES)
        aneg = jnp.concatenate([-jnp.exp(a_log[l].astype(F32)).reshape(-1), lane_pad]).reshape(1, V7X_LANES)
        dsk = jnp.repeat(a_d[l].astype(F32), A_HEAD_DIM).reshape(1, a_inner)
        ya_c, ya_l = _ssd(zc, zl, xc, xl, dtc, dtl, a_conv_w[l], a_conv_b[l].reshape(1, -1), dtb, aneg, dsk,
                          a_norm_w[l].reshape(1, -1))

        yb_c, yb_l = _s5(pbc, pbl, toe_all[l], wst_all[l], wout_all[l], dec_all[l], s5_d[l].reshape(1, -1),
                         s5_glu_w[l].astype(BF16), s5_glu_b[l].reshape(1, -1))

        yc_c, yc_l = _hgrn((qc, fc, ic, gc), (ql, fl, il, gl), lower_bounds[l].astype(F32),
                           hg_norm_w[l].reshape(1, c_width))

        w_o = w_out[l].astype(BF16)
        h_lat = _ffn(h_lat, m_lat, wi1, wo1, fw, base=6, final=last, to_raster=last and col_order,
                     mixed=(ya_l, yb_l, yc_l, w_o))
        if last:
            col_order = False
        else:
            h_ctx = _ffn(h_ctx, m_ctx, wi1, wo1, fw, base=6, final=False, mixed=(ya_c, yb_c, yc_c, w_o))
    return h_lat
```

```python
import functools
import math

import jax
import jax.numpy as jnp
from jax import lax
from jax.experimental import pallas as pl
from jax.experimental.pallas import tpu as pltpu

F32 = jnp.float32
BF16 = jnp.bfloat16
EPS = 1e-6
NEG_LOG2E = -1.4426950408889634

V7X_VMEM_BYTES = 64 * 1024 * 1024
V7X_LANES = 128
V7X_SUBLANES = 8
V7X_MXU_DIM = 256

GRID_W = 64
N_MOD = 9
A_HEADS = 8
A_HEAD_DIM = 64
A_GROUPS = 2
A_STATE = 64
A_CONV = 5
B_GROUP = 16
B_STATE = 64
C_HEADS = 4
C_KEY = 64

TOKEN_TILES = (512, 384, 256)
MIX_CHUNK = 256
HG_CHUNK = 256
SCAN_UNROLL = 2
HG_SAMPLES_PER_STEP = 1
S5_CHUNK = 16
GRID_COLS_PER_TILE = 2 * V7X_SUBLANES


def _vmem_limit(estimate_bytes):
    return int(min(V7X_VMEM_BYTES - 6 * 1024 * 1024, max(estimate_bytes, 16 * 1024 * 1024)))


def _token_tile(n):
    for tm in TOKEN_TILES:
        if n % tm == 0:
            return tm
    raise ValueError(f"token count {n} has no supported tile")


def _silu(x):
    return x * jax.nn.sigmoid(x)


def _rms(x):
    return x * lax.rsqrt(jnp.mean(x * x, axis=-1, keepdims=True) + EPS)


def _split3(x):
    hi = x.astype(BF16)
    r = x - hi.astype(F32)
    mid = r.astype(BF16)
    lo = (r - mid.astype(F32)).astype(BF16)
    return hi, mid, lo


def _dot01_left(m01, x):
    hi, mid, lo = _split3(x)
    d = lambda a: jnp.dot(m01, a, preferred_element_type=F32)
    return (d(lo) + d(mid)) + d(hi)


def _dot01_right(x, m01):
    hi, mid, lo = _split3(x)
    d = lambda a: jnp.dot(a, m01, preferred_element_type=F32)
    return (d(lo) + d(mid)) + d(hi)


def _dot_nt(a, b):
    return lax.dot_general(a, b, (((1,), (1,)), ((), ())), preferred_element_type=F32)


def _mod_spec(mod):
    per_sample = mod.shape[0] > 1
    return pl.BlockSpec((1,) + mod.shape[1:], lambda b, t: (b if per_sample else 0, 0, 0))


def _mod_kernel(c_ref, w_ref, b_ref, o_ref):
    s = _silu(c_ref[...]).astype(BF16)
    o_ref[0] = jnp.dot(s, w_ref[0].astype(BF16), preferred_element_type=F32) + b_ref[0]


def _mod_vectors(cvec, mod_w, mod_b):
    depth, dm, nm = mod_w.shape
    rows = cvec.shape[0]
    tn = 1024
    return pl.pallas_call(
        _mod_kernel,
        grid=(depth, nm // tn),
        in_specs=[pl.BlockSpec((rows, dm), lambda l, j: (0, 0)),
                  pl.BlockSpec((1, dm, tn), lambda l, j: (l, 0, j)),
                  pl.BlockSpec((1, 1, tn), lambda l, j: (l, 0, j))],
        out_specs=pl.BlockSpec((1, rows, tn), lambda l, j: (l, 0, j)),
        out_shape=jax.ShapeDtypeStruct((depth, rows, nm), F32),
        compiler_params=pltpu.CompilerParams(dimension_semantics=("arbitrary", "arbitrary")),
        name="mod_vectors",
    )(cvec, mod_w, mod_b.reshape(depth, 1, nm))


def _ff_chunks(d_ff):
    if d_ff % V7X_MXU_DIM:
        return ((0, d_ff),)
    tiles = d_ff // V7X_MXU_DIM
    first = (tiles + 1) // 2 * V7X_MXU_DIM
    return ((0, first), (first, d_ff - first)) if d_ff > first else ((0, d_ff),)


def _ffn_kernel(h_ref, mod_ref, win_ref, wout_ref, fw_ref, *rest, base, d_ff, final, cols_in, cols_out, mixed):
    o_ref = rest[-1]
    if cols_in:
        x = jnp.concatenate([h_ref[0, :, wv, :] for wv in range(cols_in)], axis=0)
    else:
        x = h_ref[0]
    m = mod_ref[0]
    if mixed:
        ya_ref, yb_ref, yc_ref, wmix_ref = rest[:4]
        mix = jnp.concatenate([r[0].astype(BF16) for r in (ya_ref, yb_ref, yc_ref)], axis=1)
        x = x + m[base - 1:base] * jnp.dot(mix, wmix_ref[...], preferred_element_type=F32)
    u = (_rms(x) * (1.0 + m[base + 1:base + 2]) + m[base:base + 1]).astype(BF16)
    acc = jnp.zeros(x.shape, F32)
    for c0, cw in _ff_chunks(d_ff):
        g = jnp.dot(u, win_ref[:, c0:c0 + cw], preferred_element_type=F32)
        up = jnp.dot(u, win_ref[:, d_ff + c0:d_ff + c0 + cw], preferred_element_type=F32)
        a = (_silu(g) * up).astype(BF16)
        acc = acc + jnp.dot(a, wout_ref[c0:c0 + cw, :], preferred_element_type=F32)
    y = x + 0.5 * m[base + 2:base + 3] * acc
    if final:
        y = _rms(y) * fw_ref[...]
    if cols_out:
        nr = y.shape[0] // cols_out
        for wv in range(cols_out):
            o_ref[0, :, wv, :] = y[wv * nr:(wv + 1) * nr]
    else:
        o_ref[0] = y


def _ffn(h, mod, w_in, w_out, final_w, *, base, final, to_column=False, to_raster=False, mixed=None):
    bsz, lt, dm = h.shape
    d_ff = w_out.shape[0]
    grid_rows = lt // GRID_W
    cols = GRID_COLS_PER_TILE
    tm = grid_rows * cols if (to_column or to_raster) else _token_tile(lt)
    tok = lambda wd: pl.BlockSpec((1, tm, wd), lambda b, t: (b, t, 0))
    const = lambda a: pl.BlockSpec(a.shape, lambda b, t: (0, 0), pipeline_mode=pl.Buffered(1))
    grid_blk = pl.BlockSpec((1, grid_rows, cols, dm), lambda b, t: (b, 0, t, 0))
    kern = functools.partial(_ffn_kernel, base=base, d_ff=d_ff, final=final, mixed=mixed is not None,
                             cols_in=cols if to_column else 0, cols_out=cols if to_raster else 0)
    extra, extra_specs = (), []
    if mixed is not None:
        extra = tuple(mixed)
        extra_specs = [tok(a.shape[-1]) for a in mixed[:3]] + [const(mixed[3])]
    w_bytes = (w_in.size + w_out.size + (mixed[3].size if mixed is not None else 0)) * 2
    est = w_bytes + 8 * tm * dm * 4 + 3 * tm * d_ff * 4 + 6 * 1024 * 1024
    out = pl.pallas_call(
        kern,
        grid=(bsz, lt // tm),
        in_specs=[grid_blk if to_column else tok(dm), _mod_spec(mod), const(w_in), const(w_out),
                  pl.BlockSpec((1, dm), lambda b, t: (0, 0))] + extra_specs,
        out_specs=grid_blk if to_raster else tok(dm),
        out_shape=jax.ShapeDtypeStruct((bsz, grid_rows, GRID_W, dm) if to_raster else (bsz, lt, dm), F32),
        compiler_params=pltpu.CompilerParams(dimension_semantics=("arbitrary", "arbitrary"),
                                             vmem_limit_bytes=_vmem_limit(est)),
        name="ffn",
    )(h.reshape(bsz, grid_rows, GRID_W, dm) if to_column else h, mod, w_in, w_out, final_w, *extra)
    return out.reshape(bsz, lt, dm)


IN_SPLIT = ((512, BF16), (768, F32), (128, F32), (256, F32), (256, BF16), (512, F32), (256, BF16), (256, BF16))


def _inproj_kernel(h_ref, mod_ref, w_ref, *out_refs):
    m = mod_ref[0]
    u = (_rms(h_ref[0]) * (1.0 + m[4:5]) + m[3:4]).astype(BF16)
    p = jnp.dot(u, w_ref[...], preferred_element_type=F32)
    off = 0
    for ref, (wd, dt) in zip(out_refs, IN_SPLIT):
        ref[0] = p[:, off:off + wd].astype(dt)
        off += wd


def _inproj(h, mod, w):
    bsz, lt, dm = h.shape
    tm = _token_tile(lt)
    tok = lambda wd: pl.BlockSpec((1, tm, wd), lambda b, t: (b, t, 0))
    return pl.pallas_call(
        _inproj_kernel,
        grid=(bsz, lt // tm),
        in_specs=[tok(dm), _mod_spec(mod),
                  pl.BlockSpec(w.shape, lambda b, t: (0, 0), pipeline_mode=pl.Buffered(1))],
        out_specs=[tok(wd) for wd, _ in IN_SPLIT],
        out_shape=[jax.ShapeDtypeStruct((bsz, lt, wd), dt) for wd, dt in IN_SPLIT],
        compiler_params=pltpu.CompilerParams(dimension_semantics=("arbitrary", "arbitrary"),
                                             vmem_limit_bytes=_vmem_limit(40 * 1024 * 1024)),
        name="inproj",
    )(h, mod, w)


def _seq_spec(a):
    return pl.BlockSpec((1,) + a.shape[1:], lambda b: (b,) + (0,) * (a.ndim - 1))


def _full_spec(a):
    return pl.BlockSpec(a.shape, lambda b: (0,) * a.ndim)


def _ssd_kernel(zc_ref, zl_ref, xc_in_ref, xl_in_ref, dtc_ref, dtl_ref, cw_ref, cb_ref, dtb_ref, aneg_ref,
                dsk_ref, nw_ref, oc_ref, ol_ref, xconv_ref, yf_ref, yb_ref, st_ref, *, q):
    inner = A_HEADS * A_HEAD_DIM
    gw = A_STATE
    hpg = A_HEADS // A_GROUPS
    gl = hpg * A_HEAD_DIM
    n_ctx = oc_ref.shape[1]
    segs = ((zc_ref, xc_in_ref, dtc_ref, oc_ref, 0), (zl_ref, xl_in_ref, dtl_ref, ol_ref, n_ctx))

    cw = cw_ref[...]
    cb = cb_ref[...]
    for _, xin_ref, _, o_ref, base in segs:
        ls = o_ref.shape[1]
        n_chunks = ls // q

        def conv_body(i, carry, xin_ref=xin_ref, base=base, ls=ls, n_chunks=n_chunks):
            t0 = pl.multiple_of(i * q, q)
            cur = xin_ref[0, pl.ds(t0, q), :]
            p0 = pl.multiple_of(jnp.maximum(t0 - V7X_SUBLANES, 0), V7X_SUBLANES)
            n0 = pl.multiple_of(jnp.minimum(t0 + q, ls - V7X_SUBLANES), V7X_SUBLANES)
            prev = jnp.where(i == 0, 0.0, xin_ref[0, pl.ds(p0, V7X_SUBLANES), :])
            nxt = jnp.where(i == n_chunks - 1, 0.0, xin_ref[0, pl.ds(n0, V7X_SUBLANES), :])
            ext = jnp.concatenate([prev, cur, nxt], axis=0)
            acc = cb
            for k in range(A_CONV):
                s0 = V7X_SUBLANES - A_CONV // 2 + k
                acc = acc + cw[k:k + 1] * ext[s0:s0 + q]
            xconv_ref[pl.ds(pl.multiple_of(base + t0, q), q), :] = _silu(acc)
            return carry

        lax.fori_loop(0, n_chunks, conv_body, 0)

    st_ref[...] = jnp.zeros(st_ref.shape, F32)

    rowi = lax.broadcasted_iota(jnp.int32, (q, q), 0)
    coli = lax.broadcasted_iota(jnp.int32, (q, q), 1)
    er = lax.broadcasted_iota(jnp.int32, (V7X_LANES, inner), 0)
    ec = lax.broadcasted_iota(jnp.int32, (V7X_LANES, inner), 1)
    dtb = dtb_ref[...]
    aneg = aneg_ref[...]

    def load_chunk(seg, t0):
        dt_ref, base = seg[2], seg[4]
        xc = xconv_ref[pl.ds(pl.multiple_of(base + t0, q), q), :]
        dt = jax.nn.softplus(dt_ref[0, pl.ds(t0, q), :] + dtb)
        return xc[:, :inner], xc[:, inner:inner + A_GROUPS * gw], xc[:, inner + A_GROUPS * gw:], dt

    def scan_sum(d, d_a):
        tri = (rowi >= coli) if d == 0 else (rowi <= coli)
        return _dot01_left(jnp.where(tri, 1.0, 0.0).astype(BF16), d_a)

    def carried(d, xs, bm, cm, dt, cum):
        last = q - 1 if d == 0 else 0
        expand = jnp.where(er == d * A_HEADS + (ec >> int(math.log2(A_HEAD_DIM))), 1.0, 0.0).astype(BF16)
        e_hi, e_mid, _ = _split3(jnp.exp(cum))
        ecum = (jnp.dot(e_mid, expand, preferred_element_type=F32) + jnp.dot(e_hi, expand, preferred_element_type=F32))
        wcol = (jnp.exp(cum[last:last + 1, :] - cum) * dt).astype(BF16)
        xw = (xs * jnp.dot(wcol, expand, preferred_element_type=F32)).astype(BF16)
        e_last = ecum[last:last + 1, :]
        bm_t = bm.T
        ys = []
        for g in range(A_GROUPS):
            cg = cm[:, g * gw:(g + 1) * gw].astype(BF16)
            st = st_ref[d, g]
            ys.append(jnp.dot(cg, st.astype(BF16), preferred_element_type=F32) * ecum[:, g * gl:(g + 1) * gl])
            upd = jnp.dot(bm_t[g * gw:(g + 1) * gw].astype(BF16), xw[:, g * gl:(g + 1) * gl],
                          preferred_element_type=F32)
            st_ref[d, g] = st * e_last[:, g * gl:(g + 1) * gl] + upd
        return jnp.concatenate(ys, axis=1)

    lower = rowi > coli
    diag = rowi == coli

    def step(seg, i, n_chunks):
        o_ref = seg[3]
        t0 = pl.multiple_of(i * q, q)
        xs, bm, cm, dt = load_chunk(seg, t0)
        d_a = dt * aneg
        cum_f = scan_sum(0, d_a)
        cum_b = scan_sum(1, d_a)
        log_dt = jnp.log(dt)
        col_f, col_b = cum_f, cum_b
        row_f = (cum_f - log_dt).T
        row_b = (cum_b - log_dt).T
        dt_t = dt.T
        xs_bf = xs.astype(BF16)
        ys = []
        for g in range(A_GROUPS):
            cg = cm[:, g * gw:(g + 1) * gw].astype(BF16)
            bg = bm[:, g * gw:(g + 1) * gw].astype(BF16)
            gmat = _dot_nt(cg, bg)
            for hh in range(hpg):
                h = g * hpg + hh
                hb = A_HEADS + h
                arg = jnp.where(lower, col_f[:, h:h + 1] - row_f[h:h + 1, :], col_b[:, hb:hb + 1] - row_b[hb:hb + 1, :])
                wgt = jnp.where(diag, dt_t[h:h + 1, :] + dt_t[hb:hb + 1, :], jnp.exp(arg))
                sc = (gmat * wgt).astype(BF16)
                ys.append(jnp.dot(sc, xs_bf[:, h * A_HEAD_DIM:(h + 1) * A_HEAD_DIM], preferred_element_type=F32))
        yf_ref[pl.ds(pl.multiple_of(seg[4] + t0, q), q), :] = (jnp.concatenate(ys, axis=1)
                                                              + carried(0, xs, bm, cm, dt, cum_f))
        tb = pl.multiple_of((n_chunks - 1 - i) * q, q)
        xsb, bmb, cmb, dtb_ = load_chunk(seg, tb)
        yb_ref[pl.ds(pl.multiple_of(seg[4] + tb, q), q), :] = carried(1, xsb, bmb, cmb, dtb_, scan_sum(1, dtb_ * aneg))

    dsk = dsk_ref[...]
    nw = nw_ref[...]
    for seg in segs:
        n_chunks = seg[3].shape[1] // q
        lax.fori_loop(0, n_chunks, lambda i, carry, seg=seg, n_chunks=n_chunks: (step(seg, i, n_chunks), carry)[1], 0,
                      unroll=math.gcd(n_chunks, SCAN_UNROLL))

    for z_ref, _, _, o_ref, base in segs:
        def fin_body(i, carry, z_ref=z_ref, o_ref=o_ref, base=base):
            t0 = pl.multiple_of(i * q, q)
            xs = xconv_ref[pl.ds(pl.multiple_of(base + t0, q), q), :inner]
            rows = pl.ds(pl.multiple_of(base + t0, q), q)
            y = yf_ref[rows, :] + yb_ref[rows, :] + dsk * xs
            y = y * _silu(z_ref[0, pl.ds(t0, q), :].astype(F32))
            o_ref[0, pl.ds(t0, q), :] = (_rms(y) * nw).astype(o_ref.dtype)
            return carry

        lax.fori_loop(0, o_ref.shape[1] // q, fin_body, 0)


def _ssd(zc, zl, xc, xl, dtc, dtl, cw, cb, dtb, aneg, dsk, nw):
    bsz, n_ctx, inner = zc.shape
    seq = zl.shape[1]
    lt = n_ctx + seq
    cd = xc.shape[-1]
    ins = (zc, zl, xc, xl, dtc, dtl)
    consts = (cw, cb, dtb, aneg, dsk, nw)
    est = 2 * lt * (2 * inner + 4 * cd + 4 * V7X_LANES + 4 * inner) + 4 * lt * (cd + inner) + 12 * 1024 * 1024
    return pl.pallas_call(
        functools.partial(_ssd_kernel, q=MIX_CHUNK),
        grid=(bsz,),
        in_specs=[_seq_spec(a) for a in ins] + [_full_spec(a) for a in consts],
        out_specs=[pl.BlockSpec((1, n_ctx, inner), lambda b: (b, 0, 0)),
                   pl.BlockSpec((1, seq, inner), lambda b: (b, 0, 0))],
        out_shape=[jax.ShapeDtypeStruct((bsz, n_ctx, inner), BF16), jax.ShapeDtypeStruct((bsz, seq, inner), BF16)],
        scratch_shapes=[pltpu.VMEM((lt, cd), F32),
                        pltpu.VMEM((lt, inner), F32),
                        pltpu.VMEM((lt, inner), F32),
                        pltpu.VMEM((2, A_GROUPS, A_STATE, inner // A_GROUPS), F32)],
        compiler_params=pltpu.CompilerParams(dimension_semantics=("arbitrary",),
                                             vmem_limit_bytes=_vmem_limit(est)),
        name="ssd_mixer",
    )(*ins, *consts)


def _hgrn_kernel(qc_ref, ql_ref, ffc_ref, ffl_ref, fbc_ref, fbl_ref, vc_ref, vl_ref, gc_ref, gl_ref,
                 lb_ref, nw_ref, oc_ref, ol_ref, of_ref, ob_ref, st_ref, *, c):
    w = C_HEADS * C_KEY
    nv = c // V7X_SUBLANES
    segs = ((qc_ref, (ffc_ref, fbc_ref), vc_ref, gc_ref, oc_ref, 0),
            (ql_ref, (ffl_ref, fbl_ref), vl_ref, gl_ref, ol_ref, oc_ref.shape[1]))
    st_ref[...] = jnp.zeros(st_ref.shape, F32)

    rowi = lax.broadcasted_iota(jnp.int32, (c, c), 0)
    coli = lax.broadcasted_iota(jnp.int32, (c, c), 1)
    pair_bits = rowi ^ coli
    hr = lax.broadcasted_iota(jnp.int32, (w, w), 0)
    hc = lax.broadcasted_iota(jnp.int32, (w, w), 1)
    same_head = (hr >> int(math.log2(C_KEY))) == (hc >> int(math.log2(C_KEY)))
    bones = jnp.where(same_head, 1.0, 0.0).astype(BF16)
    trow = lax.broadcasted_iota(jnp.int32, (c, w), 0)
    sub3 = lax.broadcasted_iota(jnp.int32, (nv, V7X_SUBLANES, w), 1)

    def boundary_small(x3, p0, s):
        b = 1
        while b <= s:
            if p0 & b:
                x3 = jnp.where((sub3 & b) != 0, x3, pltpu.roll(x3, V7X_SUBLANES - b, axis=1))
            else:
                x3 = jnp.where((sub3 & b) != 0, pltpu.roll(x3, b, axis=1), x3)
            b *= 2
        return x3

    def gates(d, seg, smp, t0):
        zf = seg[1][d][smp, pl.ds(t0, c), :]
        lb = lb_ref[d:d + 1, :]
        f = lb + (1.0 - lb) * jax.nn.sigmoid(zf)
        causal = (rowi >= coli) if d == 0 else (rowi <= coli)
        cum = _dot01_left(jnp.where(causal, 1.0, 0.0).astype(BF16), jnp.log(f))
        return 1.0 - f, cum

    def boundary(cum, d, s):
        p0 = (s - 1) if d == 0 else s
        return boundary_small(cum.reshape(nv, V7X_SUBLANES, w), p0, s).reshape(c, w)

    def carried(d, smp, qs, kk, cum, v32):
        last = c - 1 if d == 0 else 0
        cum_last = cum[last:last + 1, :]
        st = st_ref[smp, d]
        qe = (qs * jnp.exp(cum)).astype(BF16)
        kw = (kk * jnp.exp(cum_last - cum)).astype(BF16)
        upd = jnp.dot(v32.T.astype(BF16), kw, preferred_element_type=F32)
        st_ref[smp, d] = jnp.where(same_head, st * jnp.exp(cum_last) + upd, 0.0)
        return _dot_nt(qe, st.astype(BF16))

    def step(seg, smp, i, n_chunks):
        q_ref, _, v_ref, _, o_ref, base = seg
        t0 = pl.multiple_of(i * c, c)
        qs = _silu(q_ref[smp, pl.ds(t0, c), :].astype(F32))
        v = v_ref[smp, pl.ds(t0, c), :]
        v32 = v.astype(F32)
        kk0, cum0 = gates(0, seg, smp, t0)
        kk1, cum1 = gates(1, seg, smp, t0)

        out = jnp.dot((qs * (kk0 + kk1)).astype(BF16), bones, preferred_element_type=F32) * v32
        scores = [jnp.zeros((c, c), F32) for _ in range(C_HEADS)]
        s = 1
        while s < c:
            if s < V7X_SUBLANES:
                e0 = jnp.exp2(jnp.abs(cum0 - boundary(cum0, 0, s)) * NEG_LOG2E)
                e1 = jnp.exp2(jnp.abs(cum1 - boundary(cum1, 1, s)) * NEG_LOG2E)
                has_bit = (trow & s) != 0
                qm = (qs * jnp.where(has_bit, e0, e1)).astype(BF16)
                km = jnp.where(has_bit, kk1 * e1, kk0 * e0).astype(BF16)
            else:
                halves = lambda a: a.reshape(c // (2 * s), 2, s, w)
                c0, c1, q2 = halves(cum0), halves(cum1), halves(qs)
                x0 = c0 - c0[:, 0:1, s - 1:s, :]
                x1 = c1 - c1[:, 1:2, 0:1, :]
                qarg = jnp.concatenate([x1[:, 0:1], x0[:, 1:2]], axis=1)
                karg = jnp.concatenate([x0[:, 0:1], x1[:, 1:2]], axis=1)
                ksel = jnp.concatenate([halves(kk0)[:, 0:1], halves(kk1)[:, 1:2]], axis=1)
                qm = (q2 * jnp.exp2(qarg * (-NEG_LOG2E))).reshape(c, w).astype(BF16)
                km = (ksel * jnp.exp2(karg * NEG_LOG2E)).reshape(c, w).astype(BF16)
            take = pair_bits >= s
            for h in range(C_HEADS):
                sl = slice(h * C_KEY, (h + 1) * C_KEY)
                scores[h] = jnp.where(take, _dot_nt(qm[:, sl], km[:, sl]), scores[h])
            s *= 2
        out = out + jnp.concatenate(
            [jnp.dot(scores[h].astype(BF16), v[:, h * C_KEY:(h + 1) * C_KEY], preferred_element_type=F32)
             for h in range(C_HEADS)], axis=1)
        of_ref[smp, pl.ds(pl.multiple_of(base + t0, c), c), :] = out + carried(0, smp, qs, kk0, cum0, v32)
        tb = pl.multiple_of((n_chunks - 1 - i) * c, c)
        qsb = _silu(q_ref[smp, pl.ds(tb, c), :].astype(F32))
        kkb, cumb = gates(1, seg, smp, tb)
        ob_ref[smp, pl.ds(pl.multiple_of(base + tb, c), c), :] = carried(
            1, smp, qsb, kkb, cumb, v_ref[smp, pl.ds(tb, c), :].astype(F32))

    n_smp = oc_ref.shape[0]
    for seg in segs:
        n_chunks = seg[4].shape[1] // c

        def scan_body(i, carry, seg=seg, n_chunks=n_chunks):
            for smp in range(n_smp):
                step(seg, smp, i, n_chunks)
            return carry

        lax.fori_loop(0, n_chunks, scan_body, 0, unroll=math.gcd(n_chunks, SCAN_UNROLL))

    nw = nw_ref[...]
    for _, _, _, g_ref, o_ref, base in segs:
        def fin_body(i, carry, g_ref=g_ref, o_ref=o_ref, base=base):
            t0 = pl.multiple_of(i * c, c)
            for smp in range(n_smp):
                rows = pl.ds(pl.multiple_of(base + t0, c), c)
                o = of_ref[smp, rows, :] + ob_ref[smp, rows, :]
                ms = _dot01_right(o * o, bones) * (1.0 / C_KEY)
                y = o * lax.rsqrt(ms + EPS) * nw
                o_ref[smp, pl.ds(t0, c), :] = (y * _silu(g_ref[smp, pl.ds(t0, c), :].astype(F32))).astype(o_ref.dtype)
            return carry

        lax.fori_loop(0, o_ref.shape[1] // c, fin_body, 0)


def _hgrn(ctx_parts, lat_parts, lower, nw):
    qc, fc, ic, gc = ctx_parts
    ql, fl, il, gl = lat_parts
    bsz, n_ctx, w = qc.shape
    seq = ql.shape[1]
    ns = math.gcd(bsz, HG_SAMPLES_PER_STEP)
    blk = lambda a, j: pl.BlockSpec((ns, a.shape[1], w), lambda b: (b, 0, j))
    ins = (qc, ql, fc, fl, fc, fl, ic, il, gc, gl)
    specs = [blk(qc, 0), blk(ql, 0), blk(fc, 0), blk(fl, 0), blk(fc, 1), blk(fl, 1),
             blk(ic, 0), blk(il, 0), blk(gc, 0), blk(gl, 0)]
    est = ns * 2 * (n_ctx + seq) * w * (2 + 8 + 2 + 2 + 4) + 20 * 1024 * 1024
    return pl.pallas_call(
        functools.partial(_hgrn_kernel, c=HG_CHUNK),
        grid=(bsz // ns,),
        in_specs=specs + [_full_spec(lower), _full_spec(nw)],
        out_specs=[blk(qc, 0), blk(ql, 0)],
        out_shape=[jax.ShapeDtypeStruct((bsz, n_ctx, w), BF16), jax.ShapeDtypeStruct((bsz, seq, w), BF16)],
        scratch_shapes=[pltpu.VMEM((ns, n_ctx + seq, w), F32), pltpu.VMEM((ns, n_ctx + seq, w), F32),
                        pltpu.VMEM((ns, 2, w, w), F32)],
        compiler_params=pltpu.CompilerParams(dimension_semantics=("arbitrary",),
                                             vmem_limit_bytes=_vmem_limit(est)),
        name="hgrn_mixer",
    )(*ins, lower, nw)


def _s5_tables(lam_re, lam_im, log_step, b_re, b_im, c_re, c_im):
    q = S5_CHUNK
    hp = lax.Precision.HIGHEST
    lam_re = lam_re.astype(F32)
    lam_im = lam_im.astype(F32)
    step = jnp.exp(log_step.astype(F32))[..., None]
    tau = jnp.arange(q + 1, dtype=F32)[:, None, None, None]
    mag = jnp.exp(lam_re * step * tau)
    pr = mag * jnp.cos(lam_im * step * tau)
    pi = mag * jnp.sin(lam_im * step * tau)
    ar, ai = pr[1], pi[1]
    den = lam_re * lam_re + lam_im * lam_im
    nr = ar - 1.0
    kr = (nr * lam_re + ai * lam_im) / den
    ki = (ai * lam_re - nr * lam_im) / den
    b_re = b_re.astype(F32)
    b_im = b_im.astype(F32)
    br = kr[..., None] * b_re - ki[..., None] * b_im
    bi = kr[..., None] * b_im + ki[..., None] * b_re
    c_re = c_re.astype(F32)
    c_im = c_im.astype(F32)
    qr = pr[..., None] * br - pi[..., None] * bi
    qi = pr[..., None] * bi + pi[..., None] * br
    hk = jnp.einsum('dgon,tdgni->tdgoi', c_re, qr, precision=hp) - jnp.einsum('dgon,tdgni->tdgoi', c_im, qi, precision=hp)
    g, cg = b_re.shape[1], b_re.shape[3]
    width = q * cg
    strip = lambda h: h.transpose(1, 3, 0, 2).reshape(g, cg, width)
    zeros = jnp.zeros((g, cg, width - cg), F32)
    pad0 = jnp.concatenate([zeros, strip(hk[:q, 0])], axis=-1)
    pad1 = jnp.concatenate([strip(hk[:q, 1][::-1]), zeros], axis=-1)
    rows = lambda pad: jnp.stack([pad[..., (q - 1 - s) * cg:(q - 1 - s) * cg + width] for s in range(q)], axis=1)
    toe = jnp.stack([rows(pad0), rows(pad1)]).reshape(2, g, width, width)
    by_dir = lambda a: jnp.stack([a[:q, 0][::-1], a[:q, 1]])
    wst = jnp.concatenate([by_dir(qr), by_dir(qi)], axis=3)
    wst = wst.transpose(0, 2, 1, 4, 3).reshape(2, g, width, -1)
    out_pow = lambda a: jnp.stack([a[1:, 0], a[1:, 1][::-1]])
    po_r, po_i = out_pow(pr), out_pow(pi)
    w_xr = c_re[:, None] * po_r[:, :, :, None, :] - c_im[:, None] * po_i[:, :, :, None, :]
    w_xi = -c_re[:, None] * po_i[:, :, :, None, :] - c_im[:, None] * po_r[:, :, :, None, :]
    wout = jnp.concatenate([w_xr, w_xi], axis=-1)
    wout = wout.transpose(0, 2, 4, 1, 3).reshape(2, g, -1, width)
    dec = jnp.stack([jnp.concatenate([pr[q], pr[q]], axis=-1),
                     jnp.concatenate([-pi[q], pi[q]], axis=-1),
                     jnp.concatenate([pi[q], -pi[q]], axis=-1)], axis=1)
    return toe.astype(BF16), wst.astype(BF16), wout.astype(BF16), dec


def _lane_block_transpose(arrs, blk):
    n = len(arrs)
    rows, width = arrs[0].shape
    lane_blk = lax.broadcasted_iota(jnp.int32, (rows, width), 1) >> int(math.log2(blk))
    k = 1
    while k < n:
        hi_half = (lane_blk & k) != 0
        new = list(arrs)
        for a in range(n):
            if a & k == 0:
                lo_arr, hi_arr = arrs[a], arrs[a | k]
                new[a] = jnp.where(hi_half, pltpu.roll(hi_arr, blk * k, axis=1), lo_arr)
                new[a | k] = jnp.where(hi_half, hi_arr, pltpu.roll(lo_arr, width - blk * k, axis=1))
        arrs = new
        k *= 2
    return arrs


def _gelu_tanh(x):
    return 0.5 * x * (1.0 + jnp.tanh(math.sqrt(2.0 / math.pi) * (x + 0.044715 * (x * x * x))))


def _s5_kernel(uc_ref, ul_ref, toe_ref, wst_ref, wout_ref, dec_ref, sd_ref, gw_ref, gb_ref,
               oc_ref, ol_ref, z_ref, xin_ref):
    q = uc_ref.shape[2]
    ncc = uc_ref.shape[1]
    n_chunks = ncc + ul_ref.shape[1]
    ng = toe_ref.shape[1]
    half = dec_ref.shape[-1] // 2
    rows = [jnp.concatenate([uc_ref[0, :, s, :], ul_ref[0, :, s, :]], axis=0) for s in range(q)]
    ug = [a.astype(BF16) for a in _lane_block_transpose(rows, B_GROUP)]
    ys = [None] * ng
    for d in range(2):
        zs = []
        for g in range(ng):
            ys[g] = jnp.dot(ug[g], toe_ref[d, g], preferred_element_type=F32) + (0.0 if ys[g] is None else ys[g])
            zs.append(jnp.dot(ug[g], wst_ref[d, g], preferred_element_type=F32))
        z = pltpu.einshape("gcn->cgn", jnp.stack(zs))
        z_ref[0, d] = z
        z_ref[1, d] = pltpu.roll(z, half, axis=2)

    coef = [[dec_ref[d, j] for j in range(3)] for d in range(2)]

    def body(i, carry):
        nxt = []
        for d in range(2):
            if d == 0:
                c_idx = i
            else:
                c_idx = jnp.where(i < ncc, ncc - 1 - i, n_chunks - 1 - (i - ncc))
            x, xs = carry[2 * d], carry[2 * d + 1]
            a1, a2, a2s = coef[d]
            xin_ref[d, c_idx] = x
            nxt.append(a1 * x + a2 * xs + z_ref[0, d, c_idx])
            nxt.append(a1 * xs + a2s * x + z_ref[1, d, c_idx])
        return tuple(nxt)

    zero = jnp.zeros(coef[0][0].shape, F32)
    lax.fori_loop(0, n_chunks, body, (zero, zero, zero, zero), unroll=2)
    for d in range(2):
        xg = pltpu.einshape("cgn->gcn", xin_ref[d]).astype(BF16)
        for g in range(ng):
            ys[g] = ys[g] + jnp.dot(xg[g], wout_ref[d, g], preferred_element_type=F32)
    yt = _lane_block_transpose(ys, B_GROUP)
    sd = sd_ref[...]
    gw = gw_ref[...]
    gb = gb_ref[...]
    for t in range(q):
        y = _gelu_tanh(yt[t] + sd * rows[t])
        gate = jnp.dot(y.astype(BF16), gw, preferred_element_type=F32) + gb
        y = y * jax.nn.sigmoid(gate)
        oc_ref[0, :, t, :] = y[:ncc]
        ol_ref[0, :, t, :] = y[ncc:]


def _s5(pbc, pbl, toe, wst, wout, dec, s5_d, glu_w, glu_b):
    bsz, n_ctx, wd = pbc.shape
    seq = pbl.shape[1]
    q = S5_CHUNK
    ng = toe.shape[1]
    ns = wst.shape[-1]
    n_chunks = (n_ctx + seq) // q
    uc = pbc.reshape(bsz, n_ctx // q, q, wd)
    ul = pbl.reshape(bsz, seq // q, q, wd)
    consts = (toe, wst, wout, dec, s5_d, glu_w, glu_b)
    oc, ol = pl.pallas_call(
        _s5_kernel,
        grid=(bsz,),
        in_specs=[_seq_spec(uc), _seq_spec(ul)] + [_full_spec(a) for a in consts],
        out_specs=[_seq_spec(uc), _seq_spec(ul)],
        out_shape=[jax.ShapeDtypeStruct(uc.shape, F32), jax.ShapeDtypeStruct(ul.shape, F32)],
        scratch_shapes=[pltpu.VMEM((2, 2, n_chunks, ng, ns), F32), pltpu.VMEM((2, n_chunks, ng, ns), F32)],
        compiler_params=pltpu.CompilerParams(dimension_semantics=("arbitrary",),
                                             vmem_limit_bytes=_vmem_limit(40 * 1024 * 1024)),
        name="s5_mixer",
    )(uc, ul, *consts)
    return oc.reshape(bsz, n_ctx, wd), ol.reshape(bsz, seq, wd)


def _raster_to_column(t, rows):
    b, s, d = t.shape
    return t.reshape(b, rows, GRID_W, d).transpose(0, 2, 1, 3).reshape(b, s, d)


def _column_to_raster(t, rows):
    b, s, d = t.shape
    return t.reshape(b, GRID_W, rows, d).transpose(0, 2, 1, 3).reshape(b, s, d)


def kernel(x, c, ctx, c_ctx, mod_w, mod_b, ffn_w_in, ffn_w_out, w_in, w_out, a_conv_w, a_conv_b, a_dt_bias, a_log,
           a_d, a_norm_w, s5_lam_re, s5_lam_im, s5_log_step, s5_b_re, s5_b_im, s5_c_re, s5_c_im, s5_d, s5_glu_w,
           s5_glu_b, hg_lb_logits, hg_norm_w, final_norm_w):
    bsz, seq, dm = x.shape
    n_ctx = ctx.shape[1]
    depth = mod_w.shape[0]
    rows = seq // GRID_W
    assert n_ctx % MIX_CHUNK == 0 and seq % MIX_CHUNK == 0 and seq % GRID_W == 0
    a_inner = A_HEADS * A_HEAD_DIM
    a_conv_dim = a_conv_w.shape[-1]
    c_width = hg_norm_w.shape[-1]

    pad = (-(bsz + 1)) % V7X_SUBLANES
    cvec = jnp.concatenate([c, c_ctx[None, :], jnp.zeros((pad, dm), F32)], axis=0)
    mod_all = _mod_vectors(cvec, mod_w, mod_b)

    p_lb = jax.nn.softmax(hg_lb_logits.astype(F32), axis=0)
    lower_bounds = jnp.cumsum(p_lb, axis=0) - p_lb[:1]
    fw = final_norm_w.reshape(1, dm).astype(F32)
    toe_all, wst_all, wout_all, dec_all = jax.vmap(_s5_tables)(s5_lam_re, s5_lam_im, s5_log_step, s5_b_re, s5_b_im,
                                                               s5_c_re, s5_c_im)

    h_lat, h_ctx = x, ctx
    col_order = False
    for l in range(depth):
        last = l == depth - 1
        m_lat = mod_all[l, :bsz].reshape(bsz, N_MOD, dm)
        m_ctx = mod_all[l, bsz].reshape(1, N_MOD, dm)
        wi0, wo0 = ffn_w_in[l, 0].astype(BF16), ffn_w_out[l, 0].astype(BF16)
        wi1, wo1 = ffn_w_in[l, 1].astype(BF16), ffn_w_out[l, 1].astype(BF16)

        want_col = l % 2 == 1
        if col_order and not want_col:
            h_lat = _column_to_raster(h_lat, rows)
            col_order = False
        h_lat = _ffn(h_lat, m_lat, wi0, wo0, fw, base=0, final=False, to_column=want_col and not col_order)
        col_order = want_col
        h_ctx = _ffn(h_ctx, m_ctx, wi0, wo0, fw, base=0, final=False)

        wl = w_in[l]
        o_dt = a_inner + a_conv_dim
        o_b = o_dt + 2 * A_HEADS
        w_cat = jnp.concatenate([wl[:, :o_b], jnp.zeros((dm, V7X_LANES - 2 * A_HEADS), F32), wl[:, o_b:]],
                                axis=1).astype(BF16)
        zl, xl, dtl, pbl, ql, fl, il, gl = _inproj(h_lat, m_lat, w_cat)
        zc, xc, dtc, pbc, qc, fc, ic, gc = _inproj(h_ctx, m_ctx, w_cat)

        lane_pad = jnp.zeros((V7X_LANES - 2 * A_HEADS,), F32)
        dtb = jnp.concatenate([a_dt_bias[l].reshape(-1), lane_pad]).reshape(1, V7X_LANES)
        aneg = jnp.concatenate([-jnp.exp(a_log[l].astype(F32)).reshape(-1), lane_pad]).reshape(1, V7X_LANES)
        dsk = jnp.repeat(a_d[l].astype(F32), A_HEAD_DIM).reshape(1, a_inner)
        ya_c, ya_l = _ssd(zc, zl, xc, xl, dtc, dtl, a_conv_w[l], a_conv_b[l].reshape(1, -1), dtb, aneg, dsk,
                          a_norm_w[l].reshape(1, -1))

        yb_c, yb_l = _s5(pbc, pbl, toe_all[l], wst_all[l], wout_all[l], dec_all[l], s5_d[l].reshape(1, -1),
                         s5_glu_w[l].astype(BF16), s5_glu_b[l].reshape(1, -1))

        yc_c, yc_l = _hgrn((qc, fc, ic, gc), (ql, fl, il, gl), lower_bounds[l].astype(F32),
                           hg_norm_w[l].reshape(1, c_width))

        w_o = w_out[l].astype(BF16)
        h_lat = _ffn(h_lat, m_lat, wi1, wo1, fw, base=6, final=last, to_raster=last and col_order,
                     mixed=(ya_l, yb_l, yc_l, w_o))
        if last:
            col_order = False
        else:
            h_ctx = _ffn(h_ctx, m_ctx, wi1, wo1, fw, base=6, final=False, mixed=(ya_c, yb_c, yc_c, w_o))
    return h_lat
```

```python
import functools
import math

import jax
import jax.numpy as jnp
from jax import lax
from jax.experimental import pallas as pl
from jax.experimental.pallas import tpu as pltpu

F32 = jnp.float32
BF16 = jnp.bfloat16
EPS = 1e-6
NEG_LOG2E = -1.4426950408889634

V7X_VMEM_BYTES = 64 * 1024 * 1024
V7X_LANES = 128
V7X_SUBLANES = 8
V7X_MXU_DIM = 256

GRID_W = 64
N_MOD = 9
A_HEADS = 8
A_HEAD_DIM = 64
A_GROUPS = 2
A_STATE = 64
A_CONV = 5
B_GROUP = 16
C_HEADS = 4
C_KEY = 64

TOKEN_TILES = (512, 384, 256)
MIX_CHUNK = 256
HG_CHUNK = 256
SCAN_UNROLL = 2
HG_SAMPLES_PER_STEP = 1
S5_CHUNK = 16
GRID_COLS_PER_TILE = 2 * V7X_SUBLANES


def _vmem_limit(estimate_bytes):
    return int(min(V7X_VMEM_BYTES - 6 * 1024 * 1024, max(estimate_bytes, 16 * 1024 * 1024)))


def _token_tile(n):
    for tm in TOKEN_TILES:
        if n % tm == 0:
            return tm
    raise ValueError(f"token count {n} has no supported tile")


def _silu(x):
    return x * jax.nn.sigmoid(x)


def _rms(x):
    return x * lax.rsqrt(jnp.mean(x * x, axis=-1, keepdims=True) + EPS)


def _split3(x):
    hi = x.astype(BF16)
    r = x - hi.astype(F32)
    mid = r.astype(BF16)
    lo = (r - mid.astype(F32)).astype(BF16)
    return hi, mid, lo


def _dot01_left(m01, x):
    hi, mid, lo = _split3(x)
    d = lambda a: jnp.dot(m01, a, preferred_element_type=F32)
    return (d(lo) + d(mid)) + d(hi)


def _dot01_right(x, m01):
    hi, mid, lo = _split3(x)
    d = lambda a: jnp.dot(a, m01, preferred_element_type=F32)
    return (d(lo) + d(mid)) + d(hi)


def _dot_nt(a, b):
    return lax.dot_general(a, b, (((1,), (1,)), ((), ())), preferred_element_type=F32)


def _mod_spec(mod):
    per_sample = mod.shape[0] > 1
    return pl.BlockSpec((1,) + mod.shape[1:], lambda b, t: (b if per_sample else 0, 0, 0))


def _mod_kernel(c_ref, w_ref, b_ref, o_ref):
    s = _silu(c_ref[...]).astype(BF16)
    o_ref[0] = jnp.dot(s, w_ref[0].astype(BF16), preferred_element_type=F32) + b_ref[0]


def _mod_vectors(cvec, mod_w, mod_b):
    depth, dm, nm = mod_w.shape
    rows = cvec.shape[0]
    tn = 1024
    return pl.pallas_call(
        _mod_kernel,
        grid=(depth, nm // tn),
        in_specs=[pl.BlockSpec((rows, dm), lambda l, j: (0, 0)),
                  pl.BlockSpec((1, dm, tn), lambda l, j: (l, 0, j)),
                  pl.BlockSpec((1, 1, tn), lambda l, j: (l, 0, j))],
        out_specs=pl.BlockSpec((1, rows, tn), lambda l, j: (l, 0, j)),
        out_shape=jax.ShapeDtypeStruct((depth, rows, nm), F32),
        compiler_params=pltpu.CompilerParams(dimension_semantics=("arbitrary", "arbitrary")),
        name="mod_vectors",
    )(cvec, mod_w, mod_b.reshape(depth, 1, nm))


def _ff_chunks(d_ff):
    if d_ff % V7X_MXU_DIM:
        return ((0, d_ff),)
    tiles = d_ff // V7X_MXU_DIM
    first = (tiles + 1) // 2 * V7X_MXU_DIM
    return ((0, first), (first, d_ff - first)) if d_ff > first else ((0, d_ff),)


def _ffn_kernel(h_ref, mod_ref, win_ref, wout_ref, fw_ref, *rest, base, d_ff, final, cols_in, cols_out, mixed):
    o_ref = rest[-1]
    if cols_in:
        x = jnp.concatenate([h_ref[0, :, wv, :] for wv in range(cols_in)], axis=0)
    else:
        x = h_ref[0]
    m = mod_ref[0]
    if mixed:
        ya_ref, yb_ref, yc_ref, wmix_ref = rest[:4]
        mix = jnp.concatenate([r[0].astype(BF16) for r in (ya_ref, yb_ref, yc_ref)], axis=1)
        x = x + m[base - 1:base] * jnp.dot(mix, wmix_ref[...], preferred_element_type=F32)
    u = (_rms(x) * (1.0 + m[base + 1:base + 2]) + m[base:base + 1]).astype(BF16)
    acc = jnp.zeros(x.shape, F32)
    for c0, cw in _ff_chunks(d_ff):
        g = jnp.dot(u, win_ref[:, c0:c0 + cw], preferred_element_type=F32)
        up = jnp.dot(u, win_ref[:, d_ff + c0:d_ff + c0 + cw], preferred_element_type=F32)
        a = (_silu(g) * up).astype(BF16)
        acc = acc + jnp.dot(a, wout_ref[c0:c0 + cw, :], preferred_element_type=F32)
    y = x + 0.5 * m[base + 2:base + 3] * acc
    if final:
        y = _rms(y) * fw_ref[...]
    if cols_out:
        nr = y.shape[0] // cols_out
        for wv in range(cols_out):
            o_ref[0, :, wv, :] = y[wv * nr:(wv + 1) * nr]
    else:
        o_ref[0] = y


def _ffn(h, mod, w_in, w_out, final_w, *, base, final, to_column=False, to_raster=False, mixed=None):
    bsz, lt, dm = h.shape
    d_ff = w_out.shape[0]
    grid_rows = lt // GRID_W
    cols = GRID_COLS_PER_TILE
    tm = grid_rows * cols if (to_column or to_raster) else _token_tile(lt)
    tok = lambda wd: pl.BlockSpec((1, tm, wd), lambda b, t: (b, t, 0))
    const = lambda a: pl.BlockSpec(a.shape, lambda b, t: (0, 0), pipeline_mode=pl.Buffered(1))
    grid_blk = pl.BlockSpec((1, grid_rows, cols, dm), lambda b, t: (b, 0, t, 0))
    kern = functools.partial(_ffn_kernel, base=base, d_ff=d_ff, final=final, mixed=mixed is not None,
                             cols_in=cols if to_column else 0, cols_out=cols if to_raster else 0)
    extra, extra_specs = (), []
    if mixed is not None:
        extra = tuple(mixed)
        extra_specs = [tok(a.shape[-1]) for a in mixed[:3]] + [const(mixed[3])]
    w_bytes = (w_in.size + w_out.size + (mixed[3].size if mixed is not None else 0)) * 2
    est = w_bytes + 8 * tm * dm * 4 + 3 * tm * d_ff * 4 + 6 * 1024 * 1024
    out = pl.pallas_call(
        kern,
        grid=(bsz, lt // tm),
        in_specs=[grid_blk if to_column else tok(dm), _mod_spec(mod), const(w_in), const(w_out),
                  pl.BlockSpec((1, dm), lambda b, t: (0, 0))] + extra_specs,
        out_specs=grid_blk if to_raster else tok(dm),
        out_shape=jax.ShapeDtypeStruct((bsz, grid_rows, GRID_W, dm) if to_raster else (bsz, lt, dm), F32),
        compiler_params=pltpu.CompilerParams(dimension_semantics=("arbitrary", "arbitrary"),
                                             vmem_limit_bytes=_vmem_limit(est)),
        name="ffn",
    )(h.reshape(bsz, grid_rows, GRID_W, dm) if to_column else h, mod, w_in, w_out, final_w, *extra)
    return out.reshape(bsz, lt, dm)


IN_SPLIT = ((512, BF16), (768, F32), (128, F32), (256, F32), (256, BF16), (512, F32), (256, BF16), (256, BF16))


def _inproj_kernel(h_ref, mod_ref, w_ref, *out_refs):
    m = mod_ref[0]
    u = (_rms(h_ref[0]) * (1.0 + m[4:5]) + m[3:4]).astype(BF16)
    p = jnp.dot(u, w_ref[...], preferred_element_type=F32)
    off = 0
    for ref, (wd, dt) in zip(out_refs, IN_SPLIT):
        ref[0] = p[:, off:off + wd].astype(dt)
        off += wd


def _inproj(h, mod, w):
    bsz, lt, dm = h.shape
    tm = _token_tile(lt)
    tok = lambda wd: pl.BlockSpec((1, tm, wd), lambda b, t: (b, t, 0))
    return pl.pallas_call(
        _inproj_kernel,
        grid=(bsz, lt // tm),
        in_specs=[tok(dm), _mod_spec(mod),
                  pl.BlockSpec(w.shape, lambda b, t: (0, 0), pipeline_mode=pl.Buffered(1))],
        out_specs=[tok(wd) for wd, _ in IN_SPLIT],
        out_shape=[jax.ShapeDtypeStruct((bsz, lt, wd), dt) for wd, dt in IN_SPLIT],
        compiler_params=pltpu.CompilerParams(dimension_semantics=("arbitrary", "arbitrary"),
                                             vmem_limit_bytes=_vmem_limit(40 * 1024 * 1024)),
        name="inproj",
    )(h, mod, w)


def _seq_spec(a):
    return pl.BlockSpec((1,) + a.shape[1:], lambda b: (b,) + (0,) * (a.ndim - 1))


def _full_spec(a):
    return pl.BlockSpec(a.shape, lambda b: (0,) * a.ndim)


def _ssd_kernel(zc_ref, zl_ref, xc_in_ref, xl_in_ref, dtc_ref, dtl_ref, cw_ref, cb_ref, dtb_ref, aneg_ref,
                dsk_ref, nw_ref, oc_ref, ol_ref, xconv_ref, yf_ref, yb_ref, st_ref, *, q):
    inner = A_HEADS * A_HEAD_DIM
    gw = A_STATE
    hpg = A_HEADS // A_GROUPS
    gl = hpg * A_HEAD_DIM
    n_ctx = oc_ref.shape[1]
    segs = ((zc_ref, xc_in_ref, dtc_ref, oc_ref, 0), (zl_ref, xl_in_ref, dtl_ref, ol_ref, n_ctx))

    cw = cw_ref[...]
    cb = cb_ref[...]
    for _, xin_ref, _, o_ref, base in segs:
        ls = o_ref.shape[1]
        n_chunks = ls // q

        def conv_body(i, carry, xin_ref=xin_ref, base=base, ls=ls, n_chunks=n_chunks):
            t0 = pl.multiple_of(i * q, q)
            cur = xin_ref[0, pl.ds(t0, q), :]
            p0 = pl.multiple_of(jnp.maximum(t0 - V7X_SUBLANES, 0), V7X_SUBLANES)
            n0 = pl.multiple_of(jnp.minimum(t0 + q, ls - V7X_SUBLANES), V7X_SUBLANES)
            prev = jnp.where(i == 0, 0.0, xin_ref[0, pl.ds(p0, V7X_SUBLANES), :])
            nxt = jnp.where(i == n_chunks - 1, 0.0, xin_ref[0, pl.ds(n0, V7X_SUBLANES), :])
            ext = jnp.concatenate([prev, cur, nxt], axis=0)
            acc = cb
            for k in range(A_CONV):
                s0 = V7X_SUBLANES - A_CONV // 2 + k
                acc = acc + cw[k:k + 1] * ext[s0:s0 + q]
            xconv_ref[pl.ds(pl.multiple_of(base + t0, q), q), :] = _silu(acc)
            return carry

        lax.fori_loop(0, n_chunks, conv_body, 0)

    st_ref[...] = jnp.zeros(st_ref.shape, F32)

    rowi = lax.broadcasted_iota(jnp.int32, (q, q), 0)
    coli = lax.broadcasted_iota(jnp.int32, (q, q), 1)
    er = lax.broadcasted_iota(jnp.int32, (V7X_LANES, inner), 0)
    ec = lax.broadcasted_iota(jnp.int32, (V7X_LANES, inner), 1)
    dtb = dtb_ref[...]
    aneg = aneg_ref[...]

    def load_chunk(seg, t0):
        dt_ref, base = seg[2], seg[4]
        xc = xconv_ref[pl.ds(pl.multiple_of(base + t0, q), q), :]
        dt = jax.nn.softplus(dt_ref[0, pl.ds(t0, q), :] + dtb)
        return xc[:, :inner], xc[:, inner:inner + A_GROUPS * gw], xc[:, inner + A_GROUPS * gw:], dt

    def scan_sum(d, d_a):
        tri = (rowi >= coli) if d == 0 else (rowi <= coli)
        return _dot01_left(jnp.where(tri, 1.0, 0.0).astype(BF16), d_a)

    def carried(d, xs, bm, cm, dt, cum):
        last = q - 1 if d == 0 else 0
        expand = jnp.where(er == d * A_HEADS + (ec >> int(math.log2(A_HEAD_DIM))), 1.0, 0.0).astype(BF16)
        e_hi, e_mid, _ = _split3(jnp.exp(cum))
        ecum = (jnp.dot(e_mid, expand, preferred_element_type=F32) + jnp.dot(e_hi, expand, preferred_element_type=F32))
        wcol = (jnp.exp(cum[last:last + 1, :] - cum) * dt).astype(BF16)
        xw = (xs * jnp.dot(wcol, expand, preferred_element_type=F32)).astype(BF16)
        e_last = ecum[last:last + 1, :]
        bm_t = bm.T
        ys = []
        for g in range(A_GROUPS):
            cg = cm[:, g * gw:(g + 1) * gw].astype(BF16)
            st = st_ref[d, g]
            ys.append(jnp.dot(cg, st.astype(BF16), preferred_element_type=F32) * ecum[:, g * gl:(g + 1) * gl])
            upd = jnp.dot(bm_t[g * gw:(g + 1) * gw].astype(BF16), xw[:, g * gl:(g + 1) * gl],
                          preferred_element_type=F32)
            st_ref[d, g] = st * e_last[:, g * gl:(g + 1) * gl] + upd
        return jnp.concatenate(ys, axis=1)

    lower = rowi > coli
    diag = rowi == coli

    def step(seg, i, n_chunks):
        t0 = pl.multiple_of(i * q, q)
        xs, bm, cm, dt = load_chunk(seg, t0)
        d_a = dt * aneg
        cum_f = scan_sum(0, d_a)
        cum_b = scan_sum(1, d_a)
        log_dt = jnp.log(dt)
        col_f, col_b = cum_f, cum_b
        row_f = (cum_f - log_dt).T
        row_b = (cum_b - log_dt).T
        dt_t = dt.T
        xs_bf = xs.astype(BF16)
        ys = []
        for g in range(A_GROUPS):
            cg = cm[:, g * gw:(g + 1) * gw].astype(BF16)
            bg = bm[:, g * gw:(g + 1) * gw].astype(BF16)
            gmat = _dot_nt(cg, bg)
            for hh in range(hpg):
                h = g * hpg + hh
                hb = A_HEADS + h
                arg = jnp.where(lower, col_f[:, h:h + 1] - row_f[h:h + 1, :], col_b[:, hb:hb + 1] - row_b[hb:hb + 1, :])
                wgt = jnp.where(diag, dt_t[h:h + 1, :] + dt_t[hb:hb + 1, :], jnp.exp(arg))
                sc = (gmat * wgt).astype(BF16)
                ys.append(jnp.dot(sc, xs_bf[:, h * A_HEAD_DIM:(h + 1) * A_HEAD_DIM], preferred_element_type=F32))
        yf_ref[pl.ds(pl.multiple_of(seg[4] + t0, q), q), :] = (jnp.concatenate(ys, axis=1)
                                                              + carried(0, xs, bm, cm, dt, cum_f))
        tb = pl.multiple_of((n_chunks - 1 - i) * q, q)
        xsb, bmb, cmb, dtb_ = load_chunk(seg, tb)
        yb_ref[pl.ds(pl.multiple_of(seg[4] + tb, q), q), :] = carried(1, xsb, bmb, cmb, dtb_, scan_sum(1, dtb_ * aneg))

    dsk = dsk_ref[...]
    nw = nw_ref[...]
    for seg in segs:
        n_chunks = seg[3].shape[1] // q
        lax.fori_loop(0, n_chunks, lambda i, carry, seg=seg, n_chunks=n_chunks: (step(seg, i, n_chunks), carry)[1], 0,
                      unroll=math.gcd(n_chunks, SCAN_UNROLL))

    for z_ref, _, _, o_ref, base in segs:
        def fin_body(i, carry, z_ref=z_ref, o_ref=o_ref, base=base):
            t0 = pl.multiple_of(i * q, q)
            xs = xconv_ref[pl.ds(pl.multiple_of(base + t0, q), q), :inner]
            rows = pl.ds(pl.multiple_of(base + t0, q), q)
            y = yf_ref[rows, :] + yb_ref[rows, :] + dsk * xs
            y = y * _silu(z_ref[0, pl.ds(t0, q), :].astype(F32))
            o_ref[0, pl.ds(t0, q), :] = (_rms(y) * nw).astype(o_ref.dtype)
            return carry

        lax.fori_loop(0, o_ref.shape[1] // q, fin_body, 0)


def _ssd(zc, zl, xc, xl, dtc, dtl, cw, cb, dtb, aneg, dsk, nw):
    bsz, n_ctx, inner = zc.shape
    seq = zl.shape[1]
    lt = n_ctx + seq
    cd = xc.shape[-1]
    ins = (zc, zl, xc, xl, dtc, dtl)
    consts = (cw, cb, dtb, aneg, dsk, nw)
    est = 2 * lt * (2 * inner + 4 * cd + 4 * V7X_LANES + 4 * inner) + 4 * lt * (cd + inner) + 12 * 1024 * 1024
    return pl.pallas_call(
        functools.partial(_ssd_kernel, q=MIX_CHUNK),
        grid=(bsz,),
        in_specs=[_seq_spec(a) for a in ins] + [_full_spec(a) for a in consts],
        out_specs=[pl.BlockSpec((1, n_ctx, inner), lambda b: (b, 0, 0)),
                   pl.BlockSpec((1, seq, inner), lambda b: (b, 0, 0))],
        out_shape=[jax.ShapeDtypeStruct((bsz, n_ctx, inner), BF16), jax.ShapeDtypeStruct((bsz, seq, inner), BF16)],
        scratch_shapes=[pltpu.VMEM((lt, cd), F32),
                        pltpu.VMEM((lt, inner), F32),
                        pltpu.VMEM((lt, inner), F32),
                        pltpu.VMEM((2, A_GROUPS, A_STATE, inner // A_GROUPS), F32)],
        compiler_params=pltpu.CompilerParams(dimension_semantics=("arbitrary",),
                                             vmem_limit_bytes=_vmem_limit(est)),
        name="ssd_mixer",
    )(*ins, *consts)


def _hgrn_kernel(qc_ref, ql_ref, ffc_ref, ffl_ref, fbc_ref, fbl_ref, vc_ref, vl_ref, gc_ref, gl_ref,
                 lb_ref, nw_ref, oc_ref, ol_ref, of_ref, ob_ref, st_ref, *, c):
    w = C_HEADS * C_KEY
    nv = c // V7X_SUBLANES
    segs = ((qc_ref, (ffc_ref, fbc_ref), vc_ref, gc_ref, oc_ref, 0),
            (ql_ref, (ffl_ref, fbl_ref), vl_ref, gl_ref, ol_ref, oc_ref.shape[1]))
    st_ref[...] = jnp.zeros(st_ref.shape, F32)

    rowi = lax.broadcasted_iota(jnp.int32, (c, c), 0)
    coli = lax.broadcasted_iota(jnp.int32, (c, c), 1)
    pair_bits = rowi ^ coli
    hr = lax.broadcasted_iota(jnp.int32, (w, w), 0)
    hc = lax.broadcasted_iota(jnp.int32, (w, w), 1)
    same_head = (hr >> int(math.log2(C_KEY))) == (hc >> int(math.log2(C_KEY)))
    bones = jnp.where(same_head, 1.0, 0.0).astype(BF16)
    trow = lax.broadcasted_iota(jnp.int32, (c, w), 0)
    sub3 = lax.broadcasted_iota(jnp.int32, (nv, V7X_SUBLANES, w), 1)

    def boundary_small(x3, p0, s):
        b = 1
        while b <= s:
            if p0 & b:
                x3 = jnp.where((sub3 & b) != 0, x3, pltpu.roll(x3, V7X_SUBLANES - b, axis=1))
            else:
                x3 = jnp.where((sub3 & b) != 0, pltpu.roll(x3, b, axis=1), x3)
            b *= 2
        return x3

    def gates(d, seg, smp, t0):
        zf = seg[1][d][smp, pl.ds(t0, c), :]
        lb = lb_ref[d:d + 1, :]
        f = lb + (1.0 - lb) * jax.nn.sigmoid(zf)
        causal = (rowi >= coli) if d == 0 else (rowi <= coli)
        cum = _dot01_left(jnp.where(causal, 1.0, 0.0).astype(BF16), jnp.log(f))
        return 1.0 - f, cum

    def boundary(cum, d, s):
        p0 = (s - 1) if d == 0 else s
        return boundary_small(cum.reshape(nv, V7X_SUBLANES, w), p0, s).reshape(c, w)

    def carried(d, smp, qs, kk, cum, v32):
        last = c - 1 if d == 0 else 0
        cum_last = cum[last:last + 1, :]
        st = st_ref[smp, d]
        qe = (qs * jnp.exp(cum)).astype(BF16)
        kw = (kk * jnp.exp(cum_last - cum)).astype(BF16)
        upd = jnp.dot(v32.T.astype(BF16), kw, preferred_element_type=F32)
        st_ref[smp, d] = jnp.where(same_head, st * jnp.exp(cum_last) + upd, 0.0)
        return _dot_nt(qe, st.astype(BF16))

    def step(seg, smp, i, n_chunks):
        q_ref, _, v_ref, _, o_ref, base = seg
        t0 = pl.multiple_of(i * c, c)
        qs = _silu(q_ref[smp, pl.ds(t0, c), :].astype(F32))
        v = v_ref[smp, pl.ds(t0, c), :]
        v32 = v.astype(F32)
        kk0, cum0 = gates(0, seg, smp, t0)
        kk1, cum1 = gates(1, seg, smp, t0)

        out = jnp.dot((qs * (kk0 + kk1)).astype(BF16), bones, preferred_element_type=F32) * v32
        scores = [jnp.zeros((c, c), F32) for _ in range(C_HEADS)]
        s = 1
        while s < c:
            if s < V7X_SUBLANES:
                e0 = jnp.exp2(jnp.abs(cum0 - boundary(cum0, 0, s)) * NEG_LOG2E)
                e1 = jnp.exp2(jnp.abs(cum1 - boundary(cum1, 1, s)) * NEG_LOG2E)
                has_bit = (trow & s) != 0
                qm = (qs * jnp.where(has_bit, e0, e1)).astype(BF16)
                km = jnp.where(has_bit, kk1 * e1, kk0 * e0).astype(BF16)
            else:
                halves = lambda a: a.reshape(c // (2 * s), 2, s, w)
                c0, c1, q2 = halves(cum0), halves(cum1), halves(qs)
                x0 = c0 - c0[:, 0:1, s - 1:s, :]
                x1 = c1 - c1[:, 1:2, 0:1, :]
                qarg = jnp.concatenate([x1[:, 0:1], x0[:, 1:2]], axis=1)
                karg = jnp.concatenate([x0[:, 0:1], x1[:, 1:2]], axis=1)
                ksel = jnp.concatenate([halves(kk0)[:, 0:1], halves(kk1)[:, 1:2]], axis=1)
                qm = (q2 * jnp.exp2(qarg * (-NEG_LOG2E))).reshape(c, w).astype(BF16)
                km = (ksel * jnp.exp2(karg * NEG_LOG2E)).reshape(c, w).astype(BF16)
            take = pair_bits >= s
            for h in range(C_HEADS):
                sl = slice(h * C_KEY, (h + 1) * C_KEY)
                scores[h] = jnp.where(take, _dot_nt(qm[:, sl], km[:, sl]), scores[h])
            s *= 2
        out = out + jnp.concatenate(
            [jnp.dot(scores[h].astype(BF16), v[:, h * C_KEY:(h + 1) * C_KEY], preferred_element_type=F32)
             for h in range(C_HEADS)], axis=1)
        of_ref[smp, pl.ds(pl.multiple_of(base + t0, c), c), :] = out + carried(0, smp, qs, kk0, cum0, v32)
        tb = pl.multiple_of((n_chunks - 1 - i) * c, c)
        qsb = _silu(q_ref[smp, pl.ds(tb, c), :].astype(F32))
        kkb, cumb = gates(1, seg, smp, tb)
        ob_ref[smp, pl.ds(pl.multiple_of(base + tb, c), c), :] = carried(
            1, smp, qsb, kkb, cumb, v_ref[smp, pl.ds(tb, c), :].astype(F32))

    n_smp = oc_ref.shape[0]
    for seg in segs:
        n_chunks = seg[4].shape[1] // c

        def scan_body(i, carry, seg=seg, n_chunks=n_chunks):
            for smp in range(n_smp):
                step(seg, smp, i, n_chunks)
            return carry

        lax.fori_loop(0, n_chunks, scan_body, 0, unroll=math.gcd(n_chunks, SCAN_UNROLL))

    nw = nw_ref[...]
    for _, _, _, g_ref, o_ref, base in segs:
        def fin_body(i, carry, g_ref=g_ref, o_ref=o_ref, base=base):
            t0 = pl.multiple_of(i * c, c)
            for smp in range(n_smp):
                rows = pl.ds(pl.multiple_of(base + t0, c), c)
                o = of_ref[smp, rows, :] + ob_ref[smp, rows, :]
                ms = _dot01_right(o * o, bones) * (1.0 / C_KEY)
                y = o * lax.rsqrt(ms + EPS) * nw
                o_ref[smp, pl.ds(t0, c), :] = (y * _silu(g_ref[smp, pl.ds(t0, c), :].astype(F32))).astype(o_ref.dtype)
            return carry

        lax.fori_loop(0, o_ref.shape[1] // c, fin_body, 0)


def _hgrn(ctx_parts, lat_parts, lower, nw):
    qc, fc, ic, gc = ctx_parts
    ql, fl, il, gl = lat_parts
    bsz, n_ctx, w = qc.shape
    seq = ql.shape[1]
    ns = math.gcd(bsz, HG_SAMPLES_PER_STEP)
    blk = lambda a, j: pl.BlockSpec((ns, a.shape[1], w), lambda b: (b, 0, j))
    ins = (qc, ql, fc, fl, fc, fl, ic, il, gc, gl)
    specs = [blk(qc, 0), blk(ql, 0), blk(fc, 0), blk(fl, 0), blk(fc, 1), blk(fl, 1),
             blk(ic, 0), blk(il, 0), blk(gc, 0), blk(gl, 0)]
    est = ns * 2 * (n_ctx + seq) * w * (2 + 8 + 2 + 2 + 4) + 20 * 1024 * 1024
    return pl.pallas_call(
        functools.partial(_hgrn_kernel, c=HG_CHUNK),
        grid=(bsz // ns,),
        in_specs=specs + [_full_spec(lower), _full_spec(nw)],
        out_specs=[blk(qc, 0), blk(ql, 0)],
        out_shape=[jax.ShapeDtypeStruct((bsz, n_ctx, w), BF16), jax.ShapeDtypeStruct((bsz, seq, w), BF16)],
        scratch_shapes=[pltpu.VMEM((ns, n_ctx + seq, w), F32), pltpu.VMEM((ns, n_ctx + seq, w), F32),
                        pltpu.VMEM((ns, 2, w, w), F32)],
        compiler_params=pltpu.CompilerParams(dimension_semantics=("arbitrary",),
                                             vmem_limit_bytes=_vmem_limit(est)),
        name="hgrn_mixer",
    )(*ins, lower, nw)


def _s5_tables(lam_re, lam_im, log_step, b_re, b_im, c_re, c_im):
    q = S5_CHUNK
    hp = lax.Precision.HIGHEST
    lam_re = lam_re.astype(F32)
    lam_im = lam_im.astype(F32)
    step = jnp.exp(log_step.astype(F32))[..., None]
    tau = jnp.arange(q + 1, dtype=F32)[:, None, None, None]
    mag = jnp.exp(lam_re * step * tau)
    pr = mag * jnp.cos(lam_im * step * tau)
    pi = mag * jnp.sin(lam_im * step * tau)
    ar, ai = pr[1], pi[1]
    den = lam_re * lam_re + lam_im * lam_im
    nr = ar - 1.0
    kr = (nr * lam_re + ai * lam_im) / den
    ki = (ai * lam_re - nr * lam_im) / den
    b_re = b_re.astype(F32)
    b_im = b_im.astype(F32)
    br = kr[..., None] * b_re - ki[..., None] * b_im
    bi = kr[..., None] * b_im + ki[..., None] * b_re
    c_re = c_re.astype(F32)
    c_im = c_im.astype(F32)
    qr = pr[..., None] * br - pi[..., None] * bi
    qi = pr[..., None] * bi + pi[..., None] * br
    hk = jnp.einsum('dgon,tdgni->tdgoi', c_re, qr, precision=hp) - jnp.einsum('dgon,tdgni->tdgoi', c_im, qi, precision=hp)
    g, cg = b_re.shape[1], b_re.shape[3]
    width = q * cg
    strip = lambda h: h.transpose(1, 3, 0, 2).reshape(g, cg, width)
    zeros = jnp.zeros((g, cg, width - cg), F32)
    pad0 = jnp.concatenate([zeros, strip(hk[:q, 0])], axis=-1)
    pad1 = jnp.concatenate([strip(hk[:q, 1][::-1]), zeros], axis=-1)
    rows = lambda pad: jnp.stack([pad[..., (q - 1 - s) * cg:(q - 1 - s) * cg + width] for s in range(q)], axis=1)
    toe = jnp.stack([rows(pad0), rows(pad1)]).reshape(2, g, width, width)
    by_dir = lambda a: jnp.stack([a[:q, 0][::-1], a[:q, 1]])
    wst = jnp.concatenate([by_dir(qr), by_dir(qi)], axis=3)
    wst = wst.transpose(0, 2, 1, 4, 3).reshape(2, g, width, -1)
    out_pow = lambda a: jnp.stack([a[1:, 0], a[1:, 1][::-1]])
    po_r, po_i = out_pow(pr), out_pow(pi)
    w_xr = c_re[:, None] * po_r[:, :, :, None, :] - c_im[:, None] * po_i[:, :, :, None, :]
    w_xi = -c_re[:, None] * po_i[:, :, :, None, :] - c_im[:, None] * po_r[:, :, :, None, :]
    wout = jnp.concatenate([w_xr, w_xi], axis=-1)
    wout = wout.transpose(0, 2, 4, 1, 3).reshape(2, g, -1, width)
    dec = jnp.stack([jnp.concatenate([pr[q], pr[q]], axis=-1),
                     jnp.concatenate([-pi[q], pi[q]], axis=-1),
                     jnp.concatenate([pi[q], -pi[q]], axis=-1)], axis=1)
    return toe.astype(BF16), wst.astype(BF16), wout.astype(BF16), dec


def _lane_block_transpose(arrs, blk):
    n = len(arrs)
    rows, width = arrs[0].shape
    lane_blk = lax.broadcasted_iota(jnp.int32, (rows, width), 1) >> int(math.log2(blk))
    k = 1
    while k < n:
        hi_half = (lane_blk & k) != 0
        new = list(arrs)
        for a in range(n):
            if a & k == 0:
                lo_arr, hi_arr = arrs[a], arrs[a | k]
                new[a] = jnp.where(hi_half, pltpu.roll(hi_arr, blk * k, axis=1), lo_arr)
                new[a | k] = jnp.where(hi_half, hi_arr, pltpu.roll(lo_arr, width - blk * k, axis=1))
        arrs = new
        k *= 2
    return arrs


def _gelu_tanh(x):
    return 0.5 * x * (1.0 + jnp.tanh(math.sqrt(2.0 / math.pi) * (x + 0.044715 * (x * x * x))))


def _s5_kernel(uc_ref, ul_ref, toe_ref, wst_ref, wout_ref, dec_ref, sd_ref, gw_ref, gb_ref,
               oc_ref, ol_ref, z_ref, xin_ref):
    q = uc_ref.shape[2]
    ncc = uc_ref.shape[1]
    n_chunks = ncc + ul_ref.shape[1]
    ng = toe_ref.shape[1]
    half = dec_ref.shape[-1] // 2
    rows = [jnp.concatenate([uc_ref[0, :, s, :], ul_ref[0, :, s, :]], axis=0) for s in range(q)]
    ug = [a.astype(BF16) for a in _lane_block_transpose(rows, B_GROUP)]
    ys = [None] * ng
    for d in range(2):
        zs = []
        for g in range(ng):
            ys[g] = jnp.dot(ug[g], toe_ref[d, g], preferred_element_type=F32) + (0.0 if ys[g] is None else ys[g])
            zs.append(jnp.dot(ug[g], wst_ref[d, g], preferred_element_type=F32))
        z = pltpu.einshape("gcn->cgn", jnp.stack(zs))
        z_ref[0, d] = z
        z_ref[1, d] = pltpu.roll(z, half, axis=2)

    coef = [[dec_ref[d, j] for j in range(3)] for d in range(2)]

    def body(i, carry):
        nxt = []
        for d in range(2):
            if d == 0:
                c_idx = i
            else:
                c_idx = jnp.where(i < ncc, ncc - 1 - i, n_chunks - 1 - (i - ncc))
            x, xs = carry[2 * d], carry[2 * d + 1]
            a1, a2, a2s = coef[d]
            xin_ref[d, c_idx] = x
            nxt.append(a1 * x + a2 * xs + z_ref[0, d, c_idx])
            nxt.append(a1 * xs + a2s * x + z_ref[1, d, c_idx])
        return tuple(nxt)

    zero = jnp.zeros(coef[0][0].shape, F32)
    lax.fori_loop(0, n_chunks, body, (zero, zero, zero, zero), unroll=2)
    for d in range(2):
        xg = pltpu.einshape("cgn->gcn", xin_ref[d]).astype(BF16)
        for g in range(ng):
            ys[g] = ys[g] + jnp.dot(xg[g], wout_ref[d, g], preferred_element_type=F32)
    yt = _lane_block_transpose(ys, B_GROUP)
    sd = sd_ref[...]
    gw = gw_ref[...]
    gb = gb_ref[...]
    for t in range(q):
        y = _gelu_tanh(yt[t] + sd * rows[t])
        gate = jnp.dot(y.astype(BF16), gw, preferred_element_type=F32) + gb
        y = y * jax.nn.sigmoid(gate)
        oc_ref[0, :, t, :] = y[:ncc]
        ol_ref[0, :, t, :] = y[ncc:]


def _s5(pbc, pbl, toe, wst, wout, dec, s5_d, glu_w, glu_b):
    bsz, n_ctx, wd = pbc.shape
    seq = pbl.shape[1]
    q = S5_CHUNK
    ng = toe.shape[1]
    ns = wst.shape[-1]
    n_chunks = (n_ctx + seq) // q
    uc = pbc.reshape(bsz, n_ctx // q, q, wd)
    ul = pbl.reshape(bsz, seq // q, q, wd)
    consts = (toe, wst, wout, dec, s5_d, glu_w, glu_b)
    oc, ol = pl.pallas_call(
        _s5_kernel,
        grid=(bsz,),
        in_specs=[_seq_spec(uc), _seq_spec(ul)] + [_full_spec(a) for a in consts],
        out_specs=[_seq_spec(uc), _seq_spec(ul)],
        out_shape=[jax.ShapeDtypeStruct(uc.shape, F32), jax.ShapeDtypeStruct(ul.shape, F32)],
        scratch_shapes=[pltpu.VMEM((2, 2, n_chunks, ng, ns), F32), pltpu.VMEM((2, n_chunks, ng, ns), F32)],
        compiler_params=pltpu.CompilerParams(dimension_semantics=("arbitrary",),
                                             vmem_limit_bytes=_vmem_limit(40 * 1024 * 1024)),
        name="s5_mixer",
    )(uc, ul, *consts)
    return oc.reshape(bsz, n_ctx, wd), ol.reshape(bsz, seq, wd)


def _column_to_raster(t, rows):
    b, s, d = t.shape
    return t.reshape(b, GRID_W, rows, d).transpose(0, 2, 1, 3).reshape(b, s, d)


def kernel(x, c, ctx, c_ctx, mod_w, mod_b, ffn_w_in, ffn_w_out, w_in, w_out, a_conv_w, a_conv_b, a_dt_bias, a_log,
           a_d, a_norm_w, s5_lam_re, s5_lam_im, s5_log_step, s5_b_re, s5_b_im, s5_c_re, s5_c_im, s5_d, s5_glu_w,
           s5_glu_b, hg_lb_logits, hg_norm_w, final_norm_w):
    bsz, seq, dm = x.shape
    n_ctx = ctx.shape[1]
    depth = mod_w.shape[0]
    rows = seq // GRID_W
    assert n_ctx % MIX_CHUNK == 0 and seq % MIX_CHUNK == 0 and seq % GRID_W == 0
    a_inner = A_HEADS * A_HEAD_DIM
    a_conv_dim = a_conv_w.shape[-1]
    c_width = hg_norm_w.shape[-1]

    pad = (-(bsz + 1)) % V7X_SUBLANES
    cvec = jnp.concatenate([c, c_ctx[None, :], jnp.zeros((pad, dm), F32)], axis=0)
    mod_all = _mod_vectors(cvec, mod_w, mod_b)

    p_lb = jax.nn.softmax(hg_lb_logits.astype(F32), axis=0)
    lower_bounds = jnp.cumsum(p_lb, axis=0) - p_lb[:1]
    fw = final_norm_w.reshape(1, dm).astype(F32)
    toe_all, wst_all, wout_all, dec_all = jax.vmap(_s5_tables)(s5_lam_re, s5_lam_im, s5_log_step, s5_b_re, s5_b_im,
                                                               s5_c_re, s5_c_im)

    h_lat, h_ctx = x, ctx
    col_order = False
    for l in range(depth):
        last = l == depth - 1
        m_lat = mod_all[l, :bsz].reshape(bsz, N_MOD, dm)
        m_ctx = mod_all[l, bsz].reshape(1, N_MOD, dm)
        wi0, wo0 = ffn_w_in[l, 0].astype(BF16), ffn_w_out[l, 0].astype(BF16)
        wi1, wo1 = ffn_w_in[l, 1].astype(BF16), ffn_w_out[l, 1].astype(BF16)

        want_col = l % 2 == 1
        if col_order and not want_col:
            h_lat = _column_to_raster(h_lat, rows)
            col_order = False
        h_lat = _ffn(h_lat, m_lat, wi0, wo0, fw, base=0, final=False, to_column=want_col and not col_order)
        col_order = want_col
        h_ctx = _ffn(h_ctx, m_ctx, wi0, wo0, fw, base=0, final=False)

        wl = w_in[l]
        o_dt = a_inner + a_conv_dim
        o_b = o_dt + 2 * A_HEADS
        w_cat = jnp.concatenate([wl[:, :o_b], jnp.zeros((dm, V7X_LANES - 2 * A_HEADS), F32), wl[:, o_b:]],
                                axis=1).astype(BF16)
        zl, xl, dtl, pbl, ql, fl, il, gl = _inproj(h_lat, m_lat, w_cat)
        zc, xc, dtc, pbc, qc, fc, ic, gc = _inproj(h_ctx, m_ctx, w_cat)

        lane_pad = jnp.zeros((V7X_LANES - 2 * A_HEADS,), F32)
        dtb = jnp.concatenate([a_dt_bias[l].reshape(-1), lane_pad]).reshape(1, V7X_LANES)
        aneg = jnp.concatenate([-jnp.exp(a_log[l].astype(F32)).reshape(-1), lane_pad]).reshape(1, V7X_LANES)
        dsk = jnp.repeat(a_d[l].astype(F32), A_HEAD_DIM).reshape(1, a_inner)
        ya_c, ya_l = _ssd(zc, zl, xc, xl, dtc, dtl, a_conv_w[l], a_conv_b[l].reshape(1, -1), dtb, aneg, dsk,
                          a_norm_w[l].reshape(1, -1))

        yb_c, yb_l = _s5(pbc, pbl, toe_all[l], wst_all[l], wout_all[l], dec_all[l], s5_d[l].reshape(1, -1),
                         s5_glu_w[l].astype(BF16), s5_glu_b[l].reshape(1, -1))

        yc_c, yc_l = _hgrn((qc, fc, ic, gc), (ql, fl, il, gl), lower_bounds[l].astype(F32),
                           hg_norm_w[l].reshape(1, c_width))

        w_o = w_out[l].astype(BF16)
        h_lat = _ffn(h_lat, m_lat, wi1, wo1, fw, base=6, final=last, to_raster=last and col_order,
                     mixed=(ya_l, yb_l, yc_l, w_o))
        if last:
            col_order = False
        else:
            h_ctx = _ffn(h_ctx, m_ctx, wi1, wo1, fw, base=6, final=False, mixed=(ya_c, yb_c, yc_c, w_o))
    return h_lat
```

```python
import functools
import math

import jax
import jax.numpy as jnp
from jax import lax
from jax.experimental import pallas as pl
from jax.experimental.pallas import tpu as pltpu

F32 = jnp.float32
BF16 = jnp.bfloat16
EPS = 1e-6
NEG_LOG2E = -1.4426950408889634

V7X_VMEM_BYTES = 64 * 1024 * 1024
V7X_LANES = 128
V7X_SUBLANES = 8
V7X_MXU_DIM = 256

GRID_W = 64
N_MOD = 9
A_HEADS = 8
A_HEAD_DIM = 64
A_GROUPS = 2
A_STATE = 64
A_CONV = 5
B_GROUP = 16
C_HEADS = 4
C_KEY = 64

TOKEN_TILES = (512, 384, 256)
MIX_CHUNK = 256
HG_CHUNK = 256
SCAN_UNROLL = 4
HG_SAMPLES_PER_STEP = 1
S5_CHUNK = 16
GRID_COLS_PER_TILE = 2 * V7X_SUBLANES


def _vmem_limit(estimate_bytes):
    return int(min(V7X_VMEM_BYTES - 6 * 1024 * 1024, max(estimate_bytes, 16 * 1024 * 1024)))


def _token_tile(n):
    for tm in TOKEN_TILES:
        if n % tm == 0:
            return tm
    raise ValueError(f"token count {n} has no supported tile")


def _silu(x):
    return x * jax.nn.sigmoid(x)


def _rms(x):
    return x * lax.rsqrt(jnp.mean(x * x, axis=-1, keepdims=True) + EPS)


def _split3(x):
    hi = x.astype(BF16)
    r = x - hi.astype(F32)
    mid = r.astype(BF16)
    lo = (r - mid.astype(F32)).astype(BF16)
    return hi, mid, lo


def _dot01_left(m01, x):
    hi, mid, lo = _split3(x)
    d = lambda a: jnp.dot(m01, a, preferred_element_type=F32)
    return (d(lo) + d(mid)) + d(hi)


def _dot01_right(x, m01):
    hi, mid, lo = _split3(x)
    d = lambda a: jnp.dot(a, m01, preferred_element_type=F32)
    return (d(lo) + d(mid)) + d(hi)


def _dot_nt(a, b):
    return lax.dot_general(a, b, (((1,), (1,)), ((), ())), preferred_element_type=F32)


def _mod_spec(mod):
    per_sample = mod.shape[0] > 1
    return pl.BlockSpec((1,) + mod.shape[1:], lambda b, t: (b if per_sample else 0, 0, 0))


def _mod_kernel(c_ref, w_ref, b_ref, o_ref):
    s = _silu(c_ref[...]).astype(BF16)
    o_ref[0] = jnp.dot(s, w_ref[0].astype(BF16), preferred_element_type=F32) + b_ref[0]


def _mod_vectors(cvec, mod_w, mod_b):
    depth, dm, nm = mod_w.shape
    rows = cvec.shape[0]
    tn = 1024
    return pl.pallas_call(
        _mod_kernel,
        grid=(depth, nm // tn),
        in_specs=[pl.BlockSpec((rows, dm), lambda l, j: (0, 0)),
                  pl.BlockSpec((1, dm, tn), lambda l, j: (l, 0, j)),
                  pl.BlockSpec((1, 1, tn), lambda l, j: (l, 0, j))],
        out_specs=pl.BlockSpec((1, rows, tn), lambda l, j: (l, 0, j)),
        out_shape=jax.ShapeDtypeStruct((depth, rows, nm), F32),
        compiler_params=pltpu.CompilerParams(dimension_semantics=("arbitrary", "arbitrary")),
        name="mod_vectors",
    )(cvec, mod_w, mod_b.reshape(depth, 1, nm))


def _ff_chunks(d_ff):
    if d_ff % V7X_MXU_DIM:
        return ((0, d_ff),)
    tiles = d_ff // V7X_MXU_DIM
    first = (tiles + 1) // 2 * V7X_MXU_DIM
    return ((0, first), (first, d_ff - first)) if d_ff > first else ((0, d_ff),)


def _ffn_kernel(h_ref, mod_ref, win_ref, wout_ref, fw_ref, *rest, base, d_ff, final, cols_in, cols_out, mixed):
    o_ref = rest[-1]
    if cols_in:
        x = jnp.concatenate([h_ref[0, :, wv, :] for wv in range(cols_in)], axis=0)
    else:
        x = h_ref[0]
    m = mod_ref[0]
    if mixed:
        ya_ref, yb_ref, yc_ref, wmix_ref = rest[:4]
        mix = jnp.concatenate([r[0].astype(BF16) for r in (ya_ref, yb_ref, yc_ref)], axis=1)
        x = x + m[base - 1:base] * jnp.dot(mix, wmix_ref[...], preferred_element_type=F32)
    u = (_rms(x) * (1.0 + m[base + 1:base + 2]) + m[base:base + 1]).astype(BF16)
    acc = jnp.zeros(x.shape, F32)
    for c0, cw in _ff_chunks(d_ff):
        g = jnp.dot(u, win_ref[:, c0:c0 + cw], preferred_element_type=F32)
        up = jnp.dot(u, win_ref[:, d_ff + c0:d_ff + c0 + cw], preferred_element_type=F32)
        a = (_silu(g) * up).astype(BF16)
        acc = acc + jnp.dot(a, wout_ref[c0:c0 + cw, :], preferred_element_type=F32)
    y = x + 0.5 * m[base + 2:base + 3] * acc
    if final:
        y = _rms(y) * fw_ref[...]
    if cols_out:
        nr = y.shape[0] // cols_out
        for wv in range(cols_out):
            o_ref[0, :, wv, :] = y[wv * nr:(wv + 1) * nr]
    else:
        o_ref[0] = y


def _ffn(h, mod, w_in, w_out, final_w, *, base, final, to_column=False, to_raster=False, mixed=None):
    bsz, lt, dm = h.shape
    d_ff = w_out.shape[0]
    grid_rows = lt // GRID_W
    cols = GRID_COLS_PER_TILE
    tm = grid_rows * cols if (to_column or to_raster) else _token_tile(lt)
    tok = lambda wd: pl.BlockSpec((1, tm, wd), lambda b, t: (b, t, 0))
    const = lambda a: pl.BlockSpec(a.shape, lambda b, t: (0, 0), pipeline_mode=pl.Buffered(1))
    grid_blk = pl.BlockSpec((1, grid_rows, cols, dm), lambda b, t: (b, 0, t, 0))
    kern = functools.partial(_ffn_kernel, base=base, d_ff=d_ff, final=final, mixed=mixed is not None,
                             cols_in=cols if to_column else 0, cols_out=cols if to_raster else 0)
    extra, extra_specs = (), []
    if mixed is not None:
        extra = tuple(mixed)
        extra_specs = [tok(a.shape[-1]) for a in mixed[:3]] + [const(mixed[3])]
    w_bytes = (w_in.size + w_out.size + (mixed[3].size if mixed is not None else 0)) * 2
    est = w_bytes + 8 * tm * dm * 4 + 3 * tm * d_ff * 4 + 6 * 1024 * 1024
    out = pl.pallas_call(
        kern,
        grid=(bsz, lt // tm),
        in_specs=[grid_blk if to_column else tok(dm), _mod_spec(mod), const(w_in), const(w_out),
                  pl.BlockSpec((1, dm), lambda b, t: (0, 0))] + extra_specs,
        out_specs=grid_blk if to_raster else tok(dm),
        out_shape=jax.ShapeDtypeStruct((bsz, grid_rows, GRID_W, dm) if to_raster else (bsz, lt, dm), F32),
        compiler_params=pltpu.CompilerParams(dimension_semantics=("arbitrary", "arbitrary"),
                                             vmem_limit_bytes=_vmem_limit(est)),
        name="ffn",
    )(h.reshape(bsz, grid_rows, GRID_W, dm) if to_column else h, mod, w_in, w_out, final_w, *extra)
    return out.reshape(bsz, lt, dm)


IN_SPLIT = ((512, BF16), (768, F32), (128, F32), (256, F32), (256, BF16), (512, F32), (256, BF16), (256, BF16))


def _inproj_kernel(h_ref, mod_ref, w_ref, *out_refs):
    m = mod_ref[0]
    u = (_rms(h_ref[0]) * (1.0 + m[4:5]) + m[3:4]).astype(BF16)
    p = jnp.dot(u, w_ref[...], preferred_element_type=F32)
    off = 0
    for ref, (wd, dt) in zip(out_refs, IN_SPLIT):
        ref[0] = p[:, off:off + wd].astype(dt)
        off += wd


def _inproj(h, mod, w):
    bsz, lt, dm = h.shape
    tm = _token_tile(lt)
    tok = lambda wd: pl.BlockSpec((1, tm, wd), lambda b, t: (b, t, 0))
    return pl.pallas_call(
        _inproj_kernel,
        grid=(bsz, lt // tm),
        in_specs=[tok(dm), _mod_spec(mod),
                  pl.BlockSpec(w.shape, lambda b, t: (0, 0), pipeline_mode=pl.Buffered(1))],
        out_specs=[tok(wd) for wd, _ in IN_SPLIT],
        out_shape=[jax.ShapeDtypeStruct((bsz, lt, wd), dt) for wd, dt in IN_SPLIT],
        compiler_params=pltpu.CompilerParams(dimension_semantics=("arbitrary", "arbitrary"),
                                             vmem_limit_bytes=_vmem_limit(40 * 1024 * 1024)),
        name="inproj",
    )(h, mod, w)


def _seq_spec(a):
    return pl.BlockSpec((1,) + a.shape[1:], lambda b: (b,) + (0,) * (a.ndim - 1))


def _full_spec(a):
    return pl.BlockSpec(a.shape, lambda b: (0,) * a.ndim)


def _ssd_kernel(zc_ref, zl_ref, xc_in_ref, xl_in_ref, dtc_ref, dtl_ref, cw_ref, cb_ref, dtb_ref, aneg_ref,
                dsk_ref, nw_ref, oc_ref, ol_ref, xconv_ref, yf_ref, yb_ref, st_ref, *, q):
    inner = A_HEADS * A_HEAD_DIM
    gw = A_STATE
    hpg = A_HEADS // A_GROUPS
    gl = hpg * A_HEAD_DIM
    n_ctx = oc_ref.shape[1]
    segs = ((zc_ref, xc_in_ref, dtc_ref, oc_ref, 0), (zl_ref, xl_in_ref, dtl_ref, ol_ref, n_ctx))

    cw = cw_ref[...]
    cb = cb_ref[...]
    for _, xin_ref, _, o_ref, base in segs:
        ls = o_ref.shape[1]
        n_chunks = ls // q

        def conv_body(i, carry, xin_ref=xin_ref, base=base, ls=ls, n_chunks=n_chunks):
            t0 = pl.multiple_of(i * q, q)
            cur = xin_ref[0, pl.ds(t0, q), :]
            p0 = pl.multiple_of(jnp.maximum(t0 - V7X_SUBLANES, 0), V7X_SUBLANES)
            n0 = pl.multiple_of(jnp.minimum(t0 + q, ls - V7X_SUBLANES), V7X_SUBLANES)
            prev = jnp.where(i == 0, 0.0, xin_ref[0, pl.ds(p0, V7X_SUBLANES), :])
            nxt = jnp.where(i == n_chunks - 1, 0.0, xin_ref[0, pl.ds(n0, V7X_SUBLANES), :])
            ext = jnp.concatenate([prev, cur, nxt], axis=0)
            acc = cb
            for k in range(A_CONV):
                s0 = V7X_SUBLANES - A_CONV // 2 + k
                acc = acc + cw[k:k + 1] * ext[s0:s0 + q]
            xconv_ref[pl.ds(pl.multiple_of(base + t0, q), q), :] = _silu(acc)
            return carry

        lax.fori_loop(0, n_chunks, conv_body, 0)

    st_ref[...] = jnp.zeros(st_ref.shape, F32)

    rowi = lax.broadcasted_iota(jnp.int32, (q, q), 0)
    coli = lax.broadcasted_iota(jnp.int32, (q, q), 1)
    er = lax.broadcasted_iota(jnp.int32, (V7X_LANES, inner), 0)
    ec = lax.broadcasted_iota(jnp.int32, (V7X_LANES, inner), 1)
    dtb = dtb_ref[...]
    aneg = aneg_ref[...]

    def load_chunk(seg, t0):
        dt_ref, base = seg[2], seg[4]
        xc = xconv_ref[pl.ds(pl.multiple_of(base + t0, q), q), :]
        dt = jax.nn.softplus(dt_ref[0, pl.ds(t0, q), :] + dtb)
        return xc[:, :inner], xc[:, inner:inner + A_GROUPS * gw], xc[:, inner + A_GROUPS * gw:], dt

    def scan_sum(d, d_a):
        tri = (rowi >= coli) if d == 0 else (rowi <= coli)
        return _dot01_left(jnp.where(tri, 1.0, 0.0).astype(BF16), d_a)

    def carried(d, xs, bm, cm, dt, cum):
        last = q - 1 if d == 0 else 0
        expand = jnp.where(er == d * A_HEADS + (ec >> int(math.log2(A_HEAD_DIM))), 1.0, 0.0).astype(BF16)
        e_hi, e_mid, _ = _split3(jnp.exp(cum))
        ecum = (jnp.dot(e_mid, expand, preferred_element_type=F32) + jnp.dot(e_hi, expand, preferred_element_type=F32))
        wcol = (jnp.exp(cum[last:last + 1, :] - cum) * dt).astype(BF16)
        xw = (xs * jnp.dot(wcol, expand, preferred_element_type=F32)).astype(BF16)
        e_last = ecum[last:last + 1, :]
        bm_t = bm.T
        ys = []
        for g in range(A_GROUPS):
            cg = cm[:, g * gw:(g + 1) * gw].astype(BF16)
            st = st_ref[d, g]
            ys.append(jnp.dot(cg, st.astype(BF16), preferred_element_type=F32) * ecum[:, g * gl:(g + 1) * gl])
            upd = jnp.dot(bm_t[g * gw:(g + 1) * gw].astype(BF16), xw[:, g * gl:(g + 1) * gl],
                          preferred_element_type=F32)
            st_ref[d, g] = st * e_last[:, g * gl:(g + 1) * gl] + upd
        return jnp.concatenate(ys, axis=1)

    lower = rowi > coli
    diag = rowi == coli

    def step(seg, i, n_chunks):
        t0 = pl.multiple_of(i * q, q)
        xs, bm, cm, dt = load_chunk(seg, t0)
        d_a = dt * aneg
        cum_f = scan_sum(0, d_a)
        cum_b = scan_sum(1, d_a)
        log_dt = jnp.log(dt)
        col_f, col_b = cum_f, cum_b
        row_f = (cum_f - log_dt).T
        row_b = (cum_b - log_dt).T
        dt_t = dt.T
        xs_bf = xs.astype(BF16)
        ys = []
        for g in range(A_GROUPS):
            cg = cm[:, g * gw:(g + 1) * gw].astype(BF16)
            bg = bm[:, g * gw:(g + 1) * gw].astype(BF16)
            gmat = _dot_nt(cg, bg)
            for hh in range(hpg):
                h = g * hpg + hh
                hb = A_HEADS + h
                arg = jnp.where(lower, col_f[:, h:h + 1] - row_f[h:h + 1, :], col_b[:, hb:hb + 1] - row_b[hb:hb + 1, :])
                wgt = jnp.where(diag, dt_t[h:h + 1, :] + dt_t[hb:hb + 1, :], jnp.exp(arg))
                sc = (gmat * wgt).astype(BF16)
                ys.append(jnp.dot(sc, xs_bf[:, h * A_HEAD_DIM:(h + 1) * A_HEAD_DIM], preferred_element_type=F32))
        yf_ref[pl.ds(pl.multiple_of(seg[4] + t0, q), q), :] = (jnp.concatenate(ys, axis=1)
                                                              + carried(0, xs, bm, cm, dt, cum_f))
        tb = pl.multiple_of((n_chunks - 1 - i) * q, q)
        xsb, bmb, cmb, dtb_ = load_chunk(seg, tb)
        yb_ref[pl.ds(pl.multiple_of(seg[4] + tb, q), q), :] = carried(1, xsb, bmb, cmb, dtb_, scan_sum(1, dtb_ * aneg))

    dsk = dsk_ref[...]
    nw = nw_ref[...]
    for seg in segs:
        n_chunks = seg[3].shape[1] // q
        lax.fori_loop(0, n_chunks, lambda i, carry, seg=seg, n_chunks=n_chunks: (step(seg, i, n_chunks), carry)[1], 0,
                      unroll=math.gcd(n_chunks, SCAN_UNROLL))

    for z_ref, _, _, o_ref, base in segs:
        def fin_body(i, carry, z_ref=z_ref, o_ref=o_ref, base=base):
            t0 = pl.multiple_of(i * q, q)
            xs = xconv_ref[pl.ds(pl.multiple_of(base + t0, q), q), :inner]
            rows = pl.ds(pl.multiple_of(base + t0, q), q)
            y = yf_ref[rows, :] + yb_ref[rows, :] + dsk * xs
            y = y * _silu(z_ref[0, pl.ds(t0, q), :].astype(F32))
            o_ref[0, pl.ds(t0, q), :] = (_rms(y) * nw).astype(o_ref.dtype)
            return carry

        lax.fori_loop(0, o_ref.shape[1] // q, fin_body, 0)


def _ssd(zc, zl, xc, xl, dtc, dtl, cw, cb, dtb, aneg, dsk, nw):
    bsz, n_ctx, inner = zc.shape
    seq = zl.shape[1]
    lt = n_ctx + seq
    cd = xc.shape[-1]
    ins = (zc, zl, xc, xl, dtc, dtl)
    consts = (cw, cb, dtb, aneg, dsk, nw)
    est = 2 * lt * (2 * inner + 4 * cd + 4 * V7X_LANES + 4 * inner) + 4 * lt * (cd + inner) + 12 * 1024 * 1024
    return pl.pallas_call(
        functools.partial(_ssd_kernel, q=MIX_CHUNK),
        grid=(bsz,),
        in_specs=[_seq_spec(a) for a in ins] + [_full_spec(a) for a in consts],
        out_specs=[pl.BlockSpec((1, n_ctx, inner), lambda b: (b, 0, 0)),
                   pl.BlockSpec((1, seq, inner), lambda b: (b, 0, 0))],
        out_shape=[jax.ShapeDtypeStruct((bsz, n_ctx, inner), BF16), jax.ShapeDtypeStruct((bsz, seq, inner), BF16)],
        scratch_shapes=[pltpu.VMEM((lt, cd), F32),
                        pltpu.VMEM((lt, inner), F32),
                        pltpu.VMEM((lt, inner), F32),
                        pltpu.VMEM((2, A_GROUPS, A_STATE, inner // A_GROUPS), F32)],
        compiler_params=pltpu.CompilerParams(dimension_semantics=("arbitrary",),
                                             vmem_limit_bytes=_vmem_limit(est)),
        name="ssd_mixer",
    )(*ins, *consts)


def _hgrn_kernel(qc_ref, ql_ref, ffc_ref, ffl_ref, fbc_ref, fbl_ref, vc_ref, vl_ref, gc_ref, gl_ref,
                 lb_ref, nw_ref, oc_ref, ol_ref, of_ref, ob_ref, st_ref, *, c):
    w = C_HEADS * C_KEY
    nv = c // V7X_SUBLANES
    segs = ((qc_ref, (ffc_ref, fbc_ref), vc_ref, gc_ref, oc_ref, 0),
            (ql_ref, (ffl_ref, fbl_ref), vl_ref, gl_ref, ol_ref, oc_ref.shape[1]))
    st_ref[...] = jnp.zeros(st_ref.shape, F32)

    rowi = lax.broadcasted_iota(jnp.int32, (c, c), 0)
    coli = lax.broadcasted_iota(jnp.int32, (c, c), 1)
    pair_bits = rowi ^ coli
    hr = lax.broadcasted_iota(jnp.int32, (w, w), 0)
    hc = lax.broadcasted_iota(jnp.int32, (w, w), 1)
    same_head = (hr >> int(math.log2(C_KEY))) == (hc >> int(math.log2(C_KEY)))
    bones = jnp.where(same_head, 1.0, 0.0).astype(BF16)
    trow = lax.broadcasted_iota(jnp.int32, (c, w), 0)
    sub3 = lax.broadcasted_iota(jnp.int32, (nv, V7X_SUBLANES, w), 1)

    def boundary_small(x3, p0, s):
        b = 1
        while b <= s:
            if p0 & b:
                x3 = jnp.where((sub3 & b) != 0, x3, pltpu.roll(x3, V7X_SUBLANES - b, axis=1))
            else:
                x3 = jnp.where((sub3 & b) != 0, pltpu.roll(x3, b, axis=1), x3)
            b *= 2
        return x3

    def gates(d, seg, smp, t0):
        zf = seg[1][d][smp, pl.ds(t0, c), :]
        lb = lb_ref[d:d + 1, :]
        f = lb + (1.0 - lb) * jax.nn.sigmoid(zf)
        causal = (rowi >= coli) if d == 0 else (rowi <= coli)
        cum = _dot01_left(jnp.where(causal, 1.0, 0.0).astype(BF16), jnp.log(f))
        return 1.0 - f, cum

    def boundary(cum, d, s):
        p0 = (s - 1) if d == 0 else s
        return boundary_small(cum.reshape(nv, V7X_SUBLANES, w), p0, s).reshape(c, w)

    def carried(d, smp, qs, kk, cum, v32):
        last = c - 1 if d == 0 else 0
        cum_last = cum[last:last + 1, :]
        st = st_ref[smp, d]
        qe = (qs * jnp.exp(cum)).astype(BF16)
        kw = (kk * jnp.exp(cum_last - cum)).astype(BF16)
        upd = jnp.dot(v32.T.astype(BF16), kw, preferred_element_type=F32)
        st_ref[smp, d] = jnp.where(same_head, st * jnp.exp(cum_last) + upd, 0.0)
        return _dot_nt(qe, st.astype(BF16))

    def step(seg, smp, i, n_chunks):
        q_ref, _, v_ref, _, o_ref, base = seg
        t0 = pl.multiple_of(i * c, c)
        qs = _silu(q_ref[smp, pl.ds(t0, c), :].astype(F32))
        v = v_ref[smp, pl.ds(t0, c), :]
        v32 = v.astype(F32)
        kk0, cum0 = gates(0, seg, smp, t0)
        kk1, cum1 = gates(1, seg, smp, t0)

        out = jnp.dot((qs * (kk0 + kk1)).astype(BF16), bones, preferred_element_type=F32) * v32
        scores = [jnp.zeros((c, c), F32) for _ in range(C_HEADS)]
        s = 1
        while s < c:
            if s < V7X_SUBLANES:
                e0 = jnp.exp2(jnp.abs(cum0 - boundary(cum0, 0, s)) * NEG_LOG2E)
                e1 = jnp.exp2(jnp.abs(cum1 - boundary(cum1, 1, s)) * NEG_LOG2E)
                has_bit = (trow & s) != 0
                qm = (qs * jnp.where(has_bit, e0, e1)).astype(BF16)
                km = jnp.where(has_bit, kk1 * e1, kk0 * e0).astype(BF16)
            else:
                halves = lambda a: a.reshape(c // (2 * s), 2, s, w)
                c0, c1, q2 = halves(cum0), halves(cum1), halves(qs)
                x0 = c0 - c0[:, 0:1, s - 1:s, :]
                x1 = c1 - c1[:, 1:2, 0:1, :]
                qarg = jnp.concatenate([x1[:, 0:1], x0[:, 1:2]], axis=1)
                karg = jnp.concatenate([x0[:, 0:1], x1[:, 1:2]], axis=1)
                ksel = jnp.concatenate([halves(kk0)[:, 0:1], halves(kk1)[:, 1:2]], axis=1)
                qm = (q2 * jnp.exp2(qarg * (-NEG_LOG2E))).reshape(c, w).astype(BF16)
                km = (ksel * jnp.exp2(karg * NEG_LOG2E)).reshape(c, w).astype(BF16)
            take = pair_bits >= s
            for h in range(C_HEADS):
                sl = slice(h * C_KEY, (h + 1) * C_KEY)
                scores[h] = jnp.where(take, _dot_nt(qm[:, sl], km[:, sl]), scores[h])
            s *= 2
        out = out + jnp.concatenate(
            [jnp.dot(scores[h].astype(BF16), v[:, h * C_KEY:(h + 1) * C_KEY], preferred_element_type=F32)
             for h in range(C_HEADS)], axis=1)
        of_ref[smp, pl.ds(pl.multiple_of(base + t0, c), c), :] = out + carried(0, smp, qs, kk0, cum0, v32)
        tb = pl.multiple_of((n_chunks - 1 - i) * c, c)
        qsb = _silu(q_ref[smp, pl.ds(tb, c), :].astype(F32))
        kkb, cumb = gates(1, seg, smp, tb)
        ob_ref[smp, pl.ds(pl.multiple_of(base + tb, c), c), :] = carried(
            1, smp, qsb, kkb, cumb, v_ref[smp, pl.ds(tb, c), :].astype(F32))

    n_smp = oc_ref.shape[0]
    for seg in segs:
        n_chunks = seg[4].shape[1] // c

        def scan_body(i, carry, seg=seg, n_chunks=n_chunks):
            for smp in range(n_smp):
                step(seg, smp, i, n_chunks)
            return carry

        lax.fori_loop(0, n_chunks, scan_body, 0, unroll=math.gcd(n_chunks, SCAN_UNROLL))

    nw = nw_ref[...]
    for _, _, _, g_ref, o_ref, base in segs:
        def fin_body(i, carry, g_ref=g_ref, o_ref=o_ref, base=base):
            t0 = pl.multiple_of(i * c, c)
            for smp in range(n_smp):
                rows = pl.ds(pl.multiple_of(base + t0, c), c)
                o = of_ref[smp, rows, :] + ob_ref[smp, rows, :]
                ms = _dot01_right(o * o, bones) * (1.0 / C_KEY)
                y = o * lax.rsqrt(ms + EPS) * nw
                o_ref[smp, pl.ds(t0, c), :] = (y * _silu(g_ref[smp, pl.ds(t0, c), :].astype(F32))).astype(o_ref.dtype)
            return carry

        lax.fori_loop(0, o_ref.shape[1] // c, fin_body, 0)


def _hgrn(ctx_parts, lat_parts, lower, nw):
    qc, fc, ic, gc = ctx_parts
    ql, fl, il, gl = lat_parts
    bsz, n_ctx, w = qc.shape
    seq = ql.shape[1]
    ns = math.gcd(bsz, HG_SAMPLES_PER_STEP)
    blk = lambda a, j: pl.BlockSpec((ns, a.shape[1], w), lambda b: (b, 0, j))
    ins = (qc, ql, fc, fl, fc, fl, ic, il, gc, gl)
    specs = [blk(qc, 0), blk(ql, 0), blk(fc, 0), blk(fl, 0), blk(fc, 1), blk(fl, 1),
             blk(ic, 0), blk(il, 0), blk(gc, 0), blk(gl, 0)]
    est = ns * 2 * (n_ctx + seq) * w * (2 + 8 + 2 + 2 + 4) + 20 * 1024 * 1024
    return pl.pallas_call(
        functools.partial(_hgrn_kernel, c=HG_CHUNK),
        grid=(bsz // ns,),
        in_specs=specs + [_full_spec(lower), _full_spec(nw)],
        out_specs=[blk(qc, 0), blk(ql, 0)],
        out_shape=[jax.ShapeDtypeStruct((bsz, n_ctx, w), BF16), jax.ShapeDtypeStruct((bsz, seq, w), BF16)],
        scratch_shapes=[pltpu.VMEM((ns, n_ctx + seq, w), F32), pltpu.VMEM((ns, n_ctx + seq, w), F32),
                        pltpu.VMEM((ns, 2, w, w), F32)],
        compiler_params=pltpu.CompilerParams(dimension_semantics=("arbitrary",),
                                             vmem_limit_bytes=_vmem_limit(est)),
        name="hgrn_mixer",
    )(*ins, lower, nw)


def _s5_tables(lam_re, lam_im, log_step, b_re, b_im, c_re, c_im):
    q = S5_CHUNK
    hp = lax.Precision.HIGHEST
    lam_re = lam_re.astype(F32)
    lam_im = lam_im.astype(F32)
    step = jnp.exp(log_step.astype(F32))[..., None]
    tau = jnp.arange(q + 1, dtype=F32)[:, None, None, None]
    mag = jnp.exp(lam_re * step * tau)
    pr = mag * jnp.cos(lam_im * step * tau)
    pi = mag * jnp.sin(lam_im * step * tau)
    ar, ai = pr[1], pi[1]
    den = lam_re * lam_re + lam_im * lam_im
    nr = ar - 1.0
    kr = (nr * lam_re + ai * lam_im) / den
    ki = (ai * lam_re - nr * lam_im) / den
    b_re = b_re.astype(F32)
    b_im = b_im.astype(F32)
    br = kr[..., None] * b_re - ki[..., None] * b_im
    bi = kr[..., None] * b_im + ki[..., None] * b_re
    c_re = c_re.astype(F32)
    c_im = c_im.astype(F32)
    qr = pr[..., None] * br - pi[..., None] * bi
    qi = pr[..., None] * bi + pi[..., None] * br
    hk = jnp.einsum('dgon,tdgni->tdgoi', c_re, qr, precision=hp) - jnp.einsum('dgon,tdgni->tdgoi', c_im, qi, precision=hp)
    g, cg = b_re.shape[1], b_re.shape[3]
    width = q * cg
    strip = lambda h: h.transpose(1, 3, 0, 2).reshape(g, cg, width)
    zeros = jnp.zeros((g, cg, width - cg), F32)
    pad0 = jnp.concatenate([zeros, strip(hk[:q, 0])], axis=-1)
    pad1 = jnp.concatenate([strip(hk[:q, 1][::-1]), zeros], axis=-1)
    rows = lambda pad: jnp.stack([pad[..., (q - 1 - s) * cg:(q - 1 - s) * cg + width] for s in range(q)], axis=1)
    toe = jnp.stack([rows(pad0), rows(pad1)]).reshape(2, g, width, width)
    by_dir = lambda a: jnp.stack([a[:q, 0][::-1], a[:q, 1]])
    wst = jnp.concatenate([by_dir(qr), by_dir(qi)], axis=3)
    wst = wst.transpose(0, 2, 1, 4, 3).reshape(2, g, width, -1)
    out_pow = lambda a: jnp.stack([a[1:, 0], a[1:, 1][::-1]])
    po_r, po_i = out_pow(pr), out_pow(pi)
    w_xr = c_re[:, None] * po_r[:, :, :, None, :] - c_im[:, None] * po_i[:, :, :, None, :]
    w_xi = -c_re[:, None] * po_i[:, :, :, None, :] - c_im[:, None] * po_r[:, :, :, None, :]
    wout = jnp.concatenate([w_xr, w_xi], axis=-1)
    wout = wout.transpose(0, 2, 4, 1, 3).reshape(2, g, -1, width)
    dec = jnp.stack([jnp.concatenate([pr[q], pr[q]], axis=-1),
                     jnp.concatenate([-pi[q], pi[q]], axis=-1),
                     jnp.concatenate([pi[q], -pi[q]], axis=-1)], axis=1)
    return toe.astype(BF16), wst.astype(BF16), wout.astype(BF16), dec


def _lane_block_transpose(arrs, blk):
    n = len(arrs)
    rows, width = arrs[0].shape
    lane_blk = lax.broadcasted_iota(jnp.int32, (rows, width), 1) >> int(math.log2(blk))
    k = 1
    while k < n:
        hi_half = (lane_blk & k) != 0
        new = list(arrs)
        for a in range(n):
            if a & k == 0:
                lo_arr, hi_arr = arrs[a], arrs[a | k]
                new[a] = jnp.where(hi_half, pltpu.roll(hi_arr, blk * k, axis=1), lo_arr)
                new[a | k] = jnp.where(hi_half, hi_arr, pltpu.roll(lo_arr, width - blk * k, axis=1))
        arrs = new
        k *= 2
    return arrs


def _gelu_tanh(x):
    return 0.5 * x * (1.0 + jnp.tanh(math.sqrt(2.0 / math.pi) * (x + 0.044715 * (x * x * x))))


def _s5_kernel(uc_ref, ul_ref, toe_ref, wst_ref, wout_ref, dec_ref, sd_ref, gw_ref, gb_ref,
               oc_ref, ol_ref, z_ref, xin_ref):
    q = uc_ref.shape[2]
    ncc = uc_ref.shape[1]
    n_chunks = ncc + ul_ref.shape[1]
    ng = toe_ref.shape[1]
    half = dec_ref.shape[-1] // 2
    rows = [jnp.concatenate([uc_ref[0, :, s, :], ul_ref[0, :, s, :]], axis=0) for s in range(q)]
    ug = [a.astype(BF16) for a in _lane_block_transpose(rows, B_GROUP)]
    ys = [None] * ng
    for d in range(2):
        zs = []
        for g in range(ng):
            ys[g] = jnp.dot(ug[g], toe_ref[d, g], preferred_element_type=F32) + (0.0 if ys[g] is None else ys[g])
            zs.append(jnp.dot(ug[g], wst_ref[d, g], preferred_element_type=F32))
        z = pltpu.einshape("gcn->cgn", jnp.stack(zs))
        z_ref[0, d] = z
        z_ref[1, d] = pltpu.roll(z, half, axis=2)

    coef = [[dec_ref[d, j] for j in range(3)] for d in range(2)]

    def body(i, carry):
        nxt = []
        for d in range(2):
            if d == 0:
                c_idx = i
            else:
                c_idx = jnp.where(i < ncc, ncc - 1 - i, n_chunks - 1 - (i - ncc))
            x, xs = carry[2 * d], carry[2 * d + 1]
            a1, a2, a2s = coef[d]
            xin_ref[d, c_idx] = x
            nxt.append(a1 * x + a2 * xs + z_ref[0, d, c_idx])
            nxt.append(a1 * xs + a2s * x + z_ref[1, d, c_idx])
        return tuple(nxt)

    zero = jnp.zeros(coef[0][0].shape, F32)
    lax.fori_loop(0, n_chunks, body, (zero, zero, zero, zero), unroll=2)
    for d in range(2):
        xg = pltpu.einshape("cgn->gcn", xin_ref[d]).astype(BF16)
        for g in range(ng):
            ys[g] = ys[g] + jnp.dot(xg[g], wout_ref[d, g], preferred_element_type=F32)
    yt = _lane_block_transpose(ys, B_GROUP)
    sd = sd_ref[...]
    gw = gw_ref[...]
    gb = gb_ref[...]
    for t in range(q):
        y = _gelu_tanh(yt[t] + sd * rows[t])
        gate = jnp.dot(y.astype(BF16), gw, preferred_element_type=F32) + gb
        y = y * jax.nn.sigmoid(gate)
        oc_ref[0, :, t, :] = y[:ncc]
        ol_ref[0, :, t, :] = y[ncc:]


def _s5(pbc, pbl, toe, wst, wout, dec, s5_d, glu_w, glu_b):
    bsz, n_ctx, wd = pbc.shape
    seq = pbl.shape[1]
    q = S5_CHUNK
    ng = toe.shape[1]
    ns = wst.shape[-1]
    n_chunks = (n_ctx + seq) // q
    uc = pbc.reshape(bsz, n_ctx // q, q, wd)
    ul = pbl.reshape(bsz, seq // q, q, wd)
    consts = (toe, wst, wout, dec, s5_d, glu_w, glu_b)
    oc, ol = pl.pallas_call(
        _s5_kernel,
        grid=(bsz,),
        in_specs=[_seq_spec(uc), _seq_spec(ul)] + [_full_spec(a) for a in consts],
        out_specs=[_seq_spec(uc), _seq_spec(ul)],
        out_shape=[jax.ShapeDtypeStruct(uc.shape, F32), jax.ShapeDtypeStruct(ul.shape, F32)],
        scratch_shapes=[pltpu.VMEM((2, 2, n_chunks, ng, ns), F32), pltpu.VMEM((2, n_chunks, ng, ns), F32)],
        compiler_params=pltpu.CompilerParams(dimension_semantics=("arbitrary",),
                                             vmem_limit_bytes=_vmem_limit(40 * 1024 * 1024)),
        name="s5_mixer",
    )(uc, ul, *consts)
    return oc.reshape(bsz, n_ctx, wd), ol.reshape(bsz, seq, wd)


def _column_to_raster(t, rows):
    b, s, d = t.shape
    return t.reshape(b, GRID_W, rows, d).transpose(0, 2, 1, 3).reshape(b, s, d)


def kernel(x, c, ctx, c_ctx, mod_w, mod_b, ffn_w_in, ffn_w_out, w_in, w_out, a_conv_w, a_conv_b, a_dt_bias, a_log,
           a_d, a_norm_w, s5_lam_re, s5_lam_im, s5_log_step, s5_b_re, s5_b_im, s5_c_re, s5_c_im, s5_d, s5_glu_w,
           s5_glu_b, hg_lb_logits, hg_norm_w, final_norm_w):
    bsz, seq, dm = x.shape
    n_ctx = ctx.shape[1]
    depth = mod_w.shape[0]
    rows = seq // GRID_W
    assert n_ctx % MIX_CHUNK == 0 and seq % MIX_CHUNK == 0 and seq % GRID_W == 0
    a_inner = A_HEADS * A_HEAD_DIM
    a_conv_dim = a_conv_w.shape[-1]
    c_width = hg_norm_w.shape[-1]

    pad = (-(bsz + 1)) % V7X_SUBLANES
    cvec = jnp.concatenate([c, c_ctx[None, :], jnp.zeros((pad, dm), F32)], axis=0)
    mod_all = _mod_vectors(cvec, mod_w, mod_b)

    p_lb = jax.nn.softmax(hg_lb_logits.astype(F32), axis=0)
    lower_bounds = jnp.cumsum(p_lb, axis=0) - p_lb[:1]
    fw = final_norm_w.reshape(1, dm).astype(F32)
    toe_all, wst_all, wout_all, dec_all = jax.vmap(_s5_tables)(s5_lam_re, s5_lam_im, s5_log_step, s5_b_re, s5_b_im,
                                                               s5_c_re, s5_c_im)

    h_lat, h_ctx = x, ctx
    col_order = False
    for l in range(depth):
        last = l == depth - 1
        m_lat = mod_all[l, :bsz].reshape(bsz, N_MOD, dm)
        m_ctx = mod_all[l, bsz].reshape(1, N_MOD, dm)
        wi0, wo0 = ffn_w_in[l, 0].astype(BF16), ffn_w_out[l, 0].astype(BF16)
        wi1, wo1 = ffn_w_in[l, 1].astype(BF16), ffn_w_out[l, 1].astype(BF16)

        want_col = l % 2 == 1
        if col_order and not want_col:
            h_lat = _column_to_raster(h_lat, rows)
            col_order = False
        h_lat = _ffn(h_lat, m_lat, wi0, wo0, fw, base=0, final=False, to_column=want_col and not col_order)
        col_order = want_col
        h_ctx = _ffn(h_ctx, m_ctx, wi0, wo0, fw, base=0, final=False)

        wl = w_in[l]
        o_dt = a_inner + a_conv_dim
        o_b = o_dt + 2 * A_HEADS
        w_cat = jnp.concatenate([wl[:, :o_b], jnp.zeros((dm, V7X_LANES - 2 * A_HEADS), F32), wl[:, o_b:]],
                                axis=1).astype(BF16)
        zl, xl, dtl, pbl, ql, fl, il, gl = _inproj(h_lat, m_lat, w_cat)
        zc, xc, dtc, pbc, qc, fc, ic, gc = _inproj(h_ctx, m_ctx, w_cat)

        lane_pad = jnp.zeros((V7X_LANES - 2 * A_HEADS,), F32)
        dtb = jnp.concatenate([a_dt_bias[l].reshape(-1), lane_pad]).reshape(1, V7X_LANES)
        aneg = jnp.concatenate([-jnp.exp(a_log[l].astype(F32)).reshape(-1), lane_pad]).reshape(1, V7X_LANES)
        dsk = jnp.repeat(a_d[l].astype(F32), A_HEAD_DIM).reshape(1, a_inner)
        ya_c, ya_l = _ssd(zc, zl, xc, xl, dtc, dtl, a_conv_w[l], a_conv_b[l].reshape(1, -1), dtb, aneg, dsk,
                          a_norm_w[l].reshape(1, -1))

        yb_c, yb_l = _s5(pbc, pbl, toe_all[l], wst_all[l], wout_all[l], dec_all[l], s5_d[l].reshape(1, -1),
                         s5_glu_w[l].astype(BF16), s5_glu_b[l].reshape(1, -1))

        yc_c, yc_l = _hgrn((qc, fc, ic, gc), (ql, fl, il, gl), lower_bounds[l].astype(F32),
                           hg_norm_w[l].reshape(1, c_width))

        w_o = w_out[l].astype(BF16)
        h_lat = _ffn(h_lat, m_lat, wi1, wo1, fw, base=6, final=last, to_raster=last and col_order,
                     mixed=(ya_l, yb_l, yc_l, w_o))
        if last:
            col_order = False
        else:
            h_ctx = _ffn(h_ctx, m_ctx, wi1, wo1, fw, base=6, final=False, mixed=(ya_c, yb_c, yc_c, w_o))
    return h_lat
```

```python
import functools
import math

import jax
import jax.numpy as jnp
from jax import lax
from jax.experimental import pallas as pl
from jax.experimental.pallas import tpu as pltpu

F32 = jnp.float32
BF16 = jnp.bfloat16
EPS = 1e-6
NEG_LOG2E = -1.4426950408889634

V7X_VMEM_BYTES = 64 * 1024 * 1024
V7X_LANES = 128
V7X_SUBLANES = 8
V7X_MXU_DIM = 256

GRID_W = 64
N_MOD = 9
A_HEADS = 8
A_HEAD_DIM = 64
A_GROUPS = 2
A_STATE = 64
A_CONV = 5
B_GROUP = 16
C_HEADS = 4
C_KEY = 64

TOKEN_TILES = (512, 384, 256)
MIX_CHUNK = 256
HG_CHUNK = 256
SCAN_UNROLL = 4
HG_SAMPLES_PER_STEP = 1
S5_CHUNK = 16
GRID_COLS_PER_TILE = 2 * V7X_SUBLANES


def _vmem_limit(estimate_bytes):
    return int(min(V7X_VMEM_BYTES - 6 * 1024 * 1024, max(estimate_bytes, 16 * 1024 * 1024)))


def _token_tile(n):
    for tm in TOKEN_TILES:
        if n % tm == 0:
            return tm
    raise ValueError(f"token count {n} has no supported tile")


def _silu(x):
    h = 0.5 * x
    return h + h * jnp.tanh(h)


def _rms(x):
    return x * lax.rsqrt(jnp.mean(x * x, axis=-1, keepdims=True) + EPS)


def _split3(x):
    hi = x.astype(BF16)
    r = x - hi.astype(F32)
    mid = r.astype(BF16)
    lo = (r - mid.astype(F32)).astype(BF16)
    return hi, mid, lo


def _dot01_left(m01, x):
    hi, mid, lo = _split3(x)
    d = lambda a: jnp.dot(m01, a, preferred_element_type=F32)
    return (d(lo) + d(mid)) + d(hi)


def _dot01_right(x, m01):
    hi, mid, lo = _split3(x)
    d = lambda a: jnp.dot(a, m01, preferred_element_type=F32)
    return (d(lo) + d(mid)) + d(hi)


def _dot_nt(a, b):
    return lax.dot_general(a, b, (((1,), (1,)), ((), ())), preferred_element_type=F32)


def _mod_spec(mod):
    per_sample = mod.shape[0] > 1
    return pl.BlockSpec((1,) + mod.shape[1:], lambda b, t: (b if per_sample else 0, 0, 0))


def _mod_kernel(c_ref, w_ref, b_ref, o_ref):
    s = _silu(c_ref[...]).astype(BF16)
    o_ref[0] = jnp.dot(s, w_ref[0].astype(BF16), preferred_element_type=F32) + b_ref[0]


def _mod_vectors(cvec, mod_w, mod_b):
    depth, dm, nm = mod_w.shape
    rows = cvec.shape[0]
    tn = 1024
    return pl.pallas_call(
        _mod_kernel,
        grid=(depth, nm // tn),
        in_specs=[pl.BlockSpec((rows, dm), lambda l, j: (0, 0)),
                  pl.BlockSpec((1, dm, tn), lambda l, j: (l, 0, j)),
                  pl.BlockSpec((1, 1, tn), lambda l, j: (l, 0, j))],
        out_specs=pl.BlockSpec((1, rows, tn), lambda l, j: (l, 0, j)),
        out_shape=jax.ShapeDtypeStruct((depth, rows, nm), F32),
        compiler_params=pltpu.CompilerParams(dimension_semantics=("arbitrary", "arbitrary")),
        name="mod_vectors",
    )(cvec, mod_w, mod_b.reshape(depth, 1, nm))


def _ff_chunks(d_ff):
    if d_ff % V7X_MXU_DIM:
        return ((0, d_ff),)
    tiles = d_ff // V7X_MXU_DIM
    first = (tiles + 1) // 2 * V7X_MXU_DIM
    return ((0, first), (first, d_ff - first)) if d_ff > first else ((0, d_ff),)


def _ffn_kernel(h_ref, mod_ref, win_ref, wout_ref, fw_ref, *rest, base, d_ff, final, cols_in, cols_out, mixed):
    o_ref = rest[-1]
    if cols_in:
        x = jnp.concatenate([h_ref[0, :, wv, :] for wv in range(cols_in)], axis=0)
    else:
        x = h_ref[0]
    m = mod_ref[0]
    if mixed:
        ya_ref, yb_ref, yc_ref, wmix_ref = rest[:4]
        mix = jnp.concatenate([r[0].astype(BF16) for r in (ya_ref, yb_ref, yc_ref)], axis=1)
        x = x + m[base - 1:base] * jnp.dot(mix, wmix_ref[...], preferred_element_type=F32)
    u = (_rms(x) * (1.0 + m[base + 1:base + 2]) + m[base:base + 1]).astype(BF16)
    acc = jnp.zeros(x.shape, F32)
    for c0, cw in _ff_chunks(d_ff):
        g = jnp.dot(u, win_ref[:, c0:c0 + cw], preferred_element_type=F32)
        up = jnp.dot(u, win_ref[:, d_ff + c0:d_ff + c0 + cw], preferred_element_type=F32)
        a = (_silu(g) * up).astype(BF16)
        acc = acc + jnp.dot(a, wout_ref[c0:c0 + cw, :], preferred_element_type=F32)
    y = x + 0.5 * m[base + 2:base + 3] * acc
    if final:
        y = _rms(y) * fw_ref[...]
    if cols_out:
        nr = y.shape[0] // cols_out
        for wv in range(cols_out):
            o_ref[0, :, wv, :] = y[wv * nr:(wv + 1) * nr]
    else:
        o_ref[0] = y


def _ffn(h, mod, w_in, w_out, final_w, *, base, final, to_column=False, to_raster=False, mixed=None):
    bsz, lt, dm = h.shape
    d_ff = w_out.shape[0]
    grid_rows = lt // GRID_W
    cols = GRID_COLS_PER_TILE
    tm = grid_rows * cols if (to_column or to_raster) else _token_tile(lt)
    tok = lambda wd: pl.BlockSpec((1, tm, wd), lambda b, t: (b, t, 0))
    const = lambda a: pl.BlockSpec(a.shape, lambda b, t: (0, 0), pipeline_mode=pl.Buffered(1))
    grid_blk = pl.BlockSpec((1, grid_rows, cols, dm), lambda b, t: (b, 0, t, 0))
    kern = functools.partial(_ffn_kernel, base=base, d_ff=d_ff, final=final, mixed=mixed is not None,
                             cols_in=cols if to_column else 0, cols_out=cols if to_raster else 0)
    extra, extra_specs = (), []
    if mixed is not None:
        extra = tuple(mixed)
        extra_specs = [tok(a.shape[-1]) for a in mixed[:3]] + [const(mixed[3])]
    w_bytes = (w_in.size + w_out.size + (mixed[3].size if mixed is not None else 0)) * 2
    est = w_bytes + 8 * tm * dm * 4 + 3 * tm * d_ff * 4 + 6 * 1024 * 1024
    out = pl.pallas_call(
        kern,
        grid=(bsz, lt // tm),
        in_specs=[grid_blk if to_column else tok(dm), _mod_spec(mod), const(w_in), const(w_out),
                  pl.BlockSpec((1, dm), lambda b, t: (0, 0))] + extra_specs,
        out_specs=grid_blk if to_raster else tok(dm),
        out_shape=jax.ShapeDtypeStruct((bsz, grid_rows, GRID_W, dm) if to_raster else (bsz, lt, dm), F32),
        compiler_params=pltpu.CompilerParams(dimension_semantics=("arbitrary", "arbitrary"),
                                             vmem_limit_bytes=_vmem_limit(est)),
        name="ffn",
    )(h.reshape(bsz, grid_rows, GRID_W, dm) if to_column else h, mod, w_in, w_out, final_w, *extra)
    return out.reshape(bsz, lt, dm)


IN_SPLIT = ((512, BF16), (768, F32), (128, F32), (256, F32), (256, BF16), (512, F32), (256, BF16), (256, BF16))


def _inproj_kernel(h_ref, mod_ref, w_ref, *out_refs):
    m = mod_ref[0]
    u = (_rms(h_ref[0]) * (1.0 + m[4:5]) + m[3:4]).astype(BF16)
    p = jnp.dot(u, w_ref[...], preferred_element_type=F32)
    off = 0
    for ref, (wd, dt) in zip(out_refs, IN_SPLIT):
        ref[0] = p[:, off:off + wd].astype(dt)
        off += wd


def _inproj(h, mod, w):
    bsz, lt, dm = h.shape
    tm = _token_tile(lt)
    tok = lambda wd: pl.BlockSpec((1, tm, wd), lambda b, t: (b, t, 0))
    return pl.pallas_call(
        _inproj_kernel,
        grid=(bsz, lt // tm),
        in_specs=[tok(dm), _mod_spec(mod),
                  pl.BlockSpec(w.shape, lambda b, t: (0, 0), pipeline_mode=pl.Buffered(1))],
        out_specs=[tok(wd) for wd, _ in IN_SPLIT],
        out_shape=[jax.ShapeDtypeStruct((bsz, lt, wd), dt) for wd, dt in IN_SPLIT],
        compiler_params=pltpu.CompilerParams(dimension_semantics=("arbitrary", "arbitrary"),
                                             vmem_limit_bytes=_vmem_limit(40 * 1024 * 1024)),
        name="inproj",
    )(h, mod, w)


def _seq_spec(a):
    return pl.BlockSpec((1,) + a.shape[1:], lambda b: (b,) + (0,) * (a.ndim - 1))


def _full_spec(a):
    return pl.BlockSpec(a.shape, lambda b: (0,) * a.ndim)


def _ssd_kernel(zc_ref, zl_ref, xc_in_ref, xl_in_ref, dtc_ref, dtl_ref, cw_ref, cb_ref, dtb_ref, aneg_ref,
                dsk_ref, nw_ref, oc_ref, ol_ref, xconv_ref, yf_ref, yb_ref, st_ref, *, q):
    inner = A_HEADS * A_HEAD_DIM
    gw = A_STATE
    hpg = A_HEADS // A_GROUPS
    gl = hpg * A_HEAD_DIM
    n_ctx = oc_ref.shape[1]
    segs = ((zc_ref, xc_in_ref, dtc_ref, oc_ref, 0), (zl_ref, xl_in_ref, dtl_ref, ol_ref, n_ctx))

    cw = cw_ref[...]
    cb = cb_ref[...]
    for _, xin_ref, _, o_ref, base in segs:
        ls = o_ref.shape[1]
        n_chunks = ls // q

        def conv_body(i, carry, xin_ref=xin_ref, base=base, ls=ls, n_chunks=n_chunks):
            t0 = pl.multiple_of(i * q, q)
            cur = xin_ref[0, pl.ds(t0, q), :]
            p0 = pl.multiple_of(jnp.maximum(t0 - V7X_SUBLANES, 0), V7X_SUBLANES)
            n0 = pl.multiple_of(jnp.minimum(t0 + q, ls - V7X_SUBLANES), V7X_SUBLANES)
            prev = jnp.where(i == 0, 0.0, xin_ref[0, pl.ds(p0, V7X_SUBLANES), :])
            nxt = jnp.where(i == n_chunks - 1, 0.0, xin_ref[0, pl.ds(n0, V7X_SUBLANES), :])
            ext = jnp.concatenate([prev, cur, nxt], axis=0)
            acc = cb
            for k in range(A_CONV):
                s0 = V7X_SUBLANES - A_CONV // 2 + k
                acc = acc + cw[k:k + 1] * ext[s0:s0 + q]
            xconv_ref[pl.ds(pl.multiple_of(base + t0, q), q), :] = _silu(acc)
            return carry

        lax.fori_loop(0, n_chunks, conv_body, 0)

    st_ref[...] = jnp.zeros(st_ref.shape, F32)

    rowi = lax.broadcasted_iota(jnp.int32, (q, q), 0)
    coli = lax.broadcasted_iota(jnp.int32, (q, q), 1)
    er = lax.broadcasted_iota(jnp.int32, (V7X_LANES, inner), 0)
    ec = lax.broadcasted_iota(jnp.int32, (V7X_LANES, inner), 1)
    dtb = dtb_ref[...]
    aneg = aneg_ref[...]

    def load_chunk(seg, t0):
        dt_ref, base = seg[2], seg[4]
        xc = xconv_ref[pl.ds(pl.multiple_of(base + t0, q), q), :]
        dt = jax.nn.softplus(dt_ref[0, pl.ds(t0, q), :] + dtb)
        return xc[:, :inner], xc[:, inner:inner + A_GROUPS * gw], xc[:, inner + A_GROUPS * gw:], dt

    def scan_sum(d, d_a):
        tri = (rowi >= coli) if d == 0 else (rowi <= coli)
        return _dot01_left(jnp.where(tri, 1.0, 0.0).astype(BF16), d_a)

    def carried(d, xs, bm, cm, dt, cum):
        last = q - 1 if d == 0 else 0
        expand = jnp.where(er == d * A_HEADS + (ec >> int(math.log2(A_HEAD_DIM))), 1.0, 0.0).astype(BF16)
        e_hi, e_mid, _ = _split3(jnp.exp(cum))
        ecum = (jnp.dot(e_mid, expand, preferred_element_type=F32) + jnp.dot(e_hi, expand, preferred_element_type=F32))
        wcol = (jnp.exp(cum[last:last + 1, :] - cum) * dt).astype(BF16)
        xw = (xs * jnp.dot(wcol, expand, preferred_element_type=F32)).astype(BF16)
        e_last = ecum[last:last + 1, :]
        bm_t = bm.T
        ys = []
        for g in range(A_GROUPS):
            cg = cm[:, g * gw:(g + 1) * gw].astype(BF16)
            st = st_ref[d, g]
            ys.append(jnp.dot(cg, st.astype(BF16), preferred_element_type=F32) * ecum[:, g * gl:(g + 1) * gl])
            upd = jnp.dot(bm_t[g * gw:(g + 1) * gw].astype(BF16), xw[:, g * gl:(g + 1) * gl],
                          preferred_element_type=F32)
            st_ref[d, g] = st * e_last[:, g * gl:(g + 1) * gl] + upd
        return jnp.concatenate(ys, axis=1)

    lower = rowi > coli
    diag = rowi == coli

    def step(seg, i, n_chunks):
        t0 = pl.multiple_of(i * q, q)
        xs, bm, cm, dt = load_chunk(seg, t0)
        d_a = dt * aneg
        cum_f = scan_sum(0, d_a)
        cum_b = scan_sum(1, d_a)
        log_dt = jnp.log(dt)
        col_f, col_b = cum_f, cum_b
        row_f = (cum_f - log_dt).T
        row_b = (cum_b - log_dt).T
        dt_t = dt.T
        xs_bf = xs.astype(BF16)
        ys = []
        for g in range(A_GROUPS):
            cg = cm[:, g * gw:(g + 1) * gw].astype(BF16)
            bg = bm[:, g * gw:(g + 1) * gw].astype(BF16)
            gmat = _dot_nt(cg, bg)
            for hh in range(hpg):
                h = g * hpg + hh
                hb = A_HEADS + h
                arg = jnp.where(lower, col_f[:, h:h + 1] - row_f[h:h + 1, :], col_b[:, hb:hb + 1] - row_b[hb:hb + 1, :])
                wgt = jnp.where(diag, dt_t[h:h + 1, :] + dt_t[hb:hb + 1, :], jnp.exp(arg))
                sc = (gmat * wgt).astype(BF16)
                ys.append(jnp.dot(sc, xs_bf[:, h * A_HEAD_DIM:(h + 1) * A_HEAD_DIM], preferred_element_type=F32))
        yf_ref[pl.ds(pl.multiple_of(seg[4] + t0, q), q), :] = (jnp.concatenate(ys, axis=1)
                                                              + carried(0, xs, bm, cm, dt, cum_f))
        tb = pl.multiple_of((n_chunks - 1 - i) * q, q)
        xsb, bmb, cmb, dtb_ = load_chunk(seg, tb)
        yb_ref[pl.ds(pl.multiple_of(seg[4] + tb, q), q), :] = carried(1, xsb, bmb, cmb, dtb_, scan_sum(1, dtb_ * aneg))

    dsk = dsk_ref[...]
    nw = nw_ref[...]
    for seg in segs:
        n_chunks = seg[3].shape[1] // q
        lax.fori_loop(0, n_chunks, lambda i, carry, seg=seg, n_chunks=n_chunks: (step(seg, i, n_chunks), carry)[1], 0,
                      unroll=math.gcd(n_chunks, SCAN_UNROLL))

    for z_ref, _, _, o_ref, base in segs:
        def fin_body(i, carry, z_ref=z_ref, o_ref=o_ref, base=base):
            t0 = pl.multiple_of(i * q, q)
            xs = xconv_ref[pl.ds(pl.multiple_of(base + t0, q), q), :inner]
            rows = pl.ds(pl.multiple_of(base + t0, q), q)
            y = yf_ref[rows, :] + yb_ref[rows, :] + dsk * xs
            y = y * _silu(z_ref[0, pl.ds(t0, q), :].astype(F32))
            o_ref[0, pl.ds(t0, q), :] = (_rms(y) * nw).astype(o_ref.dtype)
            return carry

        lax.fori_loop(0, o_ref.shape[1] // q, fin_body, 0)


def _ssd(zc, zl, xc, xl, dtc, dtl, cw, cb, dtb, aneg, dsk, nw):
    bsz, n_ctx, inner = zc.shape
    seq = zl.shape[1]
    lt = n_ctx + seq
    cd = xc.shape[-1]
    ins = (zc, zl, xc, xl, dtc, dtl)
    consts = (cw, cb, dtb, aneg, dsk, nw)
    est = 2 * lt * (2 * inner + 4 * cd + 4 * V7X_LANES + 4 * inner) + 4 * lt * (cd + inner) + 12 * 1024 * 1024
    return pl.pallas_call(
        functools.partial(_ssd_kernel, q=MIX_CHUNK),
        grid=(bsz,),
        in_specs=[_seq_spec(a) for a in ins] + [_full_spec(a) for a in consts],
        out_specs=[pl.BlockSpec((1, n_ctx, inner), lambda b: (b, 0, 0)),
                   pl.BlockSpec((1, seq, inner), lambda b: (b, 0, 0))],
        out_shape=[jax.ShapeDtypeStruct((bsz, n_ctx, inner), BF16), jax.ShapeDtypeStruct((bsz, seq, inner), BF16)],
        scratch_shapes=[pltpu.VMEM((lt, cd), F32),
                        pltpu.VMEM((lt, inner), F32),
                        pltpu.VMEM((lt, inner), F32),
                        pltpu.VMEM((2, A_GROUPS, A_STATE, inner // A_GROUPS), F32)],
        compiler_params=pltpu.CompilerParams(dimension_semantics=("arbitrary",),
                                             vmem_limit_bytes=_vmem_limit(est)),
        name="ssd_mixer",
    )(*ins, *consts)


def _hgrn_kernel(qc_ref, ql_ref, ffc_ref, ffl_ref, fbc_ref, fbl_ref, vc_ref, vl_ref, gc_ref, gl_ref,
                 lb_ref, nw_ref, oc_ref, ol_ref, of_ref, ob_ref, st_ref, *, c):
    w = C_HEADS * C_KEY
    nv = c // V7X_SUBLANES
    segs = ((qc_ref, (ffc_ref, fbc_ref), vc_ref, gc_ref, oc_ref, 0),
            (ql_ref, (ffl_ref, fbl_ref), vl_ref, gl_ref, ol_ref, oc_ref.shape[1]))
    st_ref[...] = jnp.zeros(st_ref.shape, F32)

    rowi = lax.broadcasted_iota(jnp.int32, (c, c), 0)
    coli = lax.broadcasted_iota(jnp.int32, (c, c), 1)
    pair_bits = rowi ^ coli
    hr = lax.broadcasted_iota(jnp.int32, (w, w), 0)
    hc = lax.broadcasted_iota(jnp.int32, (w, w), 1)
    same_head = (hr >> int(math.log2(C_KEY))) == (hc >> int(math.log2(C_KEY)))
    bones = jnp.where(same_head, 1.0, 0.0).astype(BF16)
    trow = lax.broadcasted_iota(jnp.int32, (c, w), 0)
    sub3 = lax.broadcasted_iota(jnp.int32, (nv, V7X_SUBLANES, w), 1)

    def boundary_small(x3, p0, s):
        b = 1
        while b <= s:
            if p0 & b:
                x3 = jnp.where((sub3 & b) != 0, x3, pltpu.roll(x3, V7X_SUBLANES - b, axis=1))
            else:
                x3 = jnp.where((sub3 & b) != 0, pltpu.roll(x3, b, axis=1), x3)
            b *= 2
        return x3

    def gates(d, seg, smp, t0):
        zf = seg[1][d][smp, pl.ds(t0, c), :]
        lb = lb_ref[d:d + 1, :]
        f = lb + (1.0 - lb) * jax.nn.sigmoid(zf)
        causal = (rowi >= coli) if d == 0 else (rowi <= coli)
        cum = _dot01_left(jnp.where(causal, 1.0, 0.0).astype(BF16), jnp.log(f))
        return 1.0 - f, cum

    def boundary(cum, d, s):
        p0 = (s - 1) if d == 0 else s
        return boundary_small(cum.reshape(nv, V7X_SUBLANES, w), p0, s).reshape(c, w)

    def carried(d, smp, qs, kk, cum, v32):
        last = c - 1 if d == 0 else 0
        cum_last = cum[last:last + 1, :]
        st = st_ref[smp, d]
        qe = (qs * jnp.exp(cum)).astype(BF16)
        kw = (kk * jnp.exp(cum_last - cum)).astype(BF16)
        upd = jnp.dot(v32.T.astype(BF16), kw, preferred_element_type=F32)
        st_ref[smp, d] = jnp.where(same_head, st * jnp.exp(cum_last) + upd, 0.0)
        return _dot_nt(qe, st.astype(BF16))

    def step(seg, smp, i, n_chunks):
        q_ref, _, v_ref, _, o_ref, base = seg
        t0 = pl.multiple_of(i * c, c)
        qs = _silu(q_ref[smp, pl.ds(t0, c), :].astype(F32))
        v = v_ref[smp, pl.ds(t0, c), :]
        v32 = v.astype(F32)
        kk0, cum0 = gates(0, seg, smp, t0)
        kk1, cum1 = gates(1, seg, smp, t0)

        out = jnp.dot((qs * (kk0 + kk1)).astype(BF16), bones, preferred_element_type=F32) * v32
        scores = [jnp.zeros((c, c), F32) for _ in range(C_HEADS)]
        s = 1
        while s < c:
            if s < V7X_SUBLANES:
                e0 = jnp.exp2(jnp.abs(cum0 - boundary(cum0, 0, s)) * NEG_LOG2E)
                e1 = jnp.exp2(jnp.abs(cum1 - boundary(cum1, 1, s)) * NEG_LOG2E)
                has_bit = (trow & s) != 0
                qm = (qs * jnp.where(has_bit, e0, e1)).astype(BF16)
                km = jnp.where(has_bit, kk1 * e1, kk0 * e0).astype(BF16)
            else:
                halves = lambda a: a.reshape(c // (2 * s), 2, s, w)
                c0, c1, q2 = halves(cum0), halves(cum1), halves(qs)
                x0 = c0 - c0[:, 0:1, s - 1:s, :]
                x1 = c1 - c1[:, 1:2, 0:1, :]
                qarg = jnp.concatenate([x1[:, 0:1], x0[:, 1:2]], axis=1)
                karg = jnp.concatenate([x0[:, 0:1], x1[:, 1:2]], axis=1)
                ksel = jnp.concatenate([halves(kk0)[:, 0:1], halves(kk1)[:, 1:2]], axis=1)
                qm = (q2 * jnp.exp2(qarg * (-NEG_LOG2E))).reshape(c, w).astype(BF16)
                km = (ksel * jnp.exp2(karg * NEG_LOG2E)).reshape(c, w).astype(BF16)
            take = pair_bits >= s
            for h in range(C_HEADS):
                sl = slice(h * C_KEY, (h + 1) * C_KEY)
                scores[h] = jnp.where(take, _dot_nt(qm[:, sl], km[:, sl]), scores[h])
            s *= 2
        out = out + jnp.concatenate(
            [jnp.dot(scores[h].astype(BF16), v[:, h * C_KEY:(h + 1) * C_KEY], preferred_element_type=F32)
             for h in range(C_HEADS)], axis=1)
        of_ref[smp, pl.ds(pl.multiple_of(base + t0, c), c), :] = out + carried(0, smp, qs, kk0, cum0, v32)
        tb = pl.multiple_of((n_chunks - 1 - i) * c, c)
        qsb = _silu(q_ref[smp, pl.ds(tb, c), :].astype(F32))
        kkb, cumb = gates(1, seg, smp, tb)
        ob_ref[smp, pl.ds(pl.multiple_of(base + tb, c), c), :] = carried(
            1, smp, qsb, kkb, cumb, v_ref[smp, pl.ds(tb, c), :].astype(F32))

    n_smp = oc_ref.shape[0]
    for seg in segs:
        n_chunks = seg[4].shape[1] // c

        def scan_body(i, carry, seg=seg, n_chunks=n_chunks):
            for smp in range(n_smp):
                step(seg, smp, i, n_chunks)
            return carry

        lax.fori_loop(0, n_chunks, scan_body, 0, unroll=math.gcd(n_chunks, SCAN_UNROLL))

    nw = nw_ref[...]
    for _, _, _, g_ref, o_ref, base in segs:
        def fin_body(i, carry, g_ref=g_ref, o_ref=o_ref, base=base):
            t0 = pl.multiple_of(i * c, c)
            for smp in range(n_smp):
                rows = pl.ds(pl.multiple_of(base + t0, c), c)
                o = of_ref[smp, rows, :] + ob_ref[smp, rows, :]
                ms = _dot01_right(o * o, bones) * (1.0 / C_KEY)
                y = o * lax.rsqrt(ms + EPS) * nw
                o_ref[smp, pl.ds(t0, c), :] = (y * _silu(g_ref[smp, pl.ds(t0, c), :].astype(F32))).astype(o_ref.dtype)
            return carry

        lax.fori_loop(0, o_ref.shape[1] // c, fin_body, 0)


def _hgrn(ctx_parts, lat_parts, lower, nw):
    qc, fc, ic, gc = ctx_parts
    ql, fl, il, gl = lat_parts
    bsz, n_ctx, w = qc.shape
    seq = ql.shape[1]
    ns = math.gcd(bsz, HG_SAMPLES_PER_STEP)
    blk = lambda a, j: pl.BlockSpec((ns, a.shape[1], w), lambda b: (b, 0, j))
    ins = (qc, ql, fc, fl, fc, fl, ic, il, gc, gl)
    specs = [blk(qc, 0), blk(ql, 0), blk(fc, 0), blk(fl, 0), blk(fc, 1), blk(fl, 1),
             blk(ic, 0), blk(il, 0), blk(gc, 0), blk(gl, 0)]
    est = ns * 2 * (n_ctx + seq) * w * (2 + 8 + 2 + 2 + 4) + 20 * 1024 * 1024
    return pl.pallas_call(
        functools.partial(_hgrn_kernel, c=HG_CHUNK),
        grid=(bsz // ns,),
        in_specs=specs + [_full_spec(lower), _full_spec(nw)],
        out_specs=[blk(qc, 0), blk(ql, 0)],
        out_shape=[jax.ShapeDtypeStruct((bsz, n_ctx, w), BF16), jax.ShapeDtypeStruct((bsz, seq, w), BF16)],
        scratch_shapes=[pltpu.VMEM((ns, n_ctx + seq, w), F32), pltpu.VMEM((ns, n_ctx + seq, w), F32),
                        pltpu.VMEM((ns, 2, w, w), F32)],
        compiler_params=pltpu.CompilerParams(dimension_semantics=("arbitrary",),
                                             vmem_limit_bytes=_vmem_limit(est)),
        name="hgrn_mixer",
    )(*ins, lower, nw)


def _s5_tables(lam_re, lam_im, log_step, b_re, b_im, c_re, c_im):
    q = S5_CHUNK
    hp = lax.Precision.HIGHEST
    lam_re = lam_re.astype(F32)
    lam_im = lam_im.astype(F32)
    step = jnp.exp(log_step.astype(F32))[..., None]
    tau = jnp.arange(q + 1, dtype=F32)[:, None, None, None]
    mag = jnp.exp(lam_re * step * tau)
    pr = mag * jnp.cos(lam_im * step * tau)
    pi = mag * jnp.sin(lam_im * step * tau)
    ar, ai = pr[1], pi[1]
    den = lam_re * lam_re + lam_im * lam_im
    nr = ar - 1.0
    kr = (nr * lam_re + ai * lam_im) / den
    ki = (ai * lam_re - nr * lam_im) / den
    b_re = b_re.astype(F32)
    b_im = b_im.astype(F32)
    br = kr[..., None] * b_re - ki[..., None] * b_im
    bi = kr[..., None] * b_im + ki[..., None] * b_re
    c_re = c_re.astype(F32)
    c_im = c_im.astype(F32)
    qr = pr[..., None] * br - pi[..., None] * bi
    qi = pr[..., None] * bi + pi[..., None] * br
    hk = jnp.einsum('dgon,tdgni->tdgoi', c_re, qr, precision=hp) - jnp.einsum('dgon,tdgni->tdgoi', c_im, qi, precision=hp)
    g, cg = b_re.shape[1], b_re.shape[3]
    width = q * cg
    strip = lambda h: h.transpose(1, 3, 0, 2).reshape(g, cg, width)
    zeros = jnp.zeros((g, cg, width - cg), F32)
    pad0 = jnp.concatenate([zeros, strip(hk[:q, 0])], axis=-1)
    pad1 = jnp.concatenate([strip(hk[:q, 1][::-1]), zeros], axis=-1)
    rows = lambda pad: jnp.stack([pad[..., (q - 1 - s) * cg:(q - 1 - s) * cg + width] for s in range(q)], axis=1)
    toe = jnp.stack([rows(pad0), rows(pad1)]).reshape(2, g, width, width)
    by_dir = lambda a: jnp.stack([a[:q, 0][::-1], a[:q, 1]])
    wst = jnp.concatenate([by_dir(qr), by_dir(qi)], axis=3)
    wst = wst.transpose(0, 2, 1, 4, 3).reshape(2, g, width, -1)
    out_pow = lambda a: jnp.stack([a[1:, 0], a[1:, 1][::-1]])
    po_r, po_i = out_pow(pr), out_pow(pi)
    w_xr = c_re[:, None] * po_r[:, :, :, None, :] - c_im[:, None] * po_i[:, :, :, None, :]
    w_xi = -c_re[:, None] * po_i[:, :, :, None, :] - c_im[:, None] * po_r[:, :, :, None, :]
    wout = jnp.concatenate([w_xr, w_xi], axis=-1)
    wout = wout.transpose(0, 2, 4, 1, 3).reshape(2, g, -1, width)
    dec = jnp.stack([jnp.concatenate([pr[q], pr[q]], axis=-1),
                     jnp.concatenate([-pi[q], pi[q]], axis=-1),
                     jnp.concatenate([pi[q], -pi[q]], axis=-1)], axis=1)
    return toe.astype(BF16), wst.astype(BF16), wout.astype(BF16), dec


def _lane_block_transpose(arrs, blk):
    n = len(arrs)
    rows, width = arrs[0].shape
    lane_blk = lax.broadcasted_iota(jnp.int32, (rows, width), 1) >> int(math.log2(blk))
    k = 1
    while k < n:
        hi_half = (lane_blk & k) != 0
        new = list(arrs)
        for a in range(n):
            if a & k == 0:
                lo_arr, hi_arr = arrs[a], arrs[a | k]
                new[a] = jnp.where(hi_half, pltpu.roll(hi_arr, blk * k, axis=1), lo_arr)
                new[a | k] = jnp.where(hi_half, hi_arr, pltpu.roll(lo_arr, width - blk * k, axis=1))
        arrs = new
        k *= 2
    return arrs


def _gelu_tanh(x):
    return 0.5 * x * (1.0 + jnp.tanh(math.sqrt(2.0 / math.pi) * (x + 0.044715 * (x * x * x))))


def _s5_kernel(uc_ref, ul_ref, toe_ref, wst_ref, wout_ref, dec_ref, sd_ref, gw_ref, gb_ref,
               oc_ref, ol_ref, z_ref, xin_ref):
    q = uc_ref.shape[2]
    ncc = uc_ref.shape[1]
    n_chunks = ncc + ul_ref.shape[1]
    ng = toe_ref.shape[1]
    half = dec_ref.shape[-1] // 2
    rows = [jnp.concatenate([uc_ref[0, :, s, :], ul_ref[0, :, s, :]], axis=0) for s in range(q)]
    ug = [a.astype(BF16) for a in _lane_block_transpose(rows, B_GROUP)]
    ys = [None] * ng
    for d in range(2):
        zs = []
        for g in range(ng):
            ys[g] = jnp.dot(ug[g], toe_ref[d, g], preferred_element_type=F32) + (0.0 if ys[g] is None else ys[g])
            zs.append(jnp.dot(ug[g], wst_ref[d, g], preferred_element_type=F32))
        z = pltpu.einshape("gcn->cgn", jnp.stack(zs))
        z_ref[0, d] = z
        z_ref[1, d] = pltpu.roll(z, half, axis=2)

    coef = [[dec_ref[d, j] for j in range(3)] for d in range(2)]

    def body(i, carry):
        nxt = []
        for d in range(2):
            if d == 0:
                c_idx = i
            else:
                c_idx = jnp.where(i < ncc, ncc - 1 - i, n_chunks - 1 - (i - ncc))
            x, xs = carry[2 * d], carry[2 * d + 1]
            a1, a2, a2s = coef[d]
            xin_ref[d, c_idx] = x
            nxt.append(a1 * x + a2 * xs + z_ref[0, d, c_idx])
            nxt.append(a1 * xs + a2s * x + z_ref[1, d, c_idx])
        return tuple(nxt)

    zero = jnp.zeros(coef[0][0].shape, F32)
    lax.fori_loop(0, n_chunks, body, (zero, zero, zero, zero), unroll=2)
    for d in range(2):
        xg = pltpu.einshape("cgn->gcn", xin_ref[d]).astype(BF16)
        for g in range(ng):
            ys[g] = ys[g] + jnp.dot(xg[g], wout_ref[d, g], preferred_element_type=F32)
    yt = _lane_block_transpose(ys, B_GROUP)
    sd = sd_ref[...]
    gw = gw_ref[...]
    gb = gb_ref[...]
    for t in range(q):
        y = _gelu_tanh(yt[t] + sd * rows[t])
        gate = jnp.dot(y.astype(BF16), gw, preferred_element_type=F32) + gb
        y = y * jax.nn.sigmoid(gate)
        oc_ref[0, :, t, :] = y[:ncc]
        ol_ref[0, :, t, :] = y[ncc:]


def _s5(pbc, pbl, toe, wst, wout, dec, s5_d, glu_w, glu_b):
    bsz, n_ctx, wd = pbc.shape
    seq = pbl.shape[1]
    q = S5_CHUNK
    ng = toe.shape[1]
    ns = wst.shape[-1]
    n_chunks = (n_ctx + seq) // q
    uc = pbc.reshape(bsz, n_ctx // q, q, wd)
    ul = pbl.reshape(bsz, seq // q, q, wd)
    consts = (toe, wst, wout, dec, s5_d, glu_w, glu_b)
    oc, ol = pl.pallas_call(
        _s5_kernel,
        grid=(bsz,),
        in_specs=[_seq_spec(uc), _seq_spec(ul)] + [_full_spec(a) for a in consts],
        out_specs=[_seq_spec(uc), _seq_spec(ul)],
        out_shape=[jax.ShapeDtypeStruct(uc.shape, F32), jax.ShapeDtypeStruct(ul.shape, F32)],
        scratch_shapes=[pltpu.VMEM((2, 2, n_chunks, ng, ns), F32), pltpu.VMEM((2, n_chunks, ng, ns), F32)],
        compiler_params=pltpu.CompilerParams(dimension_semantics=("arbitrary",),
                                             vmem_limit_bytes=_vmem_limit(40 * 1024 * 1024)),
        name="s5_mixer",
    )(uc, ul, *consts)
    return oc.reshape(bsz, n_ctx, wd), ol.reshape(bsz, seq, wd)


def _column_to_raster(t, rows):
    b, s, d = t.shape
    return t.reshape(b, GRID_W, rows, d).transpose(0, 2, 1, 3).reshape(b, s, d)


def kernel(x, c, ctx, c_ctx, mod_w, mod_b, ffn_w_in, ffn_w_out, w_in, w_out, a_conv_w, a_conv_b, a_dt_bias, a_log,
           a_d, a_norm_w, s5_lam_re, s5_lam_im, s5_log_step, s5_b_re, s5_b_im, s5_c_re, s5_c_im, s5_d, s5_glu_w,
           s5_glu_b, hg_lb_logits, hg_norm_w, final_norm_w):
    bsz, seq, dm = x.shape
    n_ctx = ctx.shape[1]
    depth = mod_w.shape[0]
    rows = seq // GRID_W
    assert n_ctx % MIX_CHUNK == 0 and seq % MIX_CHUNK == 0 and seq % GRID_W == 0
    a_inner = A_HEADS * A_HEAD_DIM
    a_conv_dim = a_conv_w.shape[-1]
    c_width = hg_norm_w.shape[-1]

    pad = (-(bsz + 1)) % V7X_SUBLANES
    cvec = jnp.concatenate([c, c_ctx[None, :], jnp.zeros((pad, dm), F32)], axis=0)
    mod_all = _mod_vectors(cvec, mod_w, mod_b)

    p_lb = jax.nn.softmax(hg_lb_logits.astype(F32), axis=0)
    lower_bounds = jnp.cumsum(p_lb, axis=0) - p_lb[:1]
    fw = final_norm_w.reshape(1, dm).astype(F32)
    toe_all, wst_all, wout_all, dec_all = jax.vmap(_s5_tables)(s5_lam_re, s5_lam_im, s5_log_step, s5_b_re, s5_b_im,
                                                               s5_c_re, s5_c_im)

    h_lat, h_ctx = x, ctx
    col_order = False
    for l in range(depth):
        last = l == depth - 1
        m_lat = mod_all[l, :bsz].reshape(bsz, N_MOD, dm)
        m_ctx = mod_all[l, bsz].reshape(1, N_MOD, dm)
        wi0, wo0 = ffn_w_in[l, 0].astype(BF16), ffn_w_out[l, 0].astype(BF16)
        wi1, wo1 = ffn_w_in[l, 1].astype(BF16), ffn_w_out[l, 1].astype(BF16)

        want_col = l % 2 == 1
        if col_order and not want_col:
            h_lat = _column_to_raster(h_lat, rows)
            col_order = False
        h_lat = _ffn(h_lat, m_lat, wi0, wo0, fw, base=0, final=False, to_column=want_col and not col_order)
        col_order = want_col
        h_ctx = _ffn(h_ctx, m_ctx, wi0, wo0, fw, base=0, final=False)

        wl = w_in[l]
        o_dt = a_inner + a_conv_dim
        o_b = o_dt + 2 * A_HEADS
        w_cat = jnp.concatenate([wl[:, :o_b], jnp.zeros((dm, V7X_LANES - 2 * A_HEADS), F32), wl[:, o_b:]],
                                axis=1).astype(BF16)
        zl, xl, dtl, pbl, ql, fl, il, gl = _inproj(h_lat, m_lat, w_cat)
        zc, xc, dtc, pbc, qc, fc, ic, gc = _inproj(h_ctx, m_ctx, w_cat)

        lane_pad = jnp.zeros((V7X_LANES - 2 * A_HEADS,), F32)
        dtb = jnp.concatenate([a_dt_bias[l].reshape(-1), lane_pad]).reshape(1, V7X_LANES)
        aneg = jnp.concatenate([-jnp.exp(a_log[l].astype(F32)).reshape(-1), lane_pad]).reshape(1, V7X_LANES)
        dsk = jnp.repeat(a_d[l].astype(F32), A_HEAD_DIM).reshape(1, a_inner)
        ya_c, ya_l = _ssd(zc, zl, xc, xl, dtc, dtl, a_conv_w[l], a_conv_b[l].reshape(1, -1), dtb, aneg, dsk,
                          a_norm_w[l].reshape(1, -1))

        yb_c, yb_l = _s5(pbc, pbl, toe_all[l], wst_all[l], wout_all[l], dec_all[l], s5_d[l].reshape(1, -1),
                         s5_glu_w[l].astype(BF16), s5_glu_b[l].reshape(1, -1))

        yc_c, yc_l = _hgrn((qc, fc, ic, gc), (ql, fl, il, gl), lower_bounds[l].astype(F32),
                           hg_norm_w[l].reshape(1, c_width))

        w_o = w_out[l].astype(BF16)
        h_lat = _ffn(h_lat, m_lat, wi1, wo1, fw, base=6, final=last, to_raster=last and col_order,
                     mixed=(ya_l, yb_l, yc_l, w_o))
        if last:
            col_order = False
        else:
            h_ctx = _ffn(h_ctx, m_ctx, wi1, wo1, fw, base=6, final=False, mixed=(ya_c, yb_c, yc_c, w_o))
    return h_lat
```
